```python
import math
import jax, jax.numpy as jnp
from jax import lax
import numpy as np

D_MODEL = 2048
BATCH = 1
SEQ = 16384
DEPTH = 4

CHUNK = 64
Q_BLOCK = 128
ATTN_WIDTH = D_MODEL // 2
SSM_WIDTH = D_MODEL - ATTN_WIDTH
HEAD_DIM = 64
N_HEADS = ATTN_WIDTH // (2 * HEAD_DIM)
SSM_GROUP_CH = 16
N_SSM_GROUPS = SSM_WIDTH // SSM_GROUP_CH
SSM_STATE = 64
D_FF = 4 * D_MODEL
IN_COLS = 3 * ATTN_WIDTH + SSM_WIDTH
DEEPNORM_ALPHA = (2.0 * DEPTH) ** 0.25
DEEPNORM_BETA = (8.0 * DEPTH) ** -0.25
NORM_EPS = 1e-5

kernel_name = 'hymba_diffattn_s5_deepnorm_trunk'


def layer_norm(x, g, b):
    xf = x.astype(jnp.float32)
    mu = jnp.mean(xf, axis=-1, keepdims=True)
    var = jnp.mean(jnp.square(xf - mu), axis=-1, keepdims=True)
    y = (xf - mu) * lax.rsqrt(var + NORM_EPS) * g.astype(jnp.float32) + b.astype(jnp.float32)
    return y.astype(x.dtype)


def group_rms_norm(x, g):
    xf = x.astype(jnp.float32)
    y = xf * lax.rsqrt(jnp.mean(jnp.square(xf), axis=-1, keepdims=True) + NORM_EPS)
    return y * g.astype(jnp.float32)


def diff_attention(q, k, v, lam, lam_init, g_attn):
    bsz, seq_len, _ = q.shape
    n_blk = seq_len // Q_BLOCK
    q = q.reshape(bsz, seq_len, N_HEADS, 2, HEAD_DIM)
    k = k.reshape(bsz, seq_len, N_HEADS, 2, HEAD_DIM)
    v = v.reshape(bsz, seq_len, N_HEADS, 2 * HEAD_DIM)
    k_t = k.transpose(0, 2, 3, 1, 4)
    v_t = v.transpose(0, 2, 1, 3)
    q_blocks = q.reshape(bsz, n_blk, Q_BLOCK, N_HEADS, 2, HEAD_DIM).transpose(1, 0, 3, 4, 2, 5)
    key_chunk = jnp.arange(seq_len) // CHUNK
    scale = HEAD_DIM ** -0.5

    def one_block(args):
        q_blk, blk = args
        s = jnp.einsum('bhsqd,bhskd->bhsqk', q_blk, k_t).astype(jnp.float32) * scale
        q_chunk = (blk * Q_BLOCK + jnp.arange(Q_BLOCK)) // CHUNK
        allowed = key_chunk[None, :] <= q_chunk[:, None]
        s = jnp.where(allowed, s, -jnp.inf)
        p = jax.nn.softmax(s, axis=-1)
        p_diff = p[:, :, 0] - lam.astype(jnp.float32) * p[:, :, 1]
        return jnp.einsum('bhqk,bhkv->bhqv', p_diff.astype(v_t.dtype), v_t)

    o = lax.map(one_block, (q_blocks, jnp.arange(n_blk)))
    o = o.transpose(1, 0, 3, 2, 4).reshape(bsz, seq_len, N_HEADS, 2 * HEAD_DIM)
    o = group_rms_norm(o, g_attn.reshape(N_HEADS, 2 * HEAD_DIM)) * (1.0 - lam_init)
    return o.reshape(bsz, seq_len, ATTN_WIDTH).astype(q.dtype)


def _complex_affine_combine(e1, e2):
    a1r, a1i, b1r, b1i = e1
    a2r, a2i, b2r, b2i = e2
    ar = a2r * a1r - a2i * a1i
    ai = a2r * a1i + a2i * a1r
    br = a2r * b1r - a2i * b1i + b2r
    bi = a2r * b1i + a2i * b1r + b2i
    return (ar, ai, br, bi)


def s5_mixer(u, lam_re, lam_im, log_dt, b_re, b_im, c_re, c_im, d_skip, glu_w, glu_b):
    f32 = jnp.float32
    bsz, seq_len, _ = u.shape
    uf = u.astype(f32).reshape(bsz, seq_len, N_SSM_GROUPS, SSM_GROUP_CH)
    dt = jnp.exp(log_dt.astype(f32))[:, None]
    lr, li = lam_re.astype(f32), lam_im.astype(f32)
    mag = jnp.exp(lr * dt)
    ab_r, ab_i = mag * jnp.cos(li * dt), mag * jnp.sin(li * dt)
    den = lr * lr + li * li
    nr, ni = ab_r - 1.0, ab_i
    fr = (nr * lr + ni * li) / den
    fi = (ni * lr - nr * li) / den
    br_, bi_ = b_re.astype(f32), b_im.astype(f32)
    bb_r = fr[..., None] * br_ - fi[..., None] * bi_
    bb_i = fr[..., None] * bi_ + fi[..., None] * br_
    bu_r = jnp.einsum('blgc,gpc->blgp', uf, bb_r)
    bu_i = jnp.einsum('blgc,gpc->blgp', uf, bb_i)
    a_r = jnp.broadcast_to(ab_r, bu_r.shape)
    a_i = jnp.broadcast_to(ab_i, bu_i.shape)
    _, _, h_r, h_i = lax.associative_scan(_complex_affine_combine, (a_r, a_i, bu_r, bu_i), axis=1)
    y = (jnp.einsum('blgp,gcp->blgc', h_r, c_re.astype(f32))
         - jnp.einsum('blgp,gcp->blgc', h_i, c_im.astype(f32)))
    y = y + d_skip.astype(f32).reshape(N_SSM_GROUPS, SSM_GROUP_CH) * uf
    y = jax.nn.gelu(y.reshape(bsz, seq_len, SSM_WIDTH))
    y = y * jax.nn.sigmoid(y @ glu_w.astype(f32) + glu_b.astype(f32))
    return y.astype(u.dtype)


def setup_inputs(seed: int = 0) -> dict:
    key = jax.random.key(seed)
    ks = jax.random.split(key, 24)
    f32 = jnp.float32
    nrm = lambda k, shape, s: jax.random.normal(k, shape, f32) * s
    x = jax.random.normal(ks[0], (BATCH, SEQ, D_MODEL), f32)
    col_scale = jnp.concatenate([
        jnp.ones((2 * ATTN_WIDTH,), f32),
        jnp.full((ATTN_WIDTH,), DEEPNORM_BETA, f32),
        jnp.ones((SSM_WIDTH,), f32)])
    w_in = nrm(ks[1], (DEPTH, D_MODEL, IN_COLS), D_MODEL ** -0.5) * col_scale
    lambda_q1 = nrm(ks[2], (DEPTH, HEAD_DIM), 0.1)
    lambda_k1 = nrm(ks[3], (DEPTH, HEAD_DIM), 0.1)
    lambda_q2 = nrm(ks[4], (DEPTH, HEAD_DIM), 0.1)
    lambda_k2 = nrm(ks[5], (DEPTH, HEAD_DIM), 0.1)
    attn_norm_g = 1.0 + nrm(ks[6], (DEPTH, ATTN_WIDTH), 0.02)
    n_idx = jnp.arange(SSM_STATE, dtype=f32)
    ssm_lambda_re = -0.5 + nrm(ks[7], (DEPTH, N_SSM_GROUPS, SSM_STATE), 0.01)
    ssm_lambda_im = math.pi * n_idx + nrm(ks[8], (DEPTH, N_SSM_GROUPS, SSM_STATE), 0.01)
    ssm_log_dt = jax.random.uniform(ks[9], (DEPTH, N_SSM_GROUPS), f32, math.log(1e-3), math.log(1e-1))
    b_s = (2.0 * SSM_GROUP_CH) ** -0.5
    ssm_b_re = nrm(ks[10], (DEPTH, N_SSM_GROUPS, SSM_STATE, SSM_GROUP_CH), b_s)
    ssm_b_im = nrm(ks[11], (DEPTH, N_SSM_GROUPS, SSM_STATE, SSM_GROUP_CH), b_s)
    c_s = SSM_STATE ** -0.5
    ssm_c_re = nrm(ks[12], (DEPTH, N_SSM_GROUPS, SSM_GROUP_CH, SSM_STATE), c_s)
    ssm_c_im = nrm(ks[13], (DEPTH, N_SSM_GROUPS, SSM_GROUP_CH, SSM_STATE), c_s)
    ssm_d = nrm(ks[14], (DEPTH, SSM_WIDTH), 1.0)
    glu_w = nrm(ks[15], (DEPTH, SSM_WIDTH, SSM_WIDTH), SSM_WIDTH ** -0.5)
    glu_b = nrm(ks[16], (DEPTH, SSM_WIDTH), 0.02)
    ssm_norm_g = 1.0 + nrm(ks[17], (DEPTH, SSM_WIDTH), 0.02)
    w_out = nrm(ks[18], (DEPTH, D_MODEL, D_MODEL), D_MODEL ** -0.5 * DEEPNORM_BETA)
    ln1_g = 1.0 + nrm(ks[19], (DEPTH, D_MODEL), 0.02)
    ln1_b = nrm(ks[20], (DEPTH, D_MODEL), 0.02)
    w_up = nrm(ks[21], (DEPTH, D_MODEL, D_FF), D_MODEL ** -0.5 * DEEPNORM_BETA)
    w_down = nrm(ks[22], (DEPTH, D_FF, D_MODEL), D_FF ** -0.5 * DEEPNORM_BETA)
    k_ln = jax.random.split(ks[23], 2)
    ln2_g = 1.0 + nrm(k_ln[0], (DEPTH, D_MODEL), 0.02)
    ln2_b = nrm(k_ln[1], (DEPTH, D_MODEL), 0.02)
    return {'x': x, 'w_in': w_in, 'lambda_q1': lambda_q1, 'lambda_k1': lambda_k1,
            'lambda_q2': lambda_q2, 'lambda_k2': lambda_k2, 'attn_norm_g': attn_norm_g,
            'ssm_lambda_re': ssm_lambda_re, 'ssm_lambda_im': ssm_lambda_im, 'ssm_log_dt': ssm_log_dt,
            'ssm_b_re': ssm_b_re, 'ssm_b_im': ssm_b_im, 'ssm_c_re': ssm_c_re, 'ssm_c_im': ssm_c_im,
            'ssm_d': ssm_d, 'glu_w': glu_w, 'glu_b': glu_b, 'ssm_norm_g': ssm_norm_g,
            'w_out': w_out, 'ln1_g': ln1_g, 'ln1_b': ln1_b, 'w_up': w_up, 'w_down': w_down,
            'ln2_g': ln2_g, 'ln2_b': ln2_b}


def reference(x, w_in, lambda_q1, lambda_k1, lambda_q2, lambda_k2, attn_norm_g,
              ssm_lambda_re, ssm_lambda_im, ssm_log_dt, ssm_b_re, ssm_b_im, ssm_c_re, ssm_c_im,
              ssm_d, glu_w, glu_b, ssm_norm_g, w_out, ln1_g, ln1_b, w_up, w_down, ln2_g, ln2_b):
    bsz, seq_len, _ = x.shape
    for l in range(DEPTH):
        lam_init = 0.8 - 0.6 * math.exp(-0.3 * l)
        lam = (jnp.exp(jnp.sum(lambda_q1[l].astype(jnp.float32) * lambda_k1[l].astype(jnp.float32)))
               - jnp.exp(jnp.sum(lambda_q2[l].astype(jnp.float32) * lambda_k2[l].astype(jnp.float32)))
               + lam_init)
        proj = x @ w_in[l]
        q = proj[..., :ATTN_WIDTH]
        k = proj[..., ATTN_WIDTH:2 * ATTN_WIDTH]
        v = proj[..., 2 * ATTN_WIDTH:3 * ATTN_WIDTH]
        u = proj[..., 3 * ATTN_WIDTH:]
        attn_out = diff_attention(q, k, v, lam, lam_init, attn_norm_g[l])
        ssm_out = s5_mixer(u, ssm_lambda_re[l], ssm_lambda_im[l], ssm_log_dt[l], ssm_b_re[l], ssm_b_im[l],
                           ssm_c_re[l], ssm_c_im[l], ssm_d[l], glu_w[l], glu_b[l])
        ssm_out = group_rms_norm(ssm_out.reshape(bsz, seq_len, N_SSM_GROUPS, SSM_GROUP_CH),
                                 ssm_norm_g[l].reshape(N_SSM_GROUPS, SSM_GROUP_CH))
        ssm_out = ssm_out.reshape(bsz, seq_len, SSM_WIDTH).astype(x.dtype)
        mix = jnp.concatenate([attn_out, ssm_out], axis=-1) @ w_out[l]
        x = layer_norm(DEEPNORM_ALPHA * x + mix, ln1_g[l], ln1_b[l])
        hid = jnp.square(jax.nn.relu(x @ w_up[l]))
        x = layer_norm(DEEPNORM_ALPHA * x + hid @ w_down[l], ln2_g[l], ln2_b[l])
    return x
```

```python
import functools
import math

import jax
import jax.numpy as jnp
from jax import lax
from jax.experimental import pallas as pl
from jax.experimental.pallas import tpu as pltpu

F32 = jnp.float32
BF16 = jnp.bfloat16

DEPTH = 4
HEAD_DIM = 64
HEAD_W = 2 * HEAD_DIM
CHUNK = 64
SSM_GROUP_CH = 16
SSM_STATE = 64
DEEPNORM_ALPHA = (2.0 * DEPTH) ** 0.25
NORM_EPS = 1e-5
MASK_VALUE = -1e30
VMEM_LIMIT = 56 * 1024 * 1024


def _cparams(*sem):
    return pltpu.CompilerParams(dimension_semantics=sem, vmem_limit_bytes=VMEM_LIMIT)


def _mm_nn_kernel(a_ref, b_ref, o_ref):
    o_ref[...] = jnp.dot(a_ref[...], b_ref[...],
                         preferred_element_type=F32).astype(o_ref.dtype)


def _matmul_nn(a, b, out_dtype, tm, tn):
    m, k = a.shape
    n = b.shape[1]
    return pl.pallas_call(
        _mm_nn_kernel,
        grid=(m // tm, n // tn),
        in_specs=[pl.BlockSpec((tm, k), lambda i, j: (i, 0)),
                  pl.BlockSpec((k, tn), lambda i, j: (0, j))],
        out_specs=pl.BlockSpec((tm, tn), lambda i, j: (i, j)),
        out_shape=jax.ShapeDtypeStruct((m, n), out_dtype),
        compiler_params=_cparams("parallel", "arbitrary"),
        name="proj_nn",
    )(a, b)


def _mm_nt_kernel(w_ref, x_ref, o_ref):
    o_ref[...] = lax.dot_general(w_ref[...], x_ref[...], (((1,), (1,)), ((), ())),
                                 preferred_element_type=F32).astype(o_ref.dtype)


def _matmul_nt(w_t, x, out_dtype, tr, tl):
    r, k = w_t.shape
    l = x.shape[0]
    return pl.pallas_call(
        _mm_nt_kernel,
        grid=(l // tl, r // tr),
        in_specs=[pl.BlockSpec((tr, k), lambda i, j: (j, 0)),
                  pl.BlockSpec((tl, k), lambda i, j: (i, 0))],
        out_specs=pl.BlockSpec((tr, tl), lambda i, j: (j, i)),
        out_shape=jax.ShapeDtypeStruct((r, l), out_dtype),
        compiler_params=_cparams("parallel", "arbitrary"),
        name="proj_nt",
    )(w_t, x)


def _attn_kernel(lq1_ref, lk1_ref, lq2_ref, lk2_ref, q_ref, k_ref, vt_ref, g_ref, o_ref,
                 m_ref, l_ref, acc_ref, *, blk, lam_init):
    i = pl.program_id(1)
    lane = lax.broadcasted_iota(jnp.int32, (blk, HEAD_W), 1)
    q = q_ref[...] * jnp.asarray(HEAD_DIM ** -0.5, BF16)
    zero = jnp.zeros_like(q)
    q_maps = (jnp.where(lane < HEAD_DIM, q, zero), jnp.where(lane >= HEAD_DIM, q, zero))

    def scores(kb, s_idx):
        return lax.dot_general(kb, q_maps[s_idx], (((1,), (1,)), ((), ())),
                               preferred_element_type=F32)

    key_chunk = lax.broadcasted_iota(jnp.int32, (blk, blk), 0) // CHUNK
    qry_chunk = lax.broadcasted_iota(jnp.int32, (blk, blk), 1) // CHUNK
    allowed = key_chunk <= qry_chunk
    start = pl.multiple_of(i * blk, blk)
    kb = k_ref[pl.ds(start, blk), :]
    vtb = vt_ref[:, pl.ds(start, blk)]
    for s_idx in range(2):
        s = jnp.where(allowed, scores(kb, s_idx), MASK_VALUE)
        m = jnp.max(s, axis=0, keepdims=True)
        p = jnp.exp(s - m)
        m_ref[s_idx] = m
        l_ref[s_idx] = jnp.sum(p, axis=0, keepdims=True)
        acc_ref[s_idx] = jnp.dot(vtb, p.astype(BF16), preferred_element_type=F32)

    def body(j, carry):
        st = pl.multiple_of(j * blk, blk)
        kbj = k_ref[pl.ds(st, blk), :]
        vtj = vt_ref[:, pl.ds(st, blk)]
        for s_idx in range(2):
            s = scores(kbj, s_idx)
            m_old = m_ref[s_idx]
            m_new = jnp.maximum(m_old, jnp.max(s, axis=0, keepdims=True))
            alpha = jnp.exp(m_old - m_new)
            p = jnp.exp(s - m_new)
            l_ref[s_idx] = alpha * l_ref[s_idx] + jnp.sum(p, axis=0, keepdims=True)
            acc_ref[s_idx] = alpha * acc_ref[s_idx] + jnp.dot(
                vtj, p.astype(BF16), preferred_element_type=F32)
            m_ref[s_idx] = m_new
        return carry

    lax.fori_loop(0, i, body, 0)

    lam = (jnp.exp(jnp.sum(lq1_ref[...] * lk1_ref[...], axis=1, keepdims=True))
           - jnp.exp(jnp.sum(lq2_ref[...] * lk2_ref[...], axis=1, keepdims=True))
           + lam_init)
    o = acc_ref[0] / l_ref[0] - lam * (acc_ref[1] / l_ref[1])
    ms = jnp.mean(o * o, axis=0, keepdims=True)
    o = o * lax.rsqrt(ms + NORM_EPS) * g_ref[...] * (1.0 - lam_init)
    o_ref[...] = o.astype(o_ref.dtype)


def _diff_attention(qk, v_t, lq1, lk1, lq2, lk2, g_col, lam_init, blk):
    l = qk.shape[0]
    a = v_t.shape[0]
    n_heads = a // HEAD_W
    lam_spec = pl.BlockSpec((1, HEAD_DIM), lambda h, i: (0, 0))
    return pl.pallas_call(
        functools.partial(_attn_kernel, blk=blk, lam_init=lam_init),
        grid=(n_heads, l // blk),
        in_specs=[lam_spec, lam_spec, lam_spec, lam_spec,
                  pl.BlockSpec((blk, HEAD_W), lambda h, i: (i, h)),
                  pl.BlockSpec((l, HEAD_W), lambda h, i: (0, n_heads + h)),
                  pl.BlockSpec((HEAD_W, l), lambda h, i: (h, 0)),
                  pl.BlockSpec((HEAD_W, 1), lambda h, i: (h, 0))],
        out_specs=pl.BlockSpec((HEAD_W, blk), lambda h, i: (h, i)),
        out_shape=jax.ShapeDtypeStruct((a, l), BF16),
        scratch_shapes=[pltpu.VMEM((2, 1, blk), F32),
                        pltpu.VMEM((2, 1, blk), F32),
                        pltpu.VMEM((2, HEAD_W, blk), F32)],
        compiler_params=_cparams("parallel", "arbitrary"),
        name="diff_attn",
    )(lq1, lk1, lq2, lk2, qk, qk, v_t, g_col)


def _complex_pow(ar, ai, e, n_bits):
    shape = jnp.broadcast_shapes(ar.shape, e.shape)
    pr = jnp.ones(shape, F32)
    pi = jnp.zeros(shape, F32)
    fr, fi = ar, ai
    for b in range(n_bits):
        bit = ((e >> b) & 1) == 1
        nr = pr * fr - pi * fi
        ni = pr * fi + pi * fr
        pr = jnp.where(bit, nr, pr)
        pi = jnp.where(bit, ni, pi)
        if b + 1 < n_bits:
            fr, fi = fr * fr - fi * fi, 2.0 * fr * fi
    return pr, pi


def _discretise(lr, li, dt):
    mag = jnp.exp(lr * dt)
    ar = mag * jnp.cos(li * dt)
    ai = mag * jnp.sin(li * dt)
    den = lr * lr + li * li
    nr = ar - 1.0
    fr = (nr * lr + ai * li) / den
    fi = (ai * lr - nr * li) / den
    return ar, ai, fr, fi


def _ssm_kernel(u_ref, ldt_ref, lr_row_ref, li_row_ref, bt_re_ref, bt_im_ref,
                lr_col_ref, li_col_ref, c_re_ref, c_im_ref, d_ref, y_ref,
                m_ref, w_re_ref, w_im_ref, s_re_ref, s_im_ref, h_re_ref, h_im_ref, *, n_chunks):
    t = CHUNK
    n_ch = SSM_GROUP_CH
    n_bits = t.bit_length() - 1
    dt = jnp.exp(ldt_ref[0])

    ar, ai, fr, fi = _discretise(lr_row_ref[0], li_row_ref[0], dt)
    bbt_r = fr * bt_re_ref[0] - fi * bt_im_ref[0]
    bbt_i = fr * bt_im_ref[0] + fi * bt_re_ref[0]
    rev = (t - 1) - lax.broadcasted_iota(jnp.int32, (t, 1), 0)
    pr, pi = _complex_pow(ar, ai, rev, n_bits)
    for c in range(n_ch):
        br = bbt_r[c:c + 1, :]
        bi = bbt_i[c:c + 1, :]
        w_re_ref[c * t:(c + 1) * t, :] = (pr * br - pi * bi).astype(BF16)
        w_im_ref[c * t:(c + 1) * t, :] = (pr * bi + pi * br).astype(BF16)
    at_r, at_i = _complex_pow(ar, ai, jnp.full((1, 1), t, jnp.int32), n_bits + 1)

    ar_c, ai_c, _, _ = _discretise(lr_col_ref[0], li_col_ref[0], dt)
    tau = lax.broadcasted_iota(jnp.int32, (1, n_ch * t), 1) % t
    qr, qi = _complex_pow(ar_c, ai_c, tau, n_bits)
    c_re = c_re_ref[0]
    c_im = c_im_ref[0]
    ca_r = c_re * qr - c_im * qi
    ca_i = c_re * qi + c_im * qr
    z = (jnp.dot(bbt_r, ca_r, preferred_element_type=F32, precision=lax.Precision.HIGHEST)
         - jnp.dot(bbt_i, ca_i, preferred_element_type=F32, precision=lax.Precision.HIGHEST))
    v_r = (ca_r * ar_c - ca_i * ai_c).astype(BF16)
    v_i = (ca_r * ai_c + ca_i * ar_c).astype(BF16)

    s_row = lax.broadcasted_iota(jnp.int32, (t, 1), 0)
    causal = tau >= s_row
    for c in range(n_ch):
        strip = jnp.broadcast_to(z[c:c + 1, :], (t, n_ch * t))
        shifted = pltpu.roll(strip, 0, 1, stride=1, stride_axis=0)
        m_ref[c * t:(c + 1) * t, :] = jnp.where(causal, shifted, 0.0).astype(BF16)

    u = u_ref[0]
    ub = u.astype(BF16)
    y = jnp.dot(ub, m_ref[...], preferred_element_type=F32)
    s_re_ref[...] = jnp.dot(ub, w_re_ref[...], preferred_element_type=F32)
    s_im_ref[...] = jnp.dot(ub, w_im_ref[...], preferred_element_type=F32)

    def step(r, carry):
        hr, hi = carry
        h_re_ref[pl.ds(r, 1), :] = hr
        h_im_ref[pl.ds(r, 1), :] = hi
        sr = s_re_ref[pl.ds(r, 1), :]
        si = s_im_ref[pl.ds(r, 1), :]
        return (at_r * hr - at_i * hi + sr, at_r * hi + at_i * hr + si)

    zero = jnp.zeros((1, SSM_STATE), F32)
    lax.fori_loop(0, n_chunks, step, (zero, zero))

    y = y + jnp.dot(h_re_ref[...].astype(BF16), v_r, preferred_element_type=F32)
    y = y - jnp.dot(h_im_ref[...].astype(BF16), v_i, preferred_element_type=F32)
    y_ref[0] = y + d_ref[0] * u


def _ssm_core(u_g, log_dt, lam_re, lam_im, b_re, b_im, c_re, c_im, d_skip):
    g, n_chunks, w = u_g.shape
    p = SSM_STATE
    c = SSM_GROUP_CH
    ldt = log_dt.reshape(g, 1, 1)
    lr_row = lam_re.reshape(g, 1, p)
    li_row = lam_im.reshape(g, 1, p)
    bt_re = b_re.transpose(0, 2, 1)
    bt_im = b_im.transpose(0, 2, 1)
    lr_col = lam_re.reshape(g, p, 1)
    li_col = lam_im.reshape(g, p, 1)
    c_rep_re = jnp.repeat(c_re.transpose(0, 2, 1), CHUNK, axis=2)
    c_rep_im = jnp.repeat(c_im.transpose(0, 2, 1), CHUNK, axis=2)
    d_rep = jnp.repeat(d_skip.reshape(g, 1, c), CHUNK, axis=2)

    def spec(shape):
        return pl.BlockSpec((1,) + shape, lambda i: (i, 0, 0))

    return pl.pallas_call(
        functools.partial(_ssm_kernel, n_chunks=n_chunks),
        grid=(g,),
        in_specs=[spec((n_chunks, w)), spec((1, 1)), spec((1, p)), spec((1, p)),
                  spec((c, p)), spec((c, p)), spec((p, 1)), spec((p, 1)),
                  spec((p, w)), spec((p, w)), spec((1, w))],
        out_specs=spec((n_chunks, w)),
        out_shape=jax.ShapeDtypeStruct((g, n_chunks, w), F32),
        scratch_shapes=[pltpu.VMEM((w, w), BF16),
                        pltpu.VMEM((w, p), BF16), pltpu.VMEM((w, p), BF16),
                        pltpu.VMEM((n_chunks, p), F32), pltpu.VMEM((n_chunks, p), F32),
                        pltpu.VMEM((n_chunks, p), F32), pltpu.VMEM((n_chunks, p), F32)],
        compiler_params=_cparams("parallel"),
        name="ssm_core",
    )(u_g, ldt, lr_row, li_row, bt_re, bt_im, lr_col, li_col, c_rep_re, c_rep_im, d_rep)


def _ssm_post_kernel(y_ref, w_ref, b_ref, g_ref, o_ref):
    y = y_ref[...]
    k0 = math.sqrt(2.0 / math.pi)
    y = 0.5 * y * (1.0 + jnp.tanh(k0 * (y + 0.044715 * (y * y * y))))
    z = jnp.dot(w_ref[...], y.astype(BF16), preferred_element_type=F32) + b_ref[...]
    o = y * (1.0 / (1.0 + jnp.exp(-z)))
    w, tl = o.shape
    o3 = o.reshape(w // SSM_GROUP_CH, SSM_GROUP_CH, tl)
    ms = jnp.mean(o3 * o3, axis=1, keepdims=True)
    o3 = o3 * lax.rsqrt(ms + NORM_EPS)
    o_ref[...] = (o3.reshape(w, tl) * g_ref[...]).astype(o_ref.dtype)


def _ssm_post(y_t, glu_w_t, glu_b_col, g_col, tl):
    w, l = y_t.shape
    return pl.pallas_call(
        _ssm_post_kernel,
        grid=(l // tl,),
        in_specs=[pl.BlockSpec((w, tl), lambda i: (0, i)),
                  pl.BlockSpec((w, w), lambda i: (0, 0)),
                  pl.BlockSpec((w, 1), lambda i: (0, 0)),
                  pl.BlockSpec((w, 1), lambda i: (0, 0))],
        out_specs=pl.BlockSpec((w, tl), lambda i: (0, i)),
        out_shape=jax.ShapeDtypeStruct((w, l), BF16),
        compiler_params=_cparams("parallel"),
        name="ssm_post",
    )(y_t, glu_w_t, glu_b_col, g_col)


def _layer_norm(y, g, b):
    mu = jnp.mean(y, axis=-1, keepdims=True)
    yc = y - mu
    var = jnp.mean(yc * yc, axis=-1, keepdims=True)
    return yc * lax.rsqrt(var + NORM_EPS) * g + b


def _out_proj_kernel(a_ref, s_ref, wa_ref, ws_ref, x_ref, g_ref, b_ref, o_ref, ob_ref):
    tn_dims = (((0,), (0,)), ((), ()))
    mix = lax.dot_general(a_ref[...], wa_ref[...], tn_dims, preferred_element_type=F32)
    mix = mix + lax.dot_general(s_ref[...], ws_ref[...], tn_dims, preferred_element_type=F32)
    y = _layer_norm(DEEPNORM_ALPHA * x_ref[...] + mix, g_ref[...], b_ref[...])
    o_ref[...] = y
    ob_ref[...] = y.astype(BF16)


def _out_proj_ln(attn_t, ssm_t, w_a, w_s, x, g, b, tm):
    l, d = x.shape
    a = attn_t.shape[0]
    s = ssm_t.shape[0]
    row = pl.BlockSpec((1, d), lambda i: (0, 0))
    return pl.pallas_call(
        _out_proj_kernel,
        grid=(l // tm,),
        in_specs=[pl.BlockSpec((a, tm), lambda i: (0, i)),
                  pl.BlockSpec((s, tm), lambda i: (0, i)),
                  pl.BlockSpec((a, d), lambda i: (0, 0)),
                  pl.BlockSpec((s, d), lambda i: (0, 0)),
                  pl.BlockSpec((tm, d), lambda i: (i, 0)), row, row],
        out_specs=[pl.BlockSpec((tm, d), lambda i: (i, 0)),
                   pl.BlockSpec((tm, d), lambda i: (i, 0))],
        out_shape=[jax.ShapeDtypeStruct((l, d), F32), jax.ShapeDtypeStruct((l, d), BF16)],
        compiler_params=_cparams("parallel"),
        name="out_proj_ln",
    )(attn_t, ssm_t, w_a, w_s, x, g, b)


def _mlp_kernel(xb_ref, x_ref, wu_ref, wd_ref, g_ref, b_ref, o_ref, ob_ref, acc_ref):
    j = pl.program_id(1)
    h = jnp.dot(xb_ref[...], wu_ref[...], preferred_element_type=F32)
    h = jnp.maximum(h, 0.0)
    h = (h * h).astype(BF16)
    part = jnp.dot(h, wd_ref[...], preferred_element_type=F32)

    @pl.when(j == 0)
    def _():
        acc_ref[...] = part

    @pl.when(j > 0)
    def _():
        acc_ref[...] += part

    @pl.when(j == pl.num_programs(1) - 1)
    def _():
        y = _layer_norm(DEEPNORM_ALPHA * x_ref[...] + acc_ref[...], g_ref[...], b_ref[...])
        o_ref[...] = y
        ob_ref[...] = y.astype(BF16)


def _mlp_ln(xb, x, w_up, w_down, g, b, tm, tf):
    l, d = x.shape
    f = w_up.shape[1]
    row = pl.BlockSpec((1, d), lambda i, j: (0, 0))
    return pl.pallas_call(
        _mlp_kernel,
        grid=(l // tm, f // tf),
        in_specs=[pl.BlockSpec((tm, d), lambda i, j: (i, 0)),
                  pl.BlockSpec((tm, d), lambda i, j: (i, 0)),
                  pl.BlockSpec((d, tf), lambda i, j: (0, j)),
                  pl.BlockSpec((tf, d), lambda i, j: (j, 0)), row, row],
        out_specs=[pl.BlockSpec((tm, d), lambda i, j: (i, 0)),
                   pl.BlockSpec((tm, d), lambda i, j: (i, 0))],
        out_shape=[jax.ShapeDtypeStruct((l, d), F32), jax.ShapeDtypeStruct((l, d), BF16)],
        scratch_shapes=[pltpu.VMEM((tm, d), F32)],
        compiler_params=_cparams("parallel", "arbitrary"),
        name="mlp_ln",
    )(xb, x, w_up, w_down, g, b)


def _pick(n, pref):
    while n % pref:
        pref //= 2
    return pref


def kernel(x, w_in, lambda_q1, lambda_k1, lambda_q2, lambda_k2, attn_norm_g, ssm_lambda_re, ssm_lambda_im, ssm_log_dt, ssm_b_re, ssm_b_im, ssm_c_re, ssm_c_im, ssm_d, glu_w, glu_b, ssm_norm_g, w_out, ln1_g, ln1_b, w_up, w_down, ln2_g, ln2_b):
    bsz, seq, d = x.shape
    depth = w_in.shape[0]
    attn_w = attn_norm_g.shape[1]
    ssm_w = ssm_d.shape[1]
    n_groups = ssm_w // SSM_GROUP_CH
    assert bsz == 1 and seq % CHUNK == 0
    n_chunks = seq // CHUNK

    blk = _pick(seq, 512)
    xf = x.reshape(seq, d)
    xb = xf.astype(BF16)
    for l in range(depth):
        lam_init = 0.8 - 0.6 * math.exp(-0.3 * l)
        w_qk = w_in[l][:, :2 * attn_w].astype(BF16)
        w_v_t = w_in[l][:, 2 * attn_w:3 * attn_w].T.astype(BF16)
        w_u_t = w_in[l][:, 3 * attn_w:].T.astype(BF16)

        qk = _matmul_nn(xb, w_qk, BF16, _pick(seq, 1024), 1024)
        v_t = _matmul_nt(w_v_t, xb, BF16, attn_w, _pick(seq, 1024))
        u_t = _matmul_nt(w_u_t, xb, F32, ssm_w, _pick(seq, 1024))

        attn_t = _diff_attention(
            qk, v_t, lambda_q1[l].reshape(1, -1), lambda_k1[l].reshape(1, -1),
            lambda_q2[l].reshape(1, -1), lambda_k2[l].reshape(1, -1),
            attn_norm_g[l].reshape(attn_w, 1), lam_init, blk)

        u_g = (u_t.reshape(n_groups, SSM_GROUP_CH, n_chunks, CHUNK)
               .transpose(0, 2, 1, 3).reshape(n_groups, n_chunks, SSM_GROUP_CH * CHUNK))
        y_g = _ssm_core(u_g, ssm_log_dt[l], ssm_lambda_re[l], ssm_lambda_im[l],
                        ssm_b_re[l], ssm_b_im[l], ssm_c_re[l], ssm_c_im[l], ssm_d[l])
        y_t = (y_g.reshape(n_groups, n_chunks, SSM_GROUP_CH, CHUNK)
               .transpose(0, 2, 1, 3).reshape(ssm_w, seq))
        ssm_t = _ssm_post(y_t, glu_w[l].T.astype(BF16), glu_b[l].reshape(ssm_w, 1),
                          ssm_norm_g[l].reshape(ssm_w, 1), _pick(seq, 512))

        w_o = w_out[l].astype(BF16)
        xf, xb = _out_proj_ln(attn_t, ssm_t, w_o[:attn_w], w_o[attn_w:], xf,
                              ln1_g[l].reshape(1, d), ln1_b[l].reshape(1, d), _pick(seq, 512))
        xf, xb = _mlp_ln(xb, xf, w_up[l].astype(BF16), w_down[l].astype(BF16),
                         ln2_g[l].reshape(1, d), ln2_b[l].reshape(1, d), _pick(seq, 512), 1024)
    return xf.reshape(bsz, seq, d)
```

```python
import functools
import math

import jax
import jax.numpy as jnp
from jax import lax
from jax.experimental import pallas as pl
from jax.experimental.pallas import tpu as pltpu

F32 = jnp.float32
BF16 = jnp.bfloat16

DEPTH = 4
HEAD_DIM = 64
HEAD_W = 2 * HEAD_DIM
CHUNK = 64
SSM_GROUP_CH = 16
SSM_STATE = 64
DEEPNORM_ALPHA = (2.0 * DEPTH) ** 0.25
NORM_EPS = 1e-5
MASK_VALUE = -1e30
QK_SCALE_LOG2E = HEAD_DIM ** -0.5 * math.log2(math.e)
ONES_ROWS = 16
VMEM_LIMIT = 56 * 1024 * 1024


def _cparams(*sem):
    return pltpu.CompilerParams(dimension_semantics=sem, vmem_limit_bytes=VMEM_LIMIT)


def _mm_qk_kernel(a_ref, b_ref, o_ref, *, q_blocks):
    acc = jnp.dot(a_ref[...], b_ref[...], preferred_element_type=F32)
    scale = jnp.where(pl.program_id(1) < q_blocks, QK_SCALE_LOG2E, 1.0)
    o_ref[...] = (acc * scale).astype(o_ref.dtype)


def _matmul_qk(a, b, out_dtype, tm, tn, q_cols):
    m, k = a.shape
    n = b.shape[1]
    return pl.pallas_call(
        functools.partial(_mm_qk_kernel, q_blocks=q_cols // tn),
        grid=(m // tm, n // tn),
        in_specs=[pl.BlockSpec((tm, k), lambda i, j: (i, 0)),
                  pl.BlockSpec((k, tn), lambda i, j: (0, j))],
        out_specs=pl.BlockSpec((tm, tn), lambda i, j: (i, j)),
        out_shape=jax.ShapeDtypeStruct((m, n), out_dtype),
        compiler_params=_cparams("parallel", "arbitrary"),
        name="proj_nn",
    )(a, b)


def _mm_nt_kernel(w_ref, x_ref, o_ref):
    o_ref[...] = lax.dot_general(w_ref[...], x_ref[...], (((1,), (1,)), ((), ())),
                                 preferred_element_type=F32).astype(o_ref.dtype)


def _matmul_nt(w_t, x, out_dtype, tr, tl):
    r, k = w_t.shape
    l = x.shape[0]
    return pl.pallas_call(
        _mm_nt_kernel,
        grid=(l // tl, r // tr),
        in_specs=[pl.BlockSpec((tr, k), lambda i, j: (j, 0)),
                  pl.BlockSpec((tl, k), lambda i, j: (i, 0))],
        out_specs=pl.BlockSpec((tr, tl), lambda i, j: (j, i)),
        out_shape=jax.ShapeDtypeStruct((r, l), out_dtype),
        compiler_params=_cparams("parallel", "arbitrary"),
        name="proj_nt",
    )(w_t, x)


def _attn_kernel(lq1_ref, lk1_ref, lq2_ref, lk2_ref, q_ref, k_ref, vt_ref, g_ref, o_ref,
                 qz_ref, s_ref, mb_ref, m_ref, acc_ref, *, blk, lam_init):
    i = pl.program_id(1)
    nt_dims = (((1,), (1,)), ((), ()))

    lane = lax.broadcasted_iota(jnp.int32, (blk, HEAD_W), 1)
    q = q_ref[...]
    zero = jnp.zeros_like(q)
    qz_ref[0] = jnp.where(lane < HEAD_DIM, q, zero)
    qz_ref[1] = jnp.where(lane >= HEAD_DIM, q, zero)
    m_ref[...] = jnp.full(m_ref.shape, MASK_VALUE, F32)
    acc_ref[...] = jnp.zeros(acc_ref.shape, F32)
    ones = jnp.ones((ONES_ROWS, blk), BF16)

    def stage_a(block, slot, diagonal):
        st = pl.multiple_of(block * blk, blk)
        kb = k_ref[pl.ds(st, blk), :]
        for mp in range(2):
            s = lax.dot_general(kb, qz_ref[mp], nt_dims, preferred_element_type=F32)
            if diagonal:
                key_chunk = lax.broadcasted_iota(jnp.int32, (blk, blk), 0) // CHUNK
                qry_chunk = lax.broadcasted_iota(jnp.int32, (blk, blk), 1) // CHUNK
                s = jnp.where(key_chunk <= qry_chunk, s, MASK_VALUE)
            s_ref[slot, mp] = s
            mb_ref[slot, mp] = jnp.max(s, axis=0, keepdims=True)

    def stage_b(block, slot):
        st = pl.multiple_of(block * blk, blk)
        v_ext = jnp.concatenate([vt_ref[:, pl.ds(st, blk)], ones], axis=0)
        for mp in range(2):
            m_old = m_ref[mp]
            m_new = jnp.maximum(m_old, mb_ref[slot, mp])
            alpha = jnp.exp2(m_old - m_new)
            p = jnp.exp2(s_ref[slot, mp] - m_new).astype(BF16)
            acc_ref[mp] = alpha * acc_ref[mp] + jnp.dot(v_ext, p, preferred_element_type=F32)
            m_ref[mp] = m_new

    stage_a(i, 0, True)

    def pair(t, carry):
        in_slot0 = jnp.where(t == 0, i, 2 * t - 1)
        stage_a(2 * t, 1, False)
        stage_b(in_slot0, 0)
        stage_a(2 * t + 1, 0, False)
        stage_b(2 * t, 1)
        return carry

    n_pairs = i // 2
    lax.fori_loop(0, n_pairs, pair, 0)
    in_slot0 = jnp.where(n_pairs == 0, i, 2 * n_pairs - 1)
    odd = i % 2 == 1

    @pl.when(odd)
    def _():
        stage_a(i - 1, 1, False)

    stage_b(in_slot0, 0)

    @pl.when(odd)
    def _():
        stage_b(i - 1, 1)

    lam = (jnp.exp(jnp.sum(lq1_ref[...] * lk1_ref[...], axis=1, keepdims=True))
           - jnp.exp(jnp.sum(lq2_ref[...] * lk2_ref[...], axis=1, keepdims=True))
           + lam_init)
    o1 = acc_ref[0, :HEAD_W, :] / acc_ref[0, HEAD_W:HEAD_W + 1, :]
    o2 = acc_ref[1, :HEAD_W, :] / acc_ref[1, HEAD_W:HEAD_W + 1, :]
    o = o1 - lam * o2
    ms = jnp.mean(o * o, axis=0, keepdims=True)
    o = o * lax.rsqrt(ms + NORM_EPS) * g_ref[...] * (1.0 - lam_init)
    o_ref[...] = o.astype(o_ref.dtype)


def _diff_attention(qk, v_t, lq1, lk1, lq2, lk2, g_col, lam_init, blk):
    l = qk.shape[0]
    a = v_t.shape[0]
    n_heads = a // HEAD_W
    lam_spec = pl.BlockSpec((1, HEAD_DIM), lambda h, i: (0, 0))
    return pl.pallas_call(
        functools.partial(_attn_kernel, blk=blk, lam_init=lam_init),
        grid=(n_heads, l // blk),
        in_specs=[lam_spec, lam_spec, lam_spec, lam_spec,
                  pl.BlockSpec((blk, HEAD_W), lambda h, i: (i, h)),
                  pl.BlockSpec((l, HEAD_W), lambda h, i: (0, n_heads + h)),
                  pl.BlockSpec((HEAD_W, l), lambda h, i: (h, 0)),
                  pl.BlockSpec((HEAD_W, 1), lambda h, i: (h, 0))],
        out_specs=pl.BlockSpec((HEAD_W, blk), lambda h, i: (h, i)),
        out_shape=jax.ShapeDtypeStruct((a, l), BF16),
        scratch_shapes=[pltpu.VMEM((2, blk, HEAD_W), BF16),
                        pltpu.VMEM((2, 2, blk, blk), F32),
                        pltpu.VMEM((2, 2, 1, blk), F32),
                        pltpu.VMEM((2, 1, blk), F32),
                        pltpu.VMEM((2, HEAD_W + ONES_ROWS, blk), F32)],
        compiler_params=_cparams("parallel", "arbitrary"),
        name="diff_attn",
    )(lq1, lk1, lq2, lk2, qk, qk, v_t, g_col)


def _complex_pow(ar, ai, e, n_bits):
    shape = jnp.broadcast_shapes(ar.shape, e.shape)
    pr = jnp.ones(shape, F32)
    pi = jnp.zeros(shape, F32)
    fr, fi = ar, ai
    for b in range(n_bits):
        bit = ((e >> b) & 1) == 1
        nr = pr * fr - pi * fi
        ni = pr * fi + pi * fr
        pr = jnp.where(bit, nr, pr)
        pi = jnp.where(bit, ni, pi)
        if b + 1 < n_bits:
            fr, fi = fr * fr - fi * fi, 2.0 * fr * fi
    return pr, pi


def _discretise(lr, li, dt):
    mag = jnp.exp(lr * dt)
    ar = mag * jnp.cos(li * dt)
    ai = mag * jnp.sin(li * dt)
    den = lr * lr + li * li
    nr = ar - 1.0
    fr = (nr * lr + ai * li) / den
    fi = (ai * lr - nr * li) / den
    return ar, ai, fr, fi


def _ssm_kernel(u_ref, ldt_ref, lr_row_ref, li_row_ref, bt_re_ref, bt_im_ref,
                lr_col_ref, li_col_ref, c_re_ref, c_im_ref, d_ref, y_ref,
                m_ref, w_re_ref, w_im_ref, s_re_ref, s_im_ref, h_re_ref, h_im_ref, *, n_chunks):
    t = CHUNK
    n_ch = SSM_GROUP_CH
    n_bits = t.bit_length() - 1
    dt = jnp.exp(ldt_ref[0])

    ar, ai, fr, fi = _discretise(lr_row_ref[0], li_row_ref[0], dt)
    bbt_r = fr * bt_re_ref[0] - fi * bt_im_ref[0]
    bbt_i = fr * bt_im_ref[0] + fi * bt_re_ref[0]
    rev = (t - 1) - lax.broadcasted_iota(jnp.int32, (t, 1), 0)
    pr, pi = _complex_pow(ar, ai, rev, n_bits)
    for c in range(n_ch):
        br = bbt_r[c:c + 1, :]
        bi = bbt_i[c:c + 1, :]
        w_re_ref[c * t:(c + 1) * t, :] = (pr * br - pi * bi).astype(BF16)
        w_im_ref[c * t:(c + 1) * t, :] = (pr * bi + pi * br).astype(BF16)
    at_r, at_i = _complex_pow(ar, ai, jnp.full((1, 1), t, jnp.int32), n_bits + 1)

    ar_c, ai_c, _, _ = _discretise(lr_col_ref[0], li_col_ref[0], dt)
    tau = lax.broadcasted_iota(jnp.int32, (1, n_ch * t), 1) % t
    qr, qi = _complex_pow(ar_c, ai_c, tau, n_bits)
    c_re = c_re_ref[0]
    c_im = c_im_ref[0]
    ca_r = c_re * qr - c_im * qi
    ca_i = c_re * qi + c_im * qr
    z = (jnp.dot(bbt_r, ca_r, preferred_element_type=F32, precision=lax.Precision.HIGHEST)
         - jnp.dot(bbt_i, ca_i, preferred_element_type=F32, precision=lax.Precision.HIGHEST))
    v_r = (ca_r * ar_c - ca_i * ai_c).astype(BF16)
    v_i = (ca_r * ai_c + ca_i * ar_c).astype(BF16)

    s_row = lax.broadcasted_iota(jnp.int32, (t, 1), 0)
    causal = tau >= s_row
    for c in range(n_ch):
        strip = jnp.broadcast_to(z[c:c + 1, :], (t, n_ch * t))
        shifted = pltpu.roll(strip, 0, 1, stride=1, stride_axis=0)
        m_ref[c * t:(c + 1) * t, :] = jnp.where(causal, shifted, 0.0).astype(BF16)

    u = u_ref[0]
    ub = u.astype(BF16)
    y = jnp.dot(ub, m_ref[...], preferred_element_type=F32)
    s_re_ref[...] = jnp.dot(ub, w_re_ref[...], preferred_element_type=F32)
    s_im_ref[...] = jnp.dot(ub, w_im_ref[...], preferred_element_type=F32)

    def step(r, carry):
        hr, hi = carry
        h_re_ref[pl.ds(r, 1), :] = hr
        h_im_ref[pl.ds(r, 1), :] = hi
        sr = s_re_ref[pl.ds(r, 1), :]
        si = s_im_ref[pl.ds(r, 1), :]
        return (at_r * hr - at_i * hi + sr, at_r * hi + at_i * hr + si)

    zero = jnp.zeros((1, SSM_STATE), F32)
    lax.fori_loop(0, n_chunks, step, (zero, zero))

    y = y + jnp.dot(h_re_ref[...].astype(BF16), v_r, preferred_element_type=F32)
    y = y - jnp.dot(h_im_ref[...].astype(BF16), v_i, preferred_element_type=F32)
    y_ref[0] = y + d_ref[0] * u


def _ssm_core(u_g, log_dt, lam_re, lam_im, b_re, b_im, c_re, c_im, d_skip):
    g, n_chunks, w = u_g.shape
    p = SSM_STATE
    c = SSM_GROUP_CH
    ldt = log_dt.reshape(g, 1, 1)
    lr_row = lam_re.reshape(g, 1, p)
    li_row = lam_im.reshape(g, 1, p)
    bt_re = b_re.transpose(0, 2, 1)
    bt_im = b_im.transpose(0, 2, 1)
    lr_col = lam_re.reshape(g, p, 1)
    li_col = lam_im.reshape(g, p, 1)
    c_rep_re = jnp.repeat(c_re.transpose(0, 2, 1), CHUNK, axis=2)
    c_rep_im = jnp.repeat(c_im.transpose(0, 2, 1), CHUNK, axis=2)
    d_rep = jnp.repeat(d_skip.reshape(g, 1, c), CHUNK, axis=2)

    def spec(shape):
        return pl.BlockSpec((1,) + shape, lambda i: (i, 0, 0))

    return pl.pallas_call(
        functools.partial(_ssm_kernel, n_chunks=n_chunks),
        grid=(g,),
        in_specs=[spec((n_chunks, w)), spec((1, 1)), spec((1, p)), spec((1, p)),
                  spec((c, p)), spec((c, p)), spec((p, 1)), spec((p, 1)),
                  spec((p, w)), spec((p, w)), spec((1, w))],
        out_specs=spec((n_chunks, w)),
        out_shape=jax.ShapeDtypeStruct((g, n_chunks, w), F32),
        scratch_shapes=[pltpu.VMEM((w, w), BF16),
                        pltpu.VMEM((w, p), BF16), pltpu.VMEM((w, p), BF16),
                        pltpu.VMEM((n_chunks, p), F32), pltpu.VMEM((n_chunks, p), F32),
                        pltpu.VMEM((n_chunks, p), F32), pltpu.VMEM((n_chunks, p), F32)],
        compiler_params=_cparams("parallel"),
        name="ssm_core",
    )(u_g, ldt, lr_row, li_row, bt_re, bt_im, lr_col, li_col, c_rep_re, c_rep_im, d_rep)


def _ssm_post_kernel(y_ref, w_ref, b_ref, g_ref, o_ref):
    y = y_ref[...]
    k0 = math.sqrt(2.0 / math.pi)
    y = 0.5 * y * (1.0 + jnp.tanh(k0 * (y + 0.044715 * (y * y * y))))
    z = jnp.dot(w_ref[...], y.astype(BF16), preferred_element_type=F32) + b_ref[...]
    o = y * (1.0 / (1.0 + jnp.exp(-z)))
    w, tl = o.shape
    o3 = o.reshape(w // SSM_GROUP_CH, SSM_GROUP_CH, tl)
    ms = jnp.mean(o3 * o3, axis=1, keepdims=True)
    o3 = o3 * lax.rsqrt(ms + NORM_EPS)
    o_ref[...] = (o3.reshape(w, tl) * g_ref[...]).astype(o_ref.dtype)


def _ssm_post(y_t, glu_w_t, glu_b_col, g_col, tl):
    w, l = y_t.shape
    return pl.pallas_call(
        _ssm_post_kernel,
        grid=(l // tl,),
        in_specs=[pl.BlockSpec((w, tl), lambda i: (0, i)),
                  pl.BlockSpec((w, w), lambda i: (0, 0)),
                  pl.BlockSpec((w, 1), lambda i: (0, 0)),
                  pl.BlockSpec((w, 1), lambda i: (0, 0))],
        out_specs=pl.BlockSpec((w, tl), lambda i: (0, i)),
        out_shape=jax.ShapeDtypeStruct((w, l), BF16),
        compiler_params=_cparams("parallel"),
        name="ssm_post",
    )(y_t, glu_w_t, glu_b_col, g_col)


def _layer_norm(y, g, b):
    mu = jnp.mean(y, axis=-1, keepdims=True)
    yc = y - mu
    var = jnp.mean(yc * yc, axis=-1, keepdims=True)
    return yc * lax.rsqrt(var + NORM_EPS) * g + b


def _out_proj_kernel(a_ref, s_ref, wa_ref, ws_ref, x_ref, g_ref, b_ref, o_ref, ob_ref):
    tn_dims = (((0,), (0,)), ((), ()))
    mix = lax.dot_general(a_ref[...], wa_ref[...], tn_dims, preferred_element_type=F32)
    mix = mix + lax.dot_general(s_ref[...], ws_ref[...], tn_dims, preferred_element_type=F32)
    y = _layer_norm(DEEPNORM_ALPHA * x_ref[...] + mix, g_ref[...], b_ref[...])
    o_ref[...] = y
    ob_ref[...] = y.astype(BF16)


def _out_proj_ln(attn_t, ssm_t, w_a, w_s, x, g, b, tm):
    l, d = x.shape
    a = attn_t.shape[0]
    s = ssm_t.shape[0]
    row = pl.BlockSpec((1, d), lambda i: (0, 0))
    return pl.pallas_call(
        _out_proj_kernel,
        grid=(l // tm,),
        in_specs=[pl.BlockSpec((a, tm), lambda i: (0, i)),
                  pl.BlockSpec((s, tm), lambda i: (0, i)),
                  pl.BlockSpec((a, d), lambda i: (0, 0)),
                  pl.BlockSpec((s, d), lambda i: (0, 0)),
                  pl.BlockSpec((tm, d), lambda i: (i, 0)), row, row],
        out_specs=[pl.BlockSpec((tm, d), lambda i: (i, 0)),
                   pl.BlockSpec((tm, d), lambda i: (i, 0))],
        out_shape=[jax.ShapeDtypeStruct((l, d), F32), jax.ShapeDtypeStruct((l, d), BF16)],
        compiler_params=_cparams("parallel"),
        name="out_proj_ln",
    )(attn_t, ssm_t, w_a, w_s, x, g, b)


def _mlp_kernel(xb_ref, x_ref, wu_ref, wd_ref, g_ref, b_ref, o_ref, ob_ref, acc_ref):
    j = pl.program_id(1)
    h = jnp.dot(xb_ref[...], wu_ref[...], preferred_element_type=F32)
    h = jnp.maximum(h, 0.0)
    h = (h * h).astype(BF16)
    part = jnp.dot(h, wd_ref[...], preferred_element_type=F32)

    @pl.when(j == 0)
    def _():
        acc_ref[...] = part

    @pl.when(j > 0)
    def _():
        acc_ref[...] += part

    @pl.when(j == pl.num_programs(1) - 1)
    def _():
        y = _layer_norm(DEEPNORM_ALPHA * x_ref[...] + acc_ref[...], g_ref[...], b_ref[...])
        o_ref[...] = y
        ob_ref[...] = y.astype(BF16)


def _mlp_ln(xb, x, w_up, w_down, g, b, tm, tf):
    l, d = x.shape
    f = w_up.shape[1]
    row = pl.BlockSpec((1, d), lambda i, j: (0, 0))
    return pl.pallas_call(
        _mlp_kernel,
        grid=(l // tm, f // tf),
        in_specs=[pl.BlockSpec((tm, d), lambda i, j: (i, 0)),
                  pl.BlockSpec((tm, d), lambda i, j: (i, 0)),
                  pl.BlockSpec((d, tf), lambda i, j: (0, j)),
                  pl.BlockSpec((tf, d), lambda i, j: (j, 0)), row, row],
        out_specs=[pl.BlockSpec((tm, d), lambda i, j: (i, 0)),
                   pl.BlockSpec((tm, d), lambda i, j: (i, 0))],
        out_shape=[jax.ShapeDtypeStruct((l, d), F32), jax.ShapeDtypeStruct((l, d), BF16)],
        scratch_shapes=[pltpu.VMEM((tm, d), F32)],
        compiler_params=_cparams("parallel", "arbitrary"),
        name="mlp_ln",
    )(xb, x, w_up, w_down, g, b)


def _pick(n, pref):
    while n % pref:
        pref //= 2
    return pref


def kernel(x, w_in, lambda_q1, lambda_k1, lambda_q2, lambda_k2, attn_norm_g, ssm_lambda_re, ssm_lambda_im, ssm_log_dt, ssm_b_re, ssm_b_im, ssm_c_re, ssm_c_im, ssm_d, glu_w, glu_b, ssm_norm_g, w_out, ln1_g, ln1_b, w_up, w_down, ln2_g, ln2_b):
    bsz, seq, d = x.shape
    depth = w_in.shape[0]
    attn_w = attn_norm_g.shape[1]
    ssm_w = ssm_d.shape[1]
    n_groups = ssm_w // SSM_GROUP_CH
    assert bsz == 1 and seq % CHUNK == 0
    n_chunks = seq // CHUNK

    blk = _pick(seq, 512)
    xf = x.reshape(seq, d)
    xb = xf.astype(BF16)
    for l in range(depth):
        lam_init = 0.8 - 0.6 * math.exp(-0.3 * l)
        w_qk = w_in[l][:, :2 * attn_w].astype(BF16)
        w_v_t = w_in[l][:, 2 * attn_w:3 * attn_w].T.astype(BF16)
        w_u_t = w_in[l][:, 3 * attn_w:].T.astype(BF16)

        qk = _matmul_qk(xb, w_qk, BF16, _pick(seq, 1024), 1024, attn_w)
        v_t = _matmul_nt(w_v_t, xb, BF16, attn_w, _pick(seq, 1024))
        u_t = _matmul_nt(w_u_t, xb, F32, ssm_w, _pick(seq, 1024))

        attn_t = _diff_attention(
            qk, v_t, lambda_q1[l].reshape(1, -1), lambda_k1[l].reshape(1, -1),
            lambda_q2[l].reshape(1, -1), lambda_k2[l].reshape(1, -1),
            attn_norm_g[l].reshape(attn_w, 1), lam_init, blk)

        u_g = (u_t.reshape(n_groups, SSM_GROUP_CH, n_chunks, CHUNK)
               .transpose(0, 2, 1, 3).reshape(n_groups, n_chunks, SSM_GROUP_CH * CHUNK))
        y_g = _ssm_core(u_g, ssm_log_dt[l], ssm_lambda_re[l], ssm_lambda_im[l],
                        ssm_b_re[l], ssm_b_im[l], ssm_c_re[l], ssm_c_im[l], ssm_d[l])
        y_t = (y_g.reshape(n_groups, n_chunks, SSM_GROUP_CH, CHUNK)
               .transpose(0, 2, 1, 3).reshape(ssm_w, seq))
        ssm_t = _ssm_post(y_t, glu_w[l].T.astype(BF16), glu_b[l].reshape(ssm_w, 1),
                          ssm_norm_g[l].reshape(ssm_w, 1), _pick(seq, 512))

        w_o = w_out[l].astype(BF16)
        xf, xb = _out_proj_ln(attn_t, ssm_t, w_o[:attn_w], w_o[attn_w:], xf,
                              ln1_g[l].reshape(1, d), ln1_b[l].reshape(1, d), _pick(seq, 512))
        xf, xb = _mlp_ln(xb, xf, w_up[l].astype(BF16), w_down[l].astype(BF16),
                         ln2_g[l].reshape(1, d), ln2_b[l].reshape(1, d), _pick(seq, 512), 1024)
    return xf.reshape(bsz, seq, d)
```

```python
import functools
import math

import jax
import jax.numpy as jnp
from jax import lax
from jax.experimental import pallas as pl
from jax.experimental.pallas import tpu as pltpu

F32 = jnp.float32
BF16 = jnp.bfloat16

DEPTH = 4
HEAD_DIM = 64
HEAD_W = 2 * HEAD_DIM
CHUNK = 64
SSM_GROUP_CH = 16
SSM_STATE = 64
DEEPNORM_ALPHA = (2.0 * DEPTH) ** 0.25
NORM_EPS = 1e-5
MASK_VALUE = -1e30
QK_SCALE_LOG2E = HEAD_DIM ** -0.5 * math.log2(math.e)
LANES = 128
VMEM_LIMIT = 56 * 1024 * 1024
HIGHEST = lax.Precision.HIGHEST


def _cparams(*sem):
    return pltpu.CompilerParams(dimension_semantics=sem, vmem_limit_bytes=VMEM_LIMIT)


def _mm_qk_kernel(a_ref, b_ref, o_ref, *, q_blocks):
    acc = jnp.dot(a_ref[...], b_ref[...], preferred_element_type=F32)
    scale = jnp.where(pl.program_id(1) < q_blocks, QK_SCALE_LOG2E, 1.0)
    o_ref[...] = (acc * scale).astype(o_ref.dtype)


def _proj_qk(xb, w_in_b, layer, n_cols, q_cols, tm, tn):
    m, k = xb.shape
    return pl.pallas_call(
        functools.partial(_mm_qk_kernel, q_blocks=q_cols // tn),
        grid=(m // tm, n_cols // tn),
        in_specs=[pl.BlockSpec((tm, k), lambda i, j: (i, 0)),
                  pl.BlockSpec((None, k, tn), lambda i, j: (layer, 0, j))],
        out_specs=pl.BlockSpec((tm, tn), lambda i, j: (i, j)),
        out_shape=jax.ShapeDtypeStruct((m, n_cols), BF16),
        compiler_params=_cparams("parallel", "arbitrary"),
        name="proj_qk",
    )(xb, w_in_b)


def _mm_vu_kernel(w_ref, x_ref, v_ref, u_ref):
    nt_dims = (((1,), (1,)), ((), ()))
    rows_v = v_ref.shape[0]
    x = x_ref[...]
    v_ref[...] = lax.dot_general(w_ref[:rows_v, :], x, nt_dims,
                                 preferred_element_type=F32).astype(v_ref.dtype)
    u_ref[...] = lax.dot_general(w_ref[rows_v:, :], x, nt_dims,
                                 preferred_element_type=F32).astype(u_ref.dtype)


def _proj_vu(w_vu_t, xb, layer, rows_v, tl):
    _, r, k = w_vu_t.shape
    l = xb.shape[0]
    rows_u = r - rows_v
    return pl.pallas_call(
        _mm_vu_kernel,
        grid=(l // tl,),
        in_specs=[pl.BlockSpec((None, r, k), lambda i: (layer, 0, 0)),
                  pl.BlockSpec((tl, k), lambda i: (i, 0))],
        out_specs=[pl.BlockSpec((rows_v, tl), lambda i: (0, i)),
                   pl.BlockSpec((rows_u, tl), lambda i: (0, i))],
        out_shape=[jax.ShapeDtypeStruct((rows_v, l), BF16),
                   jax.ShapeDtypeStruct((rows_u, l), F32)],
        compiler_params=_cparams("parallel"),
        name="proj_vu",
    )(w_vu_t, xb)


def _attn_kernel(lq1_ref, lk1_ref, lq2_ref, lk2_ref, qa_ref, qb_ref, k_ref, vt_ref, g_ref,
                 oa_ref, ob_ref, qz_ref, s_ref, mb_ref, m_ref, l_ref, acc_ref,
                 *, blk, n_qblk, lam_init):
    i = pl.program_id(1)
    blk_b = n_qblk - 1 - i
    nt_dims = (((1,), (1,)), ((), ()))

    lane = lax.broadcasted_iota(jnp.int32, (blk, HEAD_W), 1)
    for sel, q_ref in enumerate((qa_ref, qb_ref)):
        q = q_ref[...]
        zero = jnp.zeros_like(q)
        qz_ref[sel, 0] = jnp.where(lane < HEAD_DIM, q, zero)
        qz_ref[sel, 1] = jnp.where(lane >= HEAD_DIM, q, zero)
    m_ref[...] = jnp.full(m_ref.shape, MASK_VALUE, F32)
    l_ref[...] = jnp.zeros(l_ref.shape, F32)
    acc_ref[...] = jnp.zeros(acc_ref.shape, F32)

    def full_item(k):
        idx = k - 2
        sel = (idx >= i).astype(jnp.int32)
        return idx - sel * i, sel

    def stage_a(block, sel, slot, diagonal):
        st = pl.multiple_of(block * blk, blk)
        kb = k_ref[pl.ds(st, blk), :]
        for mp in range(2):
            s = lax.dot_general(kb, qz_ref[sel, mp], nt_dims, preferred_element_type=F32)
            if diagonal:
                key_chunk = lax.broadcasted_iota(jnp.int32, (blk, blk), 0) // CHUNK
                qry_chunk = lax.broadcasted_iota(jnp.int32, (blk, blk), 1) // CHUNK
                s = jnp.where(key_chunk <= qry_chunk, s, MASK_VALUE)
            s_ref[slot, mp] = s
            mb_ref[slot, mp] = jnp.max(s, axis=0, keepdims=True)

    def stage_b(block, sel, slot):
        st = pl.multiple_of(block * blk, blk)
        vtb = vt_ref[:, pl.ds(st, blk)]
        for mp in range(2):
            m_old = m_ref[sel, mp]
            m_new = jnp.maximum(m_old, mb_ref[slot, mp])
            alpha = jnp.exp2(m_old - m_new)
            p = jnp.exp2(s_ref[slot, mp] - m_new)
            l_ref[sel, mp] = alpha * l_ref[sel, mp] + jnp.sum(p, axis=0, keepdims=True)
            acc_ref[sel, mp] = alpha * acc_ref[sel, mp] + jnp.dot(
                vtb, p.astype(BF16), preferred_element_type=F32)
            m_ref[sel, mp] = m_new

    stage_a(i, 0, 0, True)
    stage_a(blk_b, 1, 1, True)
    stage_b(i, 0, 0)
    blk2, sel2 = full_item(2)
    stage_a(blk2, sel2, 0, False)
    stage_b(blk_b, 1, 1)

    def trip(t, carry):
        b0, s0 = full_item(2 * t)
        b1, s1 = full_item(2 * t + 1)
        b2, s2 = full_item(2 * t + 2)
        stage_a(b1, s1, 1, False)
        stage_b(b0, s0, 0)
        stage_a(b2, s2, 0, False)
        stage_b(b1, s1, 1)
        return carry

    lax.fori_loop(1, n_qblk // 2, trip, 0)
    b_last, s_last = full_item(n_qblk)
    stage_b(b_last, s_last, 0)

    lam = (jnp.exp(jnp.sum(lq1_ref[...] * lk1_ref[...], axis=1, keepdims=True))
           - jnp.exp(jnp.sum(lq2_ref[...] * lk2_ref[...], axis=1, keepdims=True))
           + lam_init)
    for sel, o_ref in enumerate((oa_ref, ob_ref)):
        o = acc_ref[sel, 0] / l_ref[sel, 0] - lam * (acc_ref[sel, 1] / l_ref[sel, 1])
        ms = jnp.mean(o * o, axis=0, keepdims=True)
        o = o * lax.rsqrt(ms + NORM_EPS) * g_ref[...] * (1.0 - lam_init)
        o_ref[...] = o.astype(o_ref.dtype)


def _diff_attention(qk, v_t, lq1, lk1, lq2, lk2, g_col, layer, lam_init, blk):
    l = qk.shape[0]
    a = v_t.shape[0]
    n_heads = a // HEAD_W
    n_qblk = l // blk
    assert n_qblk % 2 == 0
    half = n_qblk // 2
    lam_spec = pl.BlockSpec((None, 1, HEAD_DIM), lambda h, i: (layer, 0, 0))
    out_sds = jax.ShapeDtypeStruct((a, l // 2), BF16)
    return pl.pallas_call(
        functools.partial(_attn_kernel, blk=blk, n_qblk=n_qblk, lam_init=lam_init),
        grid=(n_heads, half),
        in_specs=[lam_spec, lam_spec, lam_spec, lam_spec,
                  pl.BlockSpec((blk, HEAD_W), lambda h, i: (i, h)),
                  pl.BlockSpec((blk, HEAD_W), lambda h, i: (n_qblk - 1 - i, h)),
                  pl.BlockSpec((l, HEAD_W), lambda h, i: (0, n_heads + h)),
                  pl.BlockSpec((HEAD_W, l), lambda h, i: (h, 0)),
                  pl.BlockSpec((None, HEAD_W, 1), lambda h, i: (layer, h, 0))],
        out_specs=[pl.BlockSpec((HEAD_W, blk), lambda h, i: (h, i)),
                   pl.BlockSpec((HEAD_W, blk), lambda h, i: (h, half - 1 - i))],
        out_shape=[out_sds, out_sds],
        scratch_shapes=[pltpu.VMEM((2, 2, blk, HEAD_W), BF16),
                        pltpu.VMEM((2, 2, blk, blk), F32),
                        pltpu.VMEM((2, 2, 1, blk), F32),
                        pltpu.VMEM((2, 2, 1, blk), F32),
                        pltpu.VMEM((2, 2, 1, blk), F32),
                        pltpu.VMEM((2, 2, HEAD_W, blk), F32)],
        compiler_params=_cparams("parallel", "arbitrary"),
        name="diff_attn",
    )(lq1, lk1, lq2, lk2, qk, qk, qk, v_t, g_col)


def _complex_pow(ar, ai, e, n_bits):
    shape = jnp.broadcast_shapes(ar.shape, e.shape)
    pr = jnp.ones(shape, F32)
    pi = jnp.zeros(shape, F32)
    fr, fi = ar, ai
    for b in range(n_bits):
        bit = ((e >> b) & 1) == 1
        nr = pr * fr - pi * fi
        ni = pr * fi + pi * fr
        pr = jnp.where(bit, nr, pr)
        pi = jnp.where(bit, ni, pi)
        if b + 1 < n_bits:
            fr, fi = fr * fr - fi * fi, 2.0 * fr * fi
    return pr, pi


def _row_to_col(row):
    n = row.shape[1]
    eye = (lax.broadcasted_iota(jnp.int32, (n, n), 0)
           == lax.broadcasted_iota(jnp.int32, (n, n), 1))
    return jnp.sum(jnp.where(eye, jnp.broadcast_to(row, (n, n)), 0.0), axis=1, keepdims=True)


def _ssm_kernel(u_ref, ldt_ref, lr_ref, li_ref, bt_re_ref, bt_im_ref, ct_re_ref, ct_im_ref,
                d_ref, y_ref, m_ref, w_re_ref, w_im_ref, s_re_ref, s_im_ref, h_re_ref, h_im_ref,
                *, n_chunks):
    t = CHUNK
    n_ch = SSM_GROUP_CH
    w = n_ch * t
    n_bits = t.bit_length() - 1
    dt = jnp.exp(ldt_ref[...])

    lr = lr_ref[...]
    li = li_ref[...]
    mag = jnp.exp(lr * dt)
    ar = mag * jnp.cos(li * dt)
    ai = mag * jnp.sin(li * dt)
    den = lr * lr + li * li
    nr = ar - 1.0
    fr = (nr * lr + ai * li) / den
    fi = (ai * lr - nr * li) / den
    bbt_r = fr * bt_re_ref[...] - fi * bt_im_ref[...]
    bbt_i = fr * bt_im_ref[...] + fi * bt_re_ref[...]

    rev = (t - 1) - lax.broadcasted_iota(jnp.int32, (t, 1), 0)
    pr, pi = _complex_pow(ar, ai, rev, n_bits)
    for c in range(n_ch):
        br = bbt_r[c:c + 1, :]
        bi = bbt_i[c:c + 1, :]
        w_re_ref[c * t:(c + 1) * t, :] = (pr * br - pi * bi).astype(BF16)
        w_im_ref[c * t:(c + 1) * t, :] = (pr * bi + pi * br).astype(BF16)
    at_r, at_i = _complex_pow(ar, ai, jnp.full((1, 1), t, jnp.int32), n_bits + 1)

    ar_c = _row_to_col(ar)
    ai_c = _row_to_col(ai)
    tau_tile = lax.broadcasted_iota(jnp.int32, (1, LANES), 1) % t
    qr, qi = _complex_pow(ar_c, ai_c, tau_tile, n_bits)
    qr = jnp.concatenate([qr] * (w // LANES), axis=1)
    qi = jnp.concatenate([qi] * (w // LANES), axis=1)
    expand = (lax.broadcasted_iota(jnp.int32, (n_ch, w), 1) // t
              == lax.broadcasted_iota(jnp.int32, (n_ch, w), 0)).astype(F32)
    c_re = jnp.dot(ct_re_ref[...], expand, preferred_element_type=F32, precision=HIGHEST)
    c_im = jnp.dot(ct_im_ref[...], expand, preferred_element_type=F32, precision=HIGHEST)
    d_rep = jnp.dot(jnp.broadcast_to(d_ref[...], (8, n_ch)), expand,
                    preferred_element_type=F32, precision=HIGHEST)[0:1, :]
    ca_r = c_re * qr - c_im * qi
    ca_i = c_re * qi + c_im * qr
    z = (jnp.dot(bbt_r, ca_r, preferred_element_type=F32, precision=HIGHEST)
         - jnp.dot(bbt_i, ca_i, preferred_element_type=F32, precision=HIGHEST))
    v_r = (ca_r * ar_c - ca_i * ai_c).astype(BF16)
    v_i = (ca_r * ai_c + ca_i * ar_c).astype(BF16)

    tau = lax.broadcasted_iota(jnp.int32, (1, w), 1) % t
    s_row = lax.broadcasted_iota(jnp.int32, (t, 1), 0)
    causal = tau >= s_row
    for c in range(n_ch):
        strip = jnp.broadcast_to(z[c:c + 1, :], (t, w))
        shifted = pltpu.roll(strip, 0, 1, stride=1, stride_axis=0)
        m_ref[c * t:(c + 1) * t, :] = jnp.where(causal, shifted, 0.0).astype(BF16)

    u = u_ref[...]
    ub = u.astype(BF16)
    y = jnp.dot(ub, m_ref[...], preferred_element_type=F32)
    s_re_ref[...] = jnp.dot(ub, w_re_ref[...], preferred_element_type=F32)
    s_im_ref[...] = jnp.dot(ub, w_im_ref[...], preferred_element_type=F32)

    def step(r, carry):
        hr, hi = carry
        h_re_ref[pl.ds(r, 1), :] = hr
        h_im_ref[pl.ds(r, 1), :] = hi
        sr = s_re_ref[pl.ds(r, 1), :]
        si = s_im_ref[pl.ds(r, 1), :]
        return (at_r * hr - at_i * hi + sr, at_r * hi + at_i * hr + si)

    zero = jnp.zeros((1, SSM_STATE), F32)
    lax.fori_loop(0, n_chunks, step, (zero, zero))

    y = y + jnp.dot(h_re_ref[...].astype(BF16), v_r, preferred_element_type=F32)
    y = y - jnp.dot(h_im_ref[...].astype(BF16), v_i, preferred_element_type=F32)
    y_ref[...] = y + d_rep * u


def _ssm_core(u_g, ldt, lam_re, lam_im, bt_re, bt_im, ct_re, ct_im, d_skip, layer):
    g, n_chunks, w = u_g.shape
    p = SSM_STATE
    c = SSM_GROUP_CH

    def pspec(*shape):
        return pl.BlockSpec((None, None) + shape, lambda i: (layer, i, 0, 0))

    data = pl.BlockSpec((None, n_chunks, w), lambda i: (i, 0, 0))
    return pl.pallas_call(
        functools.partial(_ssm_kernel, n_chunks=n_chunks),
        grid=(g,),
        in_specs=[data, pspec(1, 1), pspec(1, p), pspec(1, p), pspec(c, p), pspec(c, p),
                  pspec(p, c), pspec(p, c), pspec(1, c)],
        out_specs=data,
        out_shape=jax.ShapeDtypeStruct((g, n_chunks, w), F32),
        scratch_shapes=[pltpu.VMEM((w, w), BF16),
                        pltpu.VMEM((w, p), BF16), pltpu.VMEM((w, p), BF16),
                        pltpu.VMEM((n_chunks, p), F32), pltpu.VMEM((n_chunks, p), F32),
                        pltpu.VMEM((n_chunks, p), F32), pltpu.VMEM((n_chunks, p), F32)],
        compiler_params=_cparams("parallel"),
        name="ssm_core",
    )(u_g, ldt, lam_re, lam_im, bt_re, bt_im, ct_re, ct_im, d_skip)


def _ssm_post_kernel(y_ref, w_ref, b_ref, g_ref, o_ref):
    y = y_ref[...]
    k0 = math.sqrt(2.0 / math.pi)
    y = 0.5 * y * (1.0 + jnp.tanh(k0 * (y + 0.044715 * (y * y * y))))
    z = jnp.dot(w_ref[...], y.astype(BF16), preferred_element_type=F32) + b_ref[...]
    o = y * (1.0 / (1.0 + jnp.exp(-z)))
    w, tl = o.shape
    o3 = o.reshape(w // SSM_GROUP_CH, SSM_GROUP_CH, tl)
    ms = jnp.mean(o3 * o3, axis=1, keepdims=True)
    o3 = o3 * lax.rsqrt(ms + NORM_EPS)
    o_ref[...] = (o3.reshape(w, tl) * g_ref[...]).astype(o_ref.dtype)


def _ssm_post(y_t, glu_w_t, glu_b_col, g_col, layer, tl):
    w, l = y_t.shape
    col = pl.BlockSpec((None, w, 1), lambda i: (layer, 0, 0))
    return pl.pallas_call(
        _ssm_post_kernel,
        grid=(l // tl,),
        in_specs=[pl.BlockSpec((w, tl), lambda i: (0, i)),
                  pl.BlockSpec((None, w, w), lambda i: (layer, 0, 0)), col, col],
        out_specs=pl.BlockSpec((w, tl), lambda i: (0, i)),
        out_shape=jax.ShapeDtypeStruct((w, l), BF16),
        compiler_params=_cparams("parallel"),
        name="ssm_post",
    )(y_t, glu_w_t, glu_b_col, g_col)


def _layer_norm(y, g, b):
    mu = jnp.mean(y, axis=-1, keepdims=True)
    yc = y - mu
    var = jnp.mean(yc * yc, axis=-1, keepdims=True)
    return yc * lax.rsqrt(var + NORM_EPS) * g + b


def _out_proj_kernel(a_ref, s_ref, wa_ref, ws_ref, x_ref, g_ref, b_ref, o_ref, ob_ref):
    tn_dims = (((0,), (0,)), ((), ()))
    mix = lax.dot_general(a_ref[...], wa_ref[...], tn_dims, preferred_element_type=F32)
    mix = mix + lax.dot_general(s_ref[...], ws_ref[...], tn_dims, preferred_element_type=F32)
    y = _layer_norm(DEEPNORM_ALPHA * x_ref[...] + mix, g_ref[...], b_ref[...])
    o_ref[...] = y
    ob_ref[...] = y.astype(BF16)


def _out_proj_ln(attn_t, ssm_t, w_out_b, x, g, b, layer, tm):
    l, d = x.shape
    a = attn_t.shape[0]
    s = ssm_t.shape[0]
    assert a == s
    row = pl.BlockSpec((None, 1, d), lambda i: (layer, 0, 0))
    return pl.pallas_call(
        _out_proj_kernel,
        grid=(l // tm,),
        in_specs=[pl.BlockSpec((a, tm), lambda i: (0, i)),
                  pl.BlockSpec((s, tm), lambda i: (0, i)),
                  pl.BlockSpec((None, a, d), lambda i: (layer, 0, 0)),
                  pl.BlockSpec((None, s, d), lambda i: (layer, 1, 0)),
                  pl.BlockSpec((tm, d), lambda i: (i, 0)), row, row],
        out_specs=[pl.BlockSpec((tm, d), lambda i: (i, 0)),
                   pl.BlockSpec((tm, d), lambda i: (i, 0))],
        out_shape=[jax.ShapeDtypeStruct((l, d), F32), jax.ShapeDtypeStruct((l, d), BF16)],
        compiler_params=_cparams("parallel"),
        name="out_proj_ln",
    )(attn_t, ssm_t, w_out_b, w_out_b, x, g, b)


def _mlp_kernel(xb_ref, x_ref, wu_ref, wd_ref, g_ref, b_ref, o_ref, ob_ref, acc_ref):
    j = pl.program_id(1)
    h = jnp.dot(xb_ref[...], wu_ref[...], preferred_element_type=F32)
    h = jnp.maximum(h, 0.0)
    h = (h * h).astype(BF16)
    part = jnp.dot(h, wd_ref[...], preferred_element_type=F32)

    @pl.when(j == 0)
    def _():
        acc_ref[...] = part

    @pl.when(j > 0)
    def _():
        acc_ref[...] += part

    @pl.when(j == pl.num_programs(1) - 1)
    def _():
        y = _layer_norm(DEEPNORM_ALPHA * x_ref[...] + acc_ref[...], g_ref[...], b_ref[...])
        o_ref[...] = y
        ob_ref[...] = y.astype(BF16)


def _mlp_ln(xb, x, w_up_b, w_down_b, g, b, layer, tm, tf):
    l, d = x.shape
    f = w_up_b.shape[2]
    row = pl.BlockSpec((None, 1, d), lambda i, j: (layer, 0, 0))
    return pl.pallas_call(
        _mlp_kernel,
        grid=(l // tm, f // tf),
        in_specs=[pl.BlockSpec((tm, d), lambda i, j: (i, 0)),
                  pl.BlockSpec((tm, d), lambda i, j: (i, 0)),
                  pl.BlockSpec((None, d, tf), lambda i, j: (layer, 0, j)),
                  pl.BlockSpec((None, tf, d), lambda i, j: (layer, j, 0)), row, row],
        out_specs=[pl.BlockSpec((tm, d), lambda i, j: (i, 0)),
                   pl.BlockSpec((tm, d), lambda i, j: (i, 0))],
        out_shape=[jax.ShapeDtypeStruct((l, d), F32), jax.ShapeDtypeStruct((l, d), BF16)],
        scratch_shapes=[pltpu.VMEM((tm, d), F32)],
        compiler_params=_cparams("parallel", "arbitrary"),
        name="mlp_ln",
    )(xb, x, w_up_b, w_down_b, g, b)


def _pick(n, pref):
    while n % pref:
        pref //= 2
    return pref


def kernel(x, w_in, lambda_q1, lambda_k1, lambda_q2, lambda_k2, attn_norm_g, ssm_lambda_re, ssm_lambda_im, ssm_log_dt, ssm_b_re, ssm_b_im, ssm_c_re, ssm_c_im, ssm_d, glu_w, glu_b, ssm_norm_g, w_out, ln1_g, ln1_b, w_up, w_down, ln2_g, ln2_b):
    bsz, seq, d = x.shape
    depth = w_in.shape[0]
    attn_w = attn_norm_g.shape[1]
    ssm_w = ssm_d.shape[1]
    n_groups = ssm_w // SSM_GROUP_CH
    assert bsz == 1 and seq % CHUNK == 0
    n_chunks = seq // CHUNK
    p = SSM_STATE
    c = SSM_GROUP_CH

    w_in_b = w_in.astype(BF16)
    w_vu_t = w_in_b[:, :, 2 * attn_w:].transpose(0, 2, 1)
    w_out_b = w_out.astype(BF16)
    w_up_b = w_up.astype(BF16)
    w_down_b = w_down.astype(BF16)
    glu_w_t = glu_w.transpose(0, 2, 1).astype(BF16)
    lq1 = lambda_q1.reshape(depth, 1, HEAD_DIM)
    lk1 = lambda_k1.reshape(depth, 1, HEAD_DIM)
    lq2 = lambda_q2.reshape(depth, 1, HEAD_DIM)
    lk2 = lambda_k2.reshape(depth, 1, HEAD_DIM)
    attn_g_col = attn_norm_g.reshape(depth, attn_w, 1)
    ldt = ssm_log_dt.reshape(depth, n_groups, 1, 1)
    lam_re = ssm_lambda_re.reshape(depth, n_groups, 1, p)
    lam_im = ssm_lambda_im.reshape(depth, n_groups, 1, p)
    bt_re = ssm_b_re.transpose(0, 1, 3, 2)
    bt_im = ssm_b_im.transpose(0, 1, 3, 2)
    ct_re = ssm_c_re.transpose(0, 1, 3, 2)
    ct_im = ssm_c_im.transpose(0, 1, 3, 2)
    d_skip = ssm_d.reshape(depth, n_groups, 1, c)
    glu_b_col = glu_b.reshape(depth, ssm_w, 1)
    ssm_g_col = ssm_norm_g.reshape(depth, ssm_w, 1)
    ln1g, ln1b = ln1_g.reshape(depth, 1, d), ln1_b.reshape(depth, 1, d)
    ln2g, ln2b = ln2_g.reshape(depth, 1, d), ln2_b.reshape(depth, 1, d)

    blk = _pick(seq, 512)
    xf = x.reshape(seq, d)
    xb = xf.astype(BF16)
    for l in range(depth):
        lam_init = 0.8 - 0.6 * math.exp(-0.3 * l)
        qk = _proj_qk(xb, w_in_b, l, 2 * attn_w, attn_w, _pick(seq, 1024), 1024)
        v_t, u_t = _proj_vu(w_vu_t, xb, l, attn_w, _pick(seq, 512))

        attn_lo, attn_hi = _diff_attention(qk, v_t, lq1, lk1, lq2, lk2, attn_g_col,
                                           l, lam_init, blk)
        attn_t = jnp.concatenate([attn_lo, attn_hi], axis=1)

        u_g = (u_t.reshape(n_groups, c, n_chunks, CHUNK)
               .transpose(0, 2, 1, 3).reshape(n_groups, n_chunks, c * CHUNK))
        y_g = _ssm_core(u_g, ldt, lam_re, lam_im, bt_re, bt_im, ct_re, ct_im, d_skip, l)
        y_t = (y_g.reshape(n_groups, n_chunks, c, CHUNK)
               .transpose(0, 2, 1, 3).reshape(ssm_w, seq))
        ssm_t = _ssm_post(y_t, glu_w_t, glu_b_col, ssm_g_col, l, _pick(seq, 512))

        xf, xb = _out_proj_ln(attn_t, ssm_t, w_out_b, xf, ln1g, ln1b, l, _pick(seq, 512))
        xf, xb = _mlp_ln(xb, xf, w_up_b, w_down_b, ln2g, ln2b, l, _pick(seq, 512), 1024)
    return xf.reshape(bsz, seq, d)
```

```python
import functools
import math

import jax
import jax.numpy as jnp
from jax import lax
from jax.experimental import pallas as pl
from jax.experimental.pallas import tpu as pltpu

F32 = jnp.float32
BF16 = jnp.bfloat16

DEPTH = 4
HEAD_DIM = 64
HEAD_W = 2 * HEAD_DIM
CHUNK = 64
SSM_GROUP_CH = 16
SSM_STATE = 64
DEEPNORM_ALPHA = (2.0 * DEPTH) ** 0.25
NORM_EPS = 1e-5
MASK_VALUE = -1e30
QK_SCALE_LOG2E = HEAD_DIM ** -0.5 * math.log2(math.e)
ONES_ROWS = 16
UNROLL = 4
LANES = 128
VMEM_LIMIT = 56 * 1024 * 1024
HIGHEST = lax.Precision.HIGHEST


def _cparams(*sem):
    return pltpu.CompilerParams(dimension_semantics=sem, vmem_limit_bytes=VMEM_LIMIT)


def _mm_qk_kernel(a_ref, b_ref, o_ref, *, q_blocks):
    acc = jnp.dot(a_ref[...], b_ref[...], preferred_element_type=F32)
    scale = jnp.where(pl.program_id(1) < q_blocks, QK_SCALE_LOG2E, 1.0)
    o_ref[...] = (acc * scale).astype(o_ref.dtype)


def _proj_qk(xb, w_in_b, layer, n_cols, q_cols, tm, tn):
    m, k = xb.shape
    return pl.pallas_call(
        functools.partial(_mm_qk_kernel, q_blocks=q_cols // tn),
        grid=(m // tm, n_cols // tn),
        in_specs=[pl.BlockSpec((tm, k), lambda i, j: (i, 0)),
                  pl.BlockSpec((None, k, tn), lambda i, j: (layer, 0, j))],
        out_specs=pl.BlockSpec((tm, tn), lambda i, j: (i, j)),
        out_shape=jax.ShapeDtypeStruct((m, n_cols), BF16),
        compiler_params=_cparams("parallel", "arbitrary"),
        name="proj_qk",
    )(xb, w_in_b)


def _mm_vu_kernel(w_ref, x_ref, v_ref, u_ref):
    nt_dims = (((1,), (1,)), ((), ()))
    rows_v = v_ref.shape[0]
    x = x_ref[...]
    v_ref[...] = lax.dot_general(w_ref[:rows_v, :], x, nt_dims,
                                 preferred_element_type=F32).astype(v_ref.dtype)
    u_ref[...] = lax.dot_general(w_ref[rows_v:, :], x, nt_dims,
                                 preferred_element_type=F32).astype(u_ref.dtype)


def _proj_vu(w_vu_t, xb, layer, rows_v, tl):
    _, r, k = w_vu_t.shape
    l = xb.shape[0]
    rows_u = r - rows_v
    return pl.pallas_call(
        _mm_vu_kernel,
        grid=(l // tl,),
        in_specs=[pl.BlockSpec((None, r, k), lambda i: (layer, 0, 0)),
                  pl.BlockSpec((tl, k), lambda i: (i, 0))],
        out_specs=[pl.BlockSpec((rows_v, tl), lambda i: (0, i)),
                   pl.BlockSpec((rows_u, tl), lambda i: (0, i))],
        out_shape=[jax.ShapeDtypeStruct((rows_v, l), BF16),
                   jax.ShapeDtypeStruct((rows_u, l), BF16)],
        compiler_params=_cparams("parallel"),
        name="proj_vu",
    )(w_vu_t, xb)


def _attn_kernel(lq1_ref, lk1_ref, lq2_ref, lk2_ref, qa_ref, qb_ref, k_ref, vt_ref, g_ref,
                 oa_ref, ob_ref, qz_ref, s_ref, mb_ref, m_ref, acc_ref,
                 *, blk, n_qblk, lam_init):
    i = pl.program_id(1)
    blk_b = n_qblk - 1 - i
    nt_dims = (((1,), (1,)), ((), ()))

    lane = lax.broadcasted_iota(jnp.int32, (blk, HEAD_W), 1)
    for sel, q_ref in enumerate((qa_ref, qb_ref)):
        q = q_ref[...]
        zero = jnp.zeros_like(q)
        qz_ref[sel, 0] = jnp.where(lane < HEAD_DIM, q, zero)
        qz_ref[sel, 1] = jnp.where(lane >= HEAD_DIM, q, zero)
    m_ref[...] = jnp.full(m_ref.shape, MASK_VALUE, F32)
    acc_ref[...] = jnp.zeros(acc_ref.shape, F32)
    ones = jnp.ones((ONES_ROWS, blk), BF16)

    def full_item(k):
        idx = k - 2
        sel = (idx >= i).astype(jnp.int32)
        return idx - sel * i, sel

    def stage_a(block, sel, slot, diagonal):
        st = pl.multiple_of(block * blk, blk)
        kb = k_ref[pl.ds(st, blk), :]
        for mp in range(2):
            s = lax.dot_general(kb, qz_ref[sel, mp], nt_dims, preferred_element_type=F32)
            if diagonal:
                key_chunk = lax.broadcasted_iota(jnp.int32, (blk, blk), 0) // CHUNK
                qry_chunk = lax.broadcasted_iota(jnp.int32, (blk, blk), 1) // CHUNK
                s = jnp.where(key_chunk <= qry_chunk, s, MASK_VALUE)
            s_ref[slot, mp] = s
            mb_ref[slot, mp] = jnp.max(s, axis=0, keepdims=True)

    def stage_b(block, sel, slot):
        st = pl.multiple_of(block * blk, blk)
        v_ext = jnp.concatenate([vt_ref[:, pl.ds(st, blk)], ones], axis=0)
        for mp in range(2):
            m_old = m_ref[sel, mp]
            m_new = jnp.maximum(m_old, mb_ref[slot, mp])
            alpha = jnp.exp2(m_old - m_new)
            p = jnp.exp2(s_ref[slot, mp] - m_new).astype(BF16)
            acc_ref[sel, mp] = alpha * acc_ref[sel, mp] + jnp.dot(
                v_ext, p, preferred_element_type=F32)
            m_ref[sel, mp] = m_new

    def item(k):
        if isinstance(k, int) and k < 2:
            return (i, 0, True) if k == 0 else (blk_b, 1, True)
        return full_item(k) + (False,)

    def four_items(base):
        for q in range(UNROLL):
            nb, ns, nd = item(base + q + 1)
            stage_a(nb, ns, (q + 1) % 2, nd)
            cb, cs, _ = item(base + q)
            stage_b(cb, cs, q % 2)

    stage_a(i, 0, 0, True)
    four_items(0)

    def trip(t, carry):
        four_items(UNROLL * t)
        return carry

    lax.fori_loop(1, n_qblk // UNROLL, trip, 0)
    b_last, s_last = full_item(n_qblk)
    stage_b(b_last, s_last, 0)

    lam = (jnp.exp(jnp.sum(lq1_ref[...] * lk1_ref[...], axis=1, keepdims=True))
           - jnp.exp(jnp.sum(lq2_ref[...] * lk2_ref[...], axis=1, keepdims=True))
           + lam_init)
    for sel, o_ref in enumerate((oa_ref, ob_ref)):
        o = (acc_ref[sel, 0, :HEAD_W, :] / acc_ref[sel, 0, HEAD_W:HEAD_W + 1, :]
             - lam * (acc_ref[sel, 1, :HEAD_W, :] / acc_ref[sel, 1, HEAD_W:HEAD_W + 1, :]))
        ms = jnp.mean(o * o, axis=0, keepdims=True)
        o = o * lax.rsqrt(ms + NORM_EPS) * g_ref[...] * (1.0 - lam_init)
        o_ref[...] = o.astype(o_ref.dtype)


def _diff_attention(qk, v_t, lq1, lk1, lq2, lk2, g_col, layer, lam_init, blk):
    l = qk.shape[0]
    a = v_t.shape[0]
    n_heads = a // HEAD_W
    n_qblk = l // blk
    assert n_qblk % UNROLL == 0
    half = n_qblk // 2
    lam_spec = pl.BlockSpec((None, 1, HEAD_DIM), lambda h, i: (layer, 0, 0))
    out_sds = jax.ShapeDtypeStruct((a, l // 2), BF16)
    return pl.pallas_call(
        functools.partial(_attn_kernel, blk=blk, n_qblk=n_qblk, lam_init=lam_init),
        grid=(n_heads, half),
        in_specs=[lam_spec, lam_spec, lam_spec, lam_spec,
                  pl.BlockSpec((blk, HEAD_W), lambda h, i: (i, h)),
                  pl.BlockSpec((blk, HEAD_W), lambda h, i: (n_qblk - 1 - i, h)),
                  pl.BlockSpec((l, HEAD_W), lambda h, i: (0, n_heads + h)),
                  pl.BlockSpec((HEAD_W, l), lambda h, i: (h, 0)),
                  pl.BlockSpec((None, HEAD_W, 1), lambda h, i: (layer, h, 0))],
        out_specs=[pl.BlockSpec((HEAD_W, blk), lambda h, i: (h, i)),
                   pl.BlockSpec((HEAD_W, blk), lambda h, i: (h, half - 1 - i))],
        out_shape=[out_sds, out_sds],
        scratch_shapes=[pltpu.VMEM((2, 2, blk, HEAD_W), BF16),
                        pltpu.VMEM((2, 2, blk, blk), F32),
                        pltpu.VMEM((2, 2, 1, blk), F32),
                        pltpu.VMEM((2, 2, 1, blk), F32),
                        pltpu.VMEM((2, 2, HEAD_W + ONES_ROWS, blk), F32)],
        compiler_params=_cparams("parallel", "arbitrary"),
        name="diff_attn",
    )(lq1, lk1, lq2, lk2, qk, qk, qk, v_t, g_col)


def _complex_pow(ar, ai, e, n_bits):
    shape = jnp.broadcast_shapes(ar.shape, e.shape)
    pr = jnp.ones(shape, F32)
    pi = jnp.zeros(shape, F32)
    fr, fi = ar, ai
    for b in range(n_bits):
        bit = ((e >> b) & 1) == 1
        nr = pr * fr - pi * fi
        ni = pr * fi + pi * fr
        pr = jnp.where(bit, nr, pr)
        pi = jnp.where(bit, ni, pi)
        if b + 1 < n_bits:
            fr, fi = fr * fr - fi * fi, 2.0 * fr * fi
    return pr, pi


def _gelu_tanh(y):
    k0 = math.sqrt(2.0 / math.pi)
    return 0.5 * y * (1.0 + jnp.tanh(k0 * (y + 0.044715 * (y * y * y))))


def _row_to_col(row):
    n = row.shape[1]
    eye = (lax.broadcasted_iota(jnp.int32, (n, n), 0)
           == lax.broadcasted_iota(jnp.int32, (n, n), 1))
    return jnp.sum(jnp.where(eye, jnp.broadcast_to(row, (n, n)), 0.0), axis=1, keepdims=True)


def _ssm_kernel(u_ref, ldt_ref, lr_ref, li_ref, bt_re_ref, bt_im_ref, ct_re_ref, ct_im_ref,
                d_ref, y_ref, m_ref, w_re_ref, w_im_ref, s_re_ref, s_im_ref, h_re_ref, h_im_ref,
                *, n_chunks):
    t = CHUNK
    n_ch = SSM_GROUP_CH
    w = n_ch * t
    n_bits = t.bit_length() - 1
    dt = jnp.exp(ldt_ref[...])

    lr = lr_ref[...]
    li = li_ref[...]
    mag = jnp.exp(lr * dt)
    ar = mag * jnp.cos(li * dt)
    ai = mag * jnp.sin(li * dt)
    den = lr * lr + li * li
    nr = ar - 1.0
    fr = (nr * lr + ai * li) / den
    fi = (ai * lr - nr * li) / den
    bbt_r = fr * bt_re_ref[...] - fi * bt_im_ref[...]
    bbt_i = fr * bt_im_ref[...] + fi * bt_re_ref[...]

    rev = (t - 1) - lax.broadcasted_iota(jnp.int32, (t, 1), 0)
    pr, pi = _complex_pow(ar, ai, rev, n_bits)
    for c in range(n_ch):
        br = bbt_r[c:c + 1, :]
        bi = bbt_i[c:c + 1, :]
        w_re_ref[c * t:(c + 1) * t, :] = (pr * br - pi * bi).astype(BF16)
        w_im_ref[c * t:(c + 1) * t, :] = (pr * bi + pi * br).astype(BF16)
    at_r, at_i = _complex_pow(ar, ai, jnp.full((1, 1), t, jnp.int32), n_bits + 1)

    ar_c = _row_to_col(ar)
    ai_c = _row_to_col(ai)
    tau_tile = lax.broadcasted_iota(jnp.int32, (1, LANES), 1) % t
    qr, qi = _complex_pow(ar_c, ai_c, tau_tile, n_bits)
    qr = jnp.concatenate([qr] * (w // LANES), axis=1)
    qi = jnp.concatenate([qi] * (w // LANES), axis=1)
    expand = (lax.broadcasted_iota(jnp.int32, (n_ch, w), 1) // t
              == lax.broadcasted_iota(jnp.int32, (n_ch, w), 0)).astype(F32)
    c_re = jnp.dot(ct_re_ref[...], expand, preferred_element_type=F32, precision=HIGHEST)
    c_im = jnp.dot(ct_im_ref[...], expand, preferred_element_type=F32, precision=HIGHEST)
    d_rep = jnp.dot(jnp.broadcast_to(d_ref[...], (8, n_ch)), expand,
                    preferred_element_type=F32, precision=HIGHEST)[0:1, :]
    ca_r = c_re * qr - c_im * qi
    ca_i = c_re * qi + c_im * qr
    z = (jnp.dot(bbt_r, ca_r, preferred_element_type=F32, precision=HIGHEST)
         - jnp.dot(bbt_i, ca_i, preferred_element_type=F32, precision=HIGHEST))
    v_r = (ca_r * ar_c - ca_i * ai_c).astype(BF16)
    v_i = (ca_r * ai_c + ca_i * ar_c).astype(BF16)

    tau = lax.broadcasted_iota(jnp.int32, (1, w), 1) % t
    s_row = lax.broadcasted_iota(jnp.int32, (t, 1), 0)
    causal = tau >= s_row
    for c in range(n_ch):
        strip = jnp.broadcast_to(z[c:c + 1, :], (t, w))
        shifted = pltpu.roll(strip, 0, 1, stride=1, stride_axis=0)
        m_ref[c * t:(c + 1) * t, :] = jnp.where(causal, shifted, 0.0).astype(BF16)

    ub = u_ref[...]
    y = jnp.dot(ub, m_ref[...], preferred_element_type=F32)
    s_re_ref[...] = jnp.dot(ub, w_re_ref[...], preferred_element_type=F32)
    s_im_ref[...] = jnp.dot(ub, w_im_ref[...], preferred_element_type=F32)

    def step(r, carry):
        hr, hi = carry
        h_re_ref[pl.ds(r, 1), :] = hr
        h_im_ref[pl.ds(r, 1), :] = hi
        sr = s_re_ref[pl.ds(r, 1), :]
        si = s_im_ref[pl.ds(r, 1), :]
        return (at_r * hr - at_i * hi + sr, at_r * hi + at_i * hr + si)

    zero = jnp.zeros((1, SSM_STATE), F32)
    lax.fori_loop(0, n_chunks, step, (zero, zero))

    y = y + jnp.dot(h_re_ref[...].astype(BF16), v_r, preferred_element_type=F32)
    y = y - jnp.dot(h_im_ref[...].astype(BF16), v_i, preferred_element_type=F32)
    y = y + d_rep * ub.astype(F32)
    y_ref[...] = _gelu_tanh(y).astype(y_ref.dtype)


def _ssm_core(u_g, ldt, lam_re, lam_im, bt_re, bt_im, ct_re, ct_im, d_skip, layer):
    g, n_chunks, w = u_g.shape
    p = SSM_STATE
    c = SSM_GROUP_CH

    def pspec(*shape):
        return pl.BlockSpec((None, None) + shape, lambda i: (layer, i, 0, 0))

    data = pl.BlockSpec((None, n_chunks, w), lambda i: (i, 0, 0))
    return pl.pallas_call(
        functools.partial(_ssm_kernel, n_chunks=n_chunks),
        grid=(g,),
        in_specs=[data, pspec(1, 1), pspec(1, p), pspec(1, p), pspec(c, p), pspec(c, p),
                  pspec(p, c), pspec(p, c), pspec(1, c)],
        out_specs=data,
        out_shape=jax.ShapeDtypeStruct((g, n_chunks, w), BF16),
        scratch_shapes=[pltpu.VMEM((w, w), BF16),
                        pltpu.VMEM((w, p), BF16), pltpu.VMEM((w, p), BF16),
                        pltpu.VMEM((n_chunks, p), F32), pltpu.VMEM((n_chunks, p), F32),
                        pltpu.VMEM((n_chunks, p), F32), pltpu.VMEM((n_chunks, p), F32)],
        compiler_params=_cparams("parallel"),
        name="ssm_core",
    )(u_g, ldt, lam_re, lam_im, bt_re, bt_im, ct_re, ct_im, d_skip)


def _ssm_post_kernel(y_ref, w_ref, b_ref, g_ref, o_ref):
    yb = y_ref[...]
    y = yb.astype(F32)
    z = jnp.dot(w_ref[...], yb, preferred_element_type=F32) + b_ref[...]
    o = y * (1.0 / (1.0 + jnp.exp(-z)))
    w, tl = o.shape
    o3 = o.reshape(w // SSM_GROUP_CH, SSM_GROUP_CH, tl)
    ms = jnp.mean(o3 * o3, axis=1, keepdims=True)
    o3 = o3 * lax.rsqrt(ms + NORM_EPS)
    o_ref[...] = (o3.reshape(w, tl) * g_ref[...]).astype(o_ref.dtype)


def _ssm_post(y_t, glu_w_t, glu_b_col, g_col, layer, tl):
    w, l = y_t.shape
    col = pl.BlockSpec((None, w, 1), lambda i: (layer, 0, 0))
    return pl.pallas_call(
        _ssm_post_kernel,
        grid=(l // tl,),
        in_specs=[pl.BlockSpec((w, tl), lambda i: (0, i)),
                  pl.BlockSpec((None, w, w), lambda i: (layer, 0, 0)), col, col],
        out_specs=pl.BlockSpec((w, tl), lambda i: (0, i)),
        out_shape=jax.ShapeDtypeStruct((w, l), BF16),
        compiler_params=_cparams("parallel"),
        name="ssm_post",
    )(y_t, glu_w_t, glu_b_col, g_col)


def _layer_norm(y, g, b):
    mu = jnp.mean(y, axis=-1, keepdims=True)
    yc = y - mu
    var = jnp.mean(yc * yc, axis=-1, keepdims=True)
    return yc * lax.rsqrt(var + NORM_EPS) * g + b


def _out_proj_kernel(a_ref, s_ref, wa_ref, ws_ref, x_ref, g_ref, b_ref, o_ref, ob_ref):
    tn_dims = (((0,), (0,)), ((), ()))
    mix = lax.dot_general(a_ref[...], wa_ref[...], tn_dims, preferred_element_type=F32)
    mix = mix + lax.dot_general(s_ref[...], ws_ref[...], tn_dims, preferred_element_type=F32)
    y = _layer_norm(DEEPNORM_ALPHA * x_ref[...] + mix, g_ref[...], b_ref[...])
    o_ref[...] = y
    ob_ref[...] = y.astype(BF16)


def _out_proj_ln(attn_t, ssm_t, w_out_b, x, g, b, layer, tm):
    l, d = x.shape
    a = attn_t.shape[0]
    s = ssm_t.shape[0]
    assert a == s
    row = pl.BlockSpec((None, 1, d), lambda i: (layer, 0, 0))
    return pl.pallas_call(
        _out_proj_kernel,
        grid=(l // tm,),
        in_specs=[pl.BlockSpec((a, tm), lambda i: (0, i)),
                  pl.BlockSpec((s, tm), lambda i: (0, i)),
                  pl.BlockSpec((None, a, d), lambda i: (layer, 0, 0)),
                  pl.BlockSpec((None, s, d), lambda i: (layer, 1, 0)),
                  pl.BlockSpec((tm, d), lambda i: (i, 0)), row, row],
        out_specs=[pl.BlockSpec((tm, d), lambda i: (i, 0)),
                   pl.BlockSpec((tm, d), lambda i: (i, 0))],
        out_shape=[jax.ShapeDtypeStruct((l, d), F32), jax.ShapeDtypeStruct((l, d), BF16)],
        compiler_params=_cparams("parallel"),
        name="out_proj_ln",
    )(attn_t, ssm_t, w_out_b, w_out_b, x, g, b)


def _mlp_kernel(xb_ref, x_ref, wu_ref, wd_ref, g_ref, b_ref, o_ref, ob_ref, acc_ref):
    j = pl.program_id(1)

    @pl.when(j == 0)
    def _():
        acc_ref[...] = jnp.zeros(acc_ref.shape, F32)

    h = jnp.dot(xb_ref[...], wu_ref[...], preferred_element_type=F32)
    h = jnp.maximum(h, 0.0)
    h = (h * h).astype(BF16)
    acc_ref[...] += jnp.dot(h, wd_ref[...], preferred_element_type=F32)

    @pl.when(j == pl.num_programs(1) - 1)
    def _():
        y = _layer_norm(DEEPNORM_ALPHA * x_ref[...] + acc_ref[...], g_ref[...], b_ref[...])
        o_ref[...] = y
        ob_ref[...] = y.astype(BF16)


def _mlp_ln(xb, x, w_up_b, w_down_b, g, b, layer, tm, tf):
    l, d = x.shape
    f = w_up_b.shape[2]
    row = pl.BlockSpec((None, 1, d), lambda i, j: (layer, 0, 0))
    return pl.pallas_call(
        _mlp_kernel,
        grid=(l // tm, f // tf),
        in_specs=[pl.BlockSpec((tm, d), lambda i, j: (i, 0)),
                  pl.BlockSpec((tm, d), lambda i, j: (i, 0)),
                  pl.BlockSpec((None, d, tf), lambda i, j: (layer, 0, j)),
                  pl.BlockSpec((None, tf, d), lambda i, j: (layer, j, 0)), row, row],
        out_specs=[pl.BlockSpec((tm, d), lambda i, j: (i, 0)),
                   pl.BlockSpec((tm, d), lambda i, j: (i, 0))],
        out_shape=[jax.ShapeDtypeStruct((l, d), F32), jax.ShapeDtypeStruct((l, d), BF16)],
        scratch_shapes=[pltpu.VMEM((tm, d), F32)],
        compiler_params=_cparams("parallel", "arbitrary"),
        name="mlp_ln",
    )(xb, x, w_up_b, w_down_b, g, b)


def _pick(n, pref):
    while n % pref:
        pref //= 2
    return pref


def kernel(x, w_in, lambda_q1, lambda_k1, lambda_q2, lambda_k2, attn_norm_g, ssm_lambda_re, ssm_lambda_im, ssm_log_dt, ssm_b_re, ssm_b_im, ssm_c_re, ssm_c_im, ssm_d, glu_w, glu_b, ssm_norm_g, w_out, ln1_g, ln1_b, w_up, w_down, ln2_g, ln2_b):
    bsz, seq, d = x.shape
    depth = w_in.shape[0]
    attn_w = attn_norm_g.shape[1]
    ssm_w = ssm_d.shape[1]
    n_groups = ssm_w // SSM_GROUP_CH
    assert bsz == 1 and seq % CHUNK == 0
    n_chunks = seq // CHUNK
    p = SSM_STATE
    c = SSM_GROUP_CH

    w_in_b = w_in.astype(BF16)
    w_vu_t = w_in_b[:, :, 2 * attn_w:].transpose(0, 2, 1)
    w_out_b = w_out.astype(BF16)
    w_up_b = w_up.astype(BF16)
    w_down_b = w_down.astype(BF16)
    glu_w_t = glu_w.transpose(0, 2, 1).astype(BF16)
    lq1 = lambda_q1.reshape(depth, 1, HEAD_DIM)
    lk1 = lambda_k1.reshape(depth, 1, HEAD_DIM)
    lq2 = lambda_q2.reshape(depth, 1, HEAD_DIM)
    lk2 = lambda_k2.reshape(depth, 1, HEAD_DIM)
    attn_g_col = attn_norm_g.reshape(depth, attn_w, 1)
    ldt = ssm_log_dt.reshape(depth, n_groups, 1, 1)
    lam_re = ssm_lambda_re.reshape(depth, n_groups, 1, p)
    lam_im = ssm_lambda_im.reshape(depth, n_groups, 1, p)
    bt_re = ssm_b_re.transpose(0, 1, 3, 2)
    bt_im = ssm_b_im.transpose(0, 1, 3, 2)
    ct_re = ssm_c_re.transpose(0, 1, 3, 2)
    ct_im = ssm_c_im.transpose(0, 1, 3, 2)
    d_skip = ssm_d.reshape(depth, n_groups, 1, c)
    glu_b_col = glu_b.reshape(depth, ssm_w, 1)
    ssm_g_col = ssm_norm_g.reshape(depth, ssm_w, 1)
    ln1g, ln1b = ln1_g.reshape(depth, 1, d), ln1_b.reshape(depth, 1, d)
    ln2g, ln2b = ln2_g.reshape(depth, 1, d), ln2_b.reshape(depth, 1, d)

    blk = _pick(seq, 512)
    xf = x.reshape(seq, d)
    xb = xf.astype(BF16)
    for l in range(depth):
        lam_init = 0.8 - 0.6 * math.exp(-0.3 * l)
        qk = _proj_qk(xb, w_in_b, l, 2 * attn_w, attn_w, _pick(seq, 1024), 1024)
        v_t, u_t = _proj_vu(w_vu_t, xb, l, attn_w, _pick(seq, 512))

        attn_lo, attn_hi = _diff_attention(qk, v_t, lq1, lk1, lq2, lk2, attn_g_col,
                                           l, lam_init, blk)
        attn_t = jnp.concatenate([attn_lo, attn_hi], axis=1)

        u_g = (u_t.reshape(n_groups, c, n_chunks, CHUNK)
               .transpose(0, 2, 1, 3).reshape(n_groups, n_chunks, c * CHUNK))
        y_g = _ssm_core(u_g, ldt, lam_re, lam_im, bt_re, bt_im, ct_re, ct_im, d_skip, l)
        y_t = (y_g.reshape(n_groups, n_chunks, c, CHUNK)
               .transpose(0, 2, 1, 3).reshape(ssm_w, seq))
        ssm_t = _ssm_post(y_t, glu_w_t, glu_b_col, ssm_g_col, l, _pick(seq, 512))

        xf, xb = _out_proj_ln(attn_t, ssm_t, w_out_b, xf, ln1g, ln1b, l, _pick(seq, 512))
        xf, xb = _mlp_ln(xb, xf, w_up_b, w_down_b, ln2g, ln2b, l, _pick(seq, 512), 1024)
    return xf.reshape(bsz, seq, d)
```

```python
import functools
import math

import jax
import jax.numpy as jnp
from jax import lax
from jax.experimental import pallas as pl
from jax.experimental.pallas import tpu as pltpu

F32 = jnp.float32
BF16 = jnp.bfloat16

DEPTH = 4
HEAD_DIM = 64
HEAD_W = 2 * HEAD_DIM
CHUNK = 64
SSM_GROUP_CH = 16
SSM_STATE = 64
DEEPNORM_ALPHA = (2.0 * DEPTH) ** 0.25
NORM_EPS = 1e-5
MASK_VALUE = -1e30
QK_SCALE_LOG2E = HEAD_DIM ** -0.5 * math.log2(math.e)
ONES_ROWS = 16
UNROLL = 4
LANES = 128
VMEM_LIMIT = 56 * 1024 * 1024
HIGHEST = lax.Precision.HIGHEST


def _cparams(*sem):
    return pltpu.CompilerParams(dimension_semantics=sem, vmem_limit_bytes=VMEM_LIMIT)


def _mm_qk_kernel(a_ref, b_ref, o_ref, *, q_blocks):
    acc = jnp.dot(a_ref[...], b_ref[...], preferred_element_type=F32)
    scale = jnp.where(pl.program_id(1) < q_blocks, QK_SCALE_LOG2E, 1.0)
    o_ref[...] = (acc * scale).astype(o_ref.dtype)


def _proj_qk(xb, w_in_b, layer, n_cols, q_cols, tm, tn):
    m, k = xb.shape
    return pl.pallas_call(
        functools.partial(_mm_qk_kernel, q_blocks=q_cols // tn),
        grid=(m // tm, n_cols // tn),
        in_specs=[pl.BlockSpec((tm, k), lambda i, j: (i, 0)),
                  pl.BlockSpec((None, k, tn), lambda i, j: (layer, 0, j))],
        out_specs=pl.BlockSpec((tm, tn), lambda i, j: (i, j)),
        out_shape=jax.ShapeDtypeStruct((m, n_cols), BF16),
        compiler_params=_cparams("parallel", "arbitrary"),
        name="proj_qk",
    )(xb, w_in_b)


def _mm_vu_kernel(w_ref, x_ref, v_ref, u_ref):
    nt_dims = (((1,), (1,)), ((), ()))
    rows_v = v_ref.shape[0]
    x = x_ref[...]
    v_ref[...] = lax.dot_general(w_ref[:rows_v, :], x, nt_dims,
                                 preferred_element_type=F32).astype(v_ref.dtype)
    u_ref[...] = lax.dot_general(w_ref[rows_v:, :], x, nt_dims,
                                 preferred_element_type=F32).astype(u_ref.dtype)


def _proj_vu(w_vu_t, xb, layer, rows_v, tl):
    _, r, k = w_vu_t.shape
    l = xb.shape[0]
    rows_u = r - rows_v
    return pl.pallas_call(
        _mm_vu_kernel,
        grid=(l // tl,),
        in_specs=[pl.BlockSpec((None, r, k), lambda i: (layer, 0, 0)),
                  pl.BlockSpec((tl, k), lambda i: (i, 0))],
        out_specs=[pl.BlockSpec((rows_v, tl), lambda i: (0, i)),
                   pl.BlockSpec((rows_u, tl), lambda i: (0, i))],
        out_shape=[jax.ShapeDtypeStruct((rows_v, l), BF16),
                   jax.ShapeDtypeStruct((rows_u, l), BF16)],
        compiler_params=_cparams("parallel"),
        name="proj_vu",
    )(w_vu_t, xb)


def _attn_kernel(lq1_ref, lk1_ref, lq2_ref, lk2_ref, qa_ref, qb_ref, k_ref, vt_ref, g_ref,
                 oa_ref, ob_ref, qz_ref, s_ref, mb_ref, m_ref, acc_ref,
                 *, blk, n_qblk, lam_init):
    i = pl.program_id(1)
    blk_b = n_qblk - 1 - i
    nt_dims = (((1,), (1,)), ((), ()))

    lane = lax.broadcasted_iota(jnp.int32, (blk, HEAD_W), 1)
    for sel, q_ref in enumerate((qa_ref, qb_ref)):
        q = q_ref[...]
        zero = jnp.zeros_like(q)
        qz_ref[sel, 0] = jnp.where(lane < HEAD_DIM, q, zero)
        qz_ref[sel, 1] = jnp.where(lane >= HEAD_DIM, q, zero)
    m_ref[...] = jnp.full(m_ref.shape, MASK_VALUE, F32)
    acc_ref[...] = jnp.zeros(acc_ref.shape, F32)
    ones = jnp.ones((ONES_ROWS, blk), BF16)

    def full_item(k):
        idx = k - 2
        sel = (idx >= i).astype(jnp.int32)
        return idx - sel * i, sel

    def stage_a(block, sel, slot, diagonal):
        st = pl.multiple_of(block * blk, blk)
        kb = k_ref[pl.ds(st, blk), :]
        for mp in range(2):
            s = lax.dot_general(kb, qz_ref[sel, mp], nt_dims, preferred_element_type=F32)
            if diagonal:
                key_chunk = lax.broadcasted_iota(jnp.int32, (blk, blk), 0) // CHUNK
                qry_chunk = lax.broadcasted_iota(jnp.int32, (blk, blk), 1) // CHUNK
                s = jnp.where(key_chunk <= qry_chunk, s, MASK_VALUE)
            s_ref[slot, mp] = s
            mb_ref[slot, mp] = jnp.max(s, axis=0, keepdims=True)

    def stage_b(block, sel, slot):
        st = pl.multiple_of(block * blk, blk)
        v_ext = jnp.concatenate([vt_ref[:, pl.ds(st, blk)], ones], axis=0)
        for mp in range(2):
            m_old = m_ref[sel, mp]
            m_new = jnp.maximum(m_old, mb_ref[slot, mp])
            alpha = jnp.exp2(m_old - m_new)
            p = jnp.exp2(s_ref[slot, mp] - m_new).astype(BF16)
            acc_ref[sel, mp] = alpha * acc_ref[sel, mp] + jnp.dot(
                v_ext, p, preferred_element_type=F32)
            m_ref[sel, mp] = m_new

    def item(k):
        if isinstance(k, int) and k < 2:
            return (i, 0, True) if k == 0 else (blk_b, 1, True)
        return full_item(k) + (False,)

    def four_items(base):
        for q in range(UNROLL):
            nb, ns, nd = item(base + q + 1)
            stage_a(nb, ns, (q + 1) % 2, nd)
            cb, cs, _ = item(base + q)
            stage_b(cb, cs, q % 2)

    stage_a(i, 0, 0, True)
    four_items(0)

    def trip(t, carry):
        four_items(UNROLL * t)
        return carry

    lax.fori_loop(1, n_qblk // UNROLL, trip, 0)
    b_last, s_last = full_item(n_qblk)
    stage_b(b_last, s_last, 0)

    lam = (jnp.exp(jnp.sum(lq1_ref[...] * lk1_ref[...], axis=1, keepdims=True))
           - jnp.exp(jnp.sum(lq2_ref[...] * lk2_ref[...], axis=1, keepdims=True))
           + lam_init)
    for sel, o_ref in enumerate((oa_ref, ob_ref)):
        o = (acc_ref[sel, 0, :HEAD_W, :] / acc_ref[sel, 0, HEAD_W:HEAD_W + 1, :]
             - lam * (acc_ref[sel, 1, :HEAD_W, :] / acc_ref[sel, 1, HEAD_W:HEAD_W + 1, :]))
        ms = jnp.mean(o * o, axis=0, keepdims=True)
        o = o * lax.rsqrt(ms + NORM_EPS) * g_ref[...] * (1.0 - lam_init)
        o_ref[...] = o.astype(o_ref.dtype)


def _diff_attention(qk, v_t, lq1, lk1, lq2, lk2, g_col, layer, lam_init, blk):
    l = qk.shape[0]
    a = v_t.shape[0]
    n_heads = a // HEAD_W
    n_qblk = l // blk
    assert n_qblk % UNROLL == 0
    half = n_qblk // 2
    lam_spec = pl.BlockSpec((None, 1, HEAD_DIM), lambda h, i: (layer, 0, 0))
    out_sds = jax.ShapeDtypeStruct((a, l // 2), BF16)
    return pl.pallas_call(
        functools.partial(_attn_kernel, blk=blk, n_qblk=n_qblk, lam_init=lam_init),
        grid=(n_heads, half),
        in_specs=[lam_spec, lam_spec, lam_spec, lam_spec,
                  pl.BlockSpec((blk, HEAD_W), lambda h, i: (i, h)),
                  pl.BlockSpec((blk, HEAD_W), lambda h, i: (n_qblk - 1 - i, h)),
                  pl.BlockSpec((l, HEAD_W), lambda h, i: (0, n_heads + h)),
                  pl.BlockSpec((HEAD_W, l), lambda h, i: (h, 0)),
                  pl.BlockSpec((None, HEAD_W, 1), lambda h, i: (layer, h, 0))],
        out_specs=[pl.BlockSpec((HEAD_W, blk), lambda h, i: (h, i)),
                   pl.BlockSpec((HEAD_W, blk), lambda h, i: (h, half - 1 - i))],
        out_shape=[out_sds, out_sds],
        scratch_shapes=[pltpu.VMEM((2, 2, blk, HEAD_W), BF16),
                        pltpu.VMEM((2, 2, blk, blk), F32),
                        pltpu.VMEM((2, 2, 1, blk), F32),
                        pltpu.VMEM((2, 2, 1, blk), F32),
                        pltpu.VMEM((2, 2, HEAD_W + ONES_ROWS, blk), F32)],
        compiler_params=_cparams("parallel", "arbitrary"),
        name="diff_attn",
    )(lq1, lk1, lq2, lk2, qk, qk, qk, v_t, g_col)


def _complex_pow(ar, ai, e, n_bits):
    shape = jnp.broadcast_shapes(ar.shape, e.shape)
    pr = jnp.ones(shape, F32)
    pi = jnp.zeros(shape, F32)
    fr, fi = ar, ai
    for b in range(n_bits):
        bit = ((e >> b) & 1) == 1
        nr = pr * fr - pi * fi
        ni = pr * fi + pi * fr
        pr = jnp.where(bit, nr, pr)
        pi = jnp.where(bit, ni, pi)
        if b + 1 < n_bits:
            fr, fi = fr * fr - fi * fi, 2.0 * fr * fi
    return pr, pi


def _gelu_tanh(y):
    k0 = math.sqrt(2.0 / math.pi)
    return 0.5 * y * (1.0 + jnp.tanh(k0 * (y + 0.044715 * (y * y * y))))


def _row_to_col(row):
    n = row.shape[1]
    eye = (lax.broadcasted_iota(jnp.int32, (n, n), 0)
           == lax.broadcasted_iota(jnp.int32, (n, n), 1))
    return jnp.sum(jnp.where(eye, jnp.broadcast_to(row, (n, n)), 0.0), axis=1, keepdims=True)


def _ssm_kernel(u_ref, ldt_ref, lr_ref, li_ref, bt_re_ref, bt_im_ref, ct_re_ref, ct_im_ref,
                d_ref, y_ref, m_ref, w_re_ref, w_im_ref, s_re_ref, s_im_ref, h_re_ref, h_im_ref,
                fold_ref, ub_ref, *, n_chunks):
    t = CHUNK
    n_ch = SSM_GROUP_CH
    w = n_ch * t
    n_bits = t.bit_length() - 1
    dt = jnp.exp(ldt_ref[...])

    lr = lr_ref[...]
    li = li_ref[...]
    mag = jnp.exp(lr * dt)
    ar = mag * jnp.cos(li * dt)
    ai = mag * jnp.sin(li * dt)
    den = lr * lr + li * li
    nr = ar - 1.0
    fr = (nr * lr + ai * li) / den
    fi = (ai * lr - nr * li) / den
    bbt_r = fr * bt_re_ref[...] - fi * bt_im_ref[...]
    bbt_i = fr * bt_im_ref[...] + fi * bt_re_ref[...]

    rev = (t - 1) - lax.broadcasted_iota(jnp.int32, (t, 1), 0)
    pr, pi = _complex_pow(ar, ai, rev, n_bits)
    for c in range(n_ch):
        br = bbt_r[c:c + 1, :]
        bi = bbt_i[c:c + 1, :]
        w_re_ref[c * t:(c + 1) * t, :] = (pr * br - pi * bi).astype(BF16)
        w_im_ref[c * t:(c + 1) * t, :] = (pr * bi + pi * br).astype(BF16)
    at_r, at_i = _complex_pow(ar, ai, jnp.full((1, 1), t, jnp.int32), n_bits + 1)

    ar_c = _row_to_col(ar)
    ai_c = _row_to_col(ai)
    tau_tile = lax.broadcasted_iota(jnp.int32, (1, LANES), 1) % t
    qr, qi = _complex_pow(ar_c, ai_c, tau_tile, n_bits)
    qr = jnp.concatenate([qr] * (w // LANES), axis=1)
    qi = jnp.concatenate([qi] * (w // LANES), axis=1)
    expand = (lax.broadcasted_iota(jnp.int32, (n_ch, w), 1) // t
              == lax.broadcasted_iota(jnp.int32, (n_ch, w), 0)).astype(F32)
    c_re = jnp.dot(ct_re_ref[...], expand, preferred_element_type=F32, precision=HIGHEST)
    c_im = jnp.dot(ct_im_ref[...], expand, preferred_element_type=F32, precision=HIGHEST)
    d_rep = jnp.dot(jnp.broadcast_to(d_ref[...], (8, n_ch)), expand,
                    preferred_element_type=F32, precision=HIGHEST)[0:1, :]
    ca_r = c_re * qr - c_im * qi
    ca_i = c_re * qi + c_im * qr
    z = (jnp.dot(bbt_r, ca_r, preferred_element_type=F32, precision=HIGHEST)
         - jnp.dot(bbt_i, ca_i, preferred_element_type=F32, precision=HIGHEST))
    v_r = (ca_r * ar_c - ca_i * ai_c).astype(BF16)
    v_i = (ca_r * ai_c + ca_i * ar_c).astype(BF16)

    tau = lax.broadcasted_iota(jnp.int32, (1, w), 1) % t
    s_row = lax.broadcasted_iota(jnp.int32, (t, 1), 0)
    causal = tau >= s_row
    for c in range(n_ch):
        strip = jnp.broadcast_to(z[c:c + 1, :], (t, w))
        shifted = pltpu.roll(strip, 0, 1, stride=1, stride_axis=0)
        m_ref[c * t:(c + 1) * t, :] = jnp.where(causal, shifted, 0.0).astype(BF16)

    pairs = n_chunks // 2
    half_lane = lax.broadcasted_iota(jnp.int32, (pairs, LANES), 1) < t
    for j in range(pairs):
        fold_ref[j * n_ch:(j + 1) * n_ch, :] = u_ref[:, j * LANES:(j + 1) * LANES].astype(F32)
    for a in range(n_ch // 2):
        lo = fold_ref[pl.ds(2 * a, pairs, stride=n_ch), :]
        hi = fold_ref[pl.ds(2 * a + 1, pairs, stride=n_ch), :]
        even = jnp.where(half_lane, lo, pltpu.roll(hi, t, 1))
        odd = jnp.where(half_lane, pltpu.roll(lo, t, 1), hi)
        ub_ref[0:pairs, a * LANES:(a + 1) * LANES] = even.astype(BF16)
        ub_ref[pairs:n_chunks, a * LANES:(a + 1) * LANES] = odd.astype(BF16)

    ub = ub_ref[...]
    y = jnp.dot(ub, m_ref[...], preferred_element_type=F32)
    s_re_ref[...] = jnp.dot(ub, w_re_ref[...], preferred_element_type=F32)
    s_im_ref[...] = jnp.dot(ub, w_im_ref[...], preferred_element_type=F32)

    def step(j, carry):
        hr, hi = carry
        for r in (j, pairs + j):
            h_re_ref[pl.ds(r, 1), :] = hr
            h_im_ref[pl.ds(r, 1), :] = hi
            sr = s_re_ref[pl.ds(r, 1), :]
            si = s_im_ref[pl.ds(r, 1), :]
            hr, hi = at_r * hr - at_i * hi + sr, at_r * hi + at_i * hr + si
        return hr, hi

    zero = jnp.zeros((1, SSM_STATE), F32)
    lax.fori_loop(0, pairs, step, (zero, zero))

    y = y + jnp.dot(h_re_ref[...].astype(BF16), v_r, preferred_element_type=F32)
    y = y - jnp.dot(h_im_ref[...].astype(BF16), v_i, preferred_element_type=F32)
    y = _gelu_tanh(y + d_rep * ub.astype(F32))

    for a in range(n_ch // 2):
        even = y[0:pairs, a * LANES:(a + 1) * LANES]
        odd = y[pairs:n_chunks, a * LANES:(a + 1) * LANES]
        fold_ref[pl.ds(2 * a, pairs, stride=n_ch), :] = jnp.where(
            half_lane, even, pltpu.roll(odd, t, 1))
        fold_ref[pl.ds(2 * a + 1, pairs, stride=n_ch), :] = jnp.where(
            half_lane, pltpu.roll(even, t, 1), odd)
    for j in range(pairs):
        y_ref[:, j * LANES:(j + 1) * LANES] = fold_ref[j * n_ch:(j + 1) * n_ch, :].astype(y_ref.dtype)


def _ssm_core(u_t, ldt, lam_re, lam_im, bt_re, bt_im, ct_re, ct_im, d_skip, layer):
    width, l = u_t.shape
    p = SSM_STATE
    c = SSM_GROUP_CH
    g = width // c
    assert 2 * CHUNK == LANES and l % (16 * LANES) == 0
    n_chunks = l // CHUNK
    w = c * CHUNK

    def pspec(*shape):
        return pl.BlockSpec((None, None) + shape, lambda i: (layer, i, 0, 0))

    data = pl.BlockSpec((c, l), lambda i: (i, 0))
    return pl.pallas_call(
        functools.partial(_ssm_kernel, n_chunks=n_chunks),
        grid=(g,),
        in_specs=[data, pspec(1, 1), pspec(1, p), pspec(1, p), pspec(c, p), pspec(c, p),
                  pspec(p, c), pspec(p, c), pspec(1, c)],
        out_specs=data,
        out_shape=jax.ShapeDtypeStruct((width, l), BF16),
        scratch_shapes=[pltpu.VMEM((w, w), BF16),
                        pltpu.VMEM((w, p), BF16), pltpu.VMEM((w, p), BF16),
                        pltpu.VMEM((n_chunks, p), F32), pltpu.VMEM((n_chunks, p), F32),
                        pltpu.VMEM((n_chunks, p), F32), pltpu.VMEM((n_chunks, p), F32),
                        pltpu.VMEM((l // LANES * c, LANES), F32),
                        pltpu.VMEM((n_chunks, w), BF16)],
        compiler_params=_cparams("parallel"),
        name="ssm_core",
    )(u_t, ldt, lam_re, lam_im, bt_re, bt_im, ct_re, ct_im, d_skip)


def _ssm_post_kernel(y_ref, w_ref, b_ref, g_ref, o_ref):
    yb = y_ref[...]
    y = yb.astype(F32)
    z = jnp.dot(w_ref[...], yb, preferred_element_type=F32) + b_ref[...]
    o = y * (1.0 / (1.0 + jnp.exp(-z)))
    w, tl = o.shape
    o3 = o.reshape(w // SSM_GROUP_CH, SSM_GROUP_CH, tl)
    ms = jnp.mean(o3 * o3, axis=1, keepdims=True)
    o3 = o3 * lax.rsqrt(ms + NORM_EPS)
    o_ref[...] = (o3.reshape(w, tl) * g_ref[...]).astype(o_ref.dtype)


def _ssm_post(y_t, glu_w_t, glu_b_col, g_col, layer, tl):
    w, l = y_t.shape
    col = pl.BlockSpec((None, w, 1), lambda i: (layer, 0, 0))
    return pl.pallas_call(
        _ssm_post_kernel,
        grid=(l // tl,),
        in_specs=[pl.BlockSpec((w, tl), lambda i: (0, i)),
                  pl.BlockSpec((None, w, w), lambda i: (layer, 0, 0)), col, col],
        out_specs=pl.BlockSpec((w, tl), lambda i: (0, i)),
        out_shape=jax.ShapeDtypeStruct((w, l), BF16),
        compiler_params=_cparams("parallel"),
        name="ssm_post",
    )(y_t, glu_w_t, glu_b_col, g_col)


def _layer_norm(y, g, b):
    mu = jnp.mean(y, axis=-1, keepdims=True)
    yc = y - mu
    var = jnp.mean(yc * yc, axis=-1, keepdims=True)
    return yc * lax.rsqrt(var + NORM_EPS) * g + b


def _out_proj_kernel(alo_ref, ahi_ref, s_ref, wa_ref, ws_ref, x_ref, g_ref, b_ref, o_ref, ob_ref):
    tn_dims = (((0,), (0,)), ((), ()))
    first_half = pl.program_id(0) < pl.num_programs(0) // 2
    attn = jnp.where(first_half, alo_ref[...], ahi_ref[...])
    mix = lax.dot_general(attn, wa_ref[...], tn_dims, preferred_element_type=F32)
    mix = mix + lax.dot_general(s_ref[...], ws_ref[...], tn_dims, preferred_element_type=F32)
    y = _layer_norm(DEEPNORM_ALPHA * x_ref[...] + mix, g_ref[...], b_ref[...])
    o_ref[...] = y
    ob_ref[...] = y.astype(BF16)


def _out_proj_ln(attn_lo, attn_hi, ssm_t, w_out_b, x, g, b, layer, tm):
    l, d = x.shape
    a = attn_lo.shape[0]
    s = ssm_t.shape[0]
    assert a == s
    half = l // tm // 2
    row = pl.BlockSpec((None, 1, d), lambda i: (layer, 0, 0))
    return pl.pallas_call(
        _out_proj_kernel,
        grid=(l // tm,),
        in_specs=[pl.BlockSpec((a, tm), lambda i: (0, jnp.minimum(i, half - 1))),
                  pl.BlockSpec((a, tm), lambda i: (0, jnp.maximum(i - half, 0))),
                  pl.BlockSpec((s, tm), lambda i: (0, i)),
                  pl.BlockSpec((None, a, d), lambda i: (layer, 0, 0)),
                  pl.BlockSpec((None, s, d), lambda i: (layer, 1, 0)),
                  pl.BlockSpec((tm, d), lambda i: (i, 0)), row, row],
        out_specs=[pl.BlockSpec((tm, d), lambda i: (i, 0)),
                   pl.BlockSpec((tm, d), lambda i: (i, 0))],
        out_shape=[jax.ShapeDtypeStruct((l, d), F32), jax.ShapeDtypeStruct((l, d), BF16)],
        compiler_params=_cparams("parallel"),
        name="out_proj_ln",
    )(attn_lo, attn_hi, ssm_t, w_out_b, w_out_b, x, g, b)


def _mlp_kernel(xb_ref, x_ref, wu_ref, wd_ref, g_ref, b_ref, o_ref, ob_ref, acc_ref):
    j = pl.program_id(1)

    @pl.when(j == 0)
    def _():
        acc_ref[...] = jnp.zeros(acc_ref.shape, F32)

    h = jnp.dot(xb_ref[...], wu_ref[...], preferred_element_type=F32)
    h = jnp.maximum(h, 0.0)
    h = (h * h).astype(BF16)
    acc_ref[...] += jnp.dot(h, wd_ref[...], preferred_element_type=F32)

    @pl.when(j == pl.num_programs(1) - 1)
    def _():
        y = _layer_norm(DEEPNORM_ALPHA * x_ref[...] + acc_ref[...], g_ref[...], b_ref[...])
        o_ref[...] = y
        ob_ref[...] = y.astype(BF16)


def _mlp_ln(xb, x, w_up_b, w_down_b, g, b, layer, tm, tf):
    l, d = x.shape
    f = w_up_b.shape[2]
    row = pl.BlockSpec((None, 1, d), lambda i, j: (layer, 0, 0))
    return pl.pallas_call(
        _mlp_kernel,
        grid=(l // tm, f // tf),
        in_specs=[pl.BlockSpec((tm, d), lambda i, j: (i, 0)),
                  pl.BlockSpec((tm, d), lambda i, j: (i, 0)),
                  pl.BlockSpec((None, d, tf), lambda i, j: (layer, 0, j)),
                  pl.BlockSpec((None, tf, d), lambda i, j: (layer, j, 0)), row, row],
        out_specs=[pl.BlockSpec((tm, d), lambda i, j: (i, 0)),
                   pl.BlockSpec((tm, d), lambda i, j: (i, 0))],
        out_shape=[jax.ShapeDtypeStruct((l, d), F32), jax.ShapeDtypeStruct((l, d), BF16)],
        scratch_shapes=[pltpu.VMEM((tm, d), F32)],
        compiler_params=_cparams("parallel", "arbitrary"),
        name="mlp_ln",
    )(xb, x, w_up_b, w_down_b, g, b)


def _pick(n, pref):
    while n % pref:
        pref //= 2
    return pref


def kernel(x, w_in, lambda_q1, lambda_k1, lambda_q2, lambda_k2, attn_norm_g, ssm_lambda_re, ssm_lambda_im, ssm_log_dt, ssm_b_re, ssm_b_im, ssm_c_re, ssm_c_im, ssm_d, glu_w, glu_b, ssm_norm_g, w_out, ln1_g, ln1_b, w_up, w_down, ln2_g, ln2_b):
    bsz, seq, d = x.shape
    depth = w_in.shape[0]
    attn_w = attn_norm_g.shape[1]
    ssm_w = ssm_d.shape[1]
    n_groups = ssm_w // SSM_GROUP_CH
    assert bsz == 1 and seq % CHUNK == 0
    n_chunks = seq // CHUNK
    p = SSM_STATE
    c = SSM_GROUP_CH

    w_in_b = w_in.astype(BF16)
    w_vu_t = w_in_b[:, :, 2 * attn_w:].transpose(0, 2, 1)
    w_out_b = w_out.astype(BF16)
    w_up_b = w_up.astype(BF16)
    w_down_b = w_down.astype(BF16)
    glu_w_t = glu_w.transpose(0, 2, 1).astype(BF16)
    lq1 = lambda_q1.reshape(depth, 1, HEAD_DIM)
    lk1 = lambda_k1.reshape(depth, 1, HEAD_DIM)
    lq2 = lambda_q2.reshape(depth, 1, HEAD_DIM)
    lk2 = lambda_k2.reshape(depth, 1, HEAD_DIM)
    attn_g_col = attn_norm_g.reshape(depth, attn_w, 1)
    ldt = ssm_log_dt.reshape(depth, n_groups, 1, 1)
    lam_re = ssm_lambda_re.reshape(depth, n_groups, 1, p)
    lam_im = ssm_lambda_im.reshape(depth, n_groups, 1, p)
    bt_re = ssm_b_re.transpose(0, 1, 3, 2)
    bt_im = ssm_b_im.transpose(0, 1, 3, 2)
    ct_re = ssm_c_re.transpose(0, 1, 3, 2)
    ct_im = ssm_c_im.transpose(0, 1, 3, 2)
    d_skip = ssm_d.reshape(depth, n_groups, 1, c)
    glu_b_col = glu_b.reshape(depth, ssm_w, 1)
    ssm_g_col = ssm_norm_g.reshape(depth, ssm_w, 1)
    ln1g, ln1b = ln1_g.reshape(depth, 1, d), ln1_b.reshape(depth, 1, d)
    ln2g, ln2b = ln2_g.reshape(depth, 1, d), ln2_b.reshape(depth, 1, d)

    blk = _pick(seq, 512)
    xf = x.reshape(seq, d)
    xb = xf.astype(BF16)
    for l in range(depth):
        lam_init = 0.8 - 0.6 * math.exp(-0.3 * l)
        qk = _proj_qk(xb, w_in_b, l, 2 * attn_w, attn_w, _pick(seq, 1024), 1024)
        v_t, u_t = _proj_vu(w_vu_t, xb, l, attn_w, _pick(seq, 512))

        attn_lo, attn_hi = _diff_attention(qk, v_t, lq1, lk1, lq2, lk2, attn_g_col,
                                           l, lam_init, blk)

        y_t = _ssm_core(u_t, ldt, lam_re, lam_im, bt_re, bt_im, ct_re, ct_im, d_skip, l)
        ssm_t = _ssm_post(y_t, glu_w_t, glu_b_col, ssm_g_col, l, _pick(seq, 512))

        xf, xb = _out_proj_ln(attn_lo, attn_hi, ssm_t, w_out_b, xf, ln1g, ln1b, l, _pick(seq, 512))
        xf, xb = _mlp_ln(xb, xf, w_up_b, w_down_b, ln2g, ln2b, l, _pick(seq, 512), 1024)
    return xf.reshape(bsz, seq, d)
```

```python
import functools
import math

import jax
import jax.numpy as jnp
from jax import lax
from jax.experimental import pallas as pl
from jax.experimental.pallas import tpu as pltpu

F32 = jnp.float32
BF16 = jnp.bfloat16

DEPTH = 4
HEAD_DIM = 64
HEAD_W = 2 * HEAD_DIM
CHUNK = 64
SSM_GROUP_CH = 16
SSM_STATE = 64
DEEPNORM_ALPHA = (2.0 * DEPTH) ** 0.25
NORM_EPS = 1e-5
MASK_VALUE = -1e30
QK_SCALE_LOG2E = HEAD_DIM ** -0.5 * math.log2(math.e)
ONES_ROWS = 16
UNROLL = 8
NQ = 4
LANES = 128
VMEM_LIMIT = 56 * 1024 * 1024
HIGHEST = lax.Precision.HIGHEST


def _cparams(*sem):
    return pltpu.CompilerParams(dimension_semantics=sem, vmem_limit_bytes=VMEM_LIMIT)


def _mm_qk_kernel(a_ref, b_ref, o_ref, *, q_blocks):
    acc = jnp.dot(a_ref[...], b_ref[...], preferred_element_type=F32)
    scale = jnp.where(pl.program_id(1) < q_blocks, QK_SCALE_LOG2E, 1.0)
    o_ref[...] = (acc * scale).astype(o_ref.dtype)


def _proj_qk(xb, w_in_b, layer, n_cols, q_cols, tm, tn):
    m, k = xb.shape
    return pl.pallas_call(
        functools.partial(_mm_qk_kernel, q_blocks=q_cols // tn),
        grid=(m // tm, n_cols // tn),
        in_specs=[pl.BlockSpec((tm, k), lambda i, j: (i, 0)),
                  pl.BlockSpec((None, k, tn), lambda i, j: (layer, 0, j))],
        out_specs=pl.BlockSpec((tm, tn), lambda i, j: (i, j)),
        out_shape=jax.ShapeDtypeStruct((m, n_cols), BF16),
        compiler_params=_cparams("parallel", "arbitrary"),
        name="proj_qk",
    )(xb, w_in_b)


def _mm_vu_kernel(w_ref, x_ref, v_ref, u_ref):
    nt_dims = (((1,), (1,)), ((), ()))
    rows_v = v_ref.shape[0]
    x = x_ref[...]
    v_ref[...] = lax.dot_general(w_ref[:rows_v, :], x, nt_dims,
                                 preferred_element_type=F32).astype(v_ref.dtype)
    u_ref[...] = lax.dot_general(w_ref[rows_v:, :], x, nt_dims,
                                 preferred_element_type=F32).astype(u_ref.dtype)


def _proj_vu(w_vu_t, xb, layer, rows_v, tl):
    _, r, k = w_vu_t.shape
    l = xb.shape[0]
    rows_u = r - rows_v
    return pl.pallas_call(
        _mm_vu_kernel,
        grid=(l // tl,),
        in_specs=[pl.BlockSpec((None, r, k), lambda i: (layer, 0, 0)),
                  pl.BlockSpec((tl, k), lambda i: (i, 0))],
        out_specs=[pl.BlockSpec((rows_v, tl), lambda i: (0, i)),
                   pl.BlockSpec((rows_u, tl), lambda i: (0, i))],
        out_shape=[jax.ShapeDtypeStruct((rows_v, l), BF16),
                   jax.ShapeDtypeStruct((rows_u, l), BF16)],
        compiler_params=_cparams("parallel"),
        name="proj_vu",
    )(w_vu_t, xb)


def _query_blocks(i, n_qblk):
    quarter = n_qblk // NQ
    return (i, 2 * quarter - 1 - i, 2 * quarter + i, n_qblk - 1 - i)


def _attn_kernel(lq1_ref, lk1_ref, lq2_ref, lk2_ref, q0_ref, q1_ref, q2_ref, q3_ref,
                 k_ref, vt_ref, g_ref, o0_ref, o1_ref, o2_ref, o3_ref,
                 qz_ref, s_ref, mb_ref, m_ref, acc_ref, *, blk, n_qblk, lam_init):
    i = pl.program_id(1)
    q_blocks = _query_blocks(i, n_qblk)
    starts = (0, q_blocks[0], q_blocks[0] + q_blocks[1], q_blocks[0] + q_blocks[1] + q_blocks[2])
    n_items = NQ + 2 * n_qblk - 2
    nt_dims = (((1,), (1,)), ((), ()))

    lane = lax.broadcasted_iota(jnp.int32, (blk, HEAD_W), 1)
    for sel, q_ref in enumerate((q0_ref, q1_ref, q2_ref, q3_ref)):
        q = q_ref[...]
        zero = jnp.zeros_like(q)
        qz_ref[sel, 0] = jnp.where(lane < HEAD_DIM, q, zero)
        qz_ref[sel, 1] = jnp.where(lane >= HEAD_DIM, q, zero)
    m_ref[...] = jnp.full(m_ref.shape, MASK_VALUE, F32)
    acc_ref[...] = jnp.zeros(acc_ref.shape, F32)
    ones = jnp.ones((ONES_ROWS, blk), BF16)

    def stage_a(block, sel, slot, diagonal):
        st = pl.multiple_of(block * blk, blk)
        kb = k_ref[pl.ds(st, blk), :]
        for mp in range(2):
            s = lax.dot_general(kb, qz_ref[sel, mp], nt_dims, preferred_element_type=F32)
            if diagonal:
                key_chunk = lax.broadcasted_iota(jnp.int32, (blk, blk), 0) // CHUNK
                qry_chunk = lax.broadcasted_iota(jnp.int32, (blk, blk), 1) // CHUNK
                s = jnp.where(key_chunk <= qry_chunk, s, MASK_VALUE)
            s_ref[slot, mp] = s
            mb_ref[slot, mp] = jnp.max(s, axis=0, keepdims=True)

    def stage_b(block, sel, slot):
        st = pl.multiple_of(block * blk, blk)
        v_ext = jnp.concatenate([vt_ref[:, pl.ds(st, blk)], ones], axis=0)
        for mp in range(2):
            m_old = m_ref[sel, mp]
            m_new = jnp.maximum(m_old, mb_ref[slot, mp])
            alpha = jnp.exp2(m_old - m_new)
            p = jnp.exp2(s_ref[slot, mp] - m_new).astype(BF16)
            acc_ref[sel, mp] = alpha * acc_ref[sel, mp] + jnp.dot(
                v_ext, p, preferred_element_type=F32)
            m_ref[sel, mp] = m_new

    def item(k):
        if isinstance(k, int) and k < NQ:
            return q_blocks[k], k, True
        idx = k - NQ
        sel = sum((idx >= st).astype(jnp.int32) for st in starts[1:])
        first = jnp.where(sel == 0, starts[0], jnp.where(sel == 1, starts[1],
                          jnp.where(sel == 2, starts[2], starts[3])))
        return idx - first, sel, False

    def run_items(base, count):
        for q in range(count):
            nb, ns, nd = item(base + q + 1)
            stage_a(nb, ns, q % 2, nd)
            cb, cs, _ = item(base + q)
            stage_b(cb, cs, (q + 1) % 2)

    stage_a(q_blocks[0], 0, 0, True)
    nb, ns, nd = item(1)
    stage_a(nb, ns, 1, nd)
    stage_b(q_blocks[0], 0, 0)
    run_items(1, UNROLL)

    def trip(t, carry):
        run_items(1 + UNROLL * t, UNROLL)
        return carry

    lax.fori_loop(1, (n_items - 2) // UNROLL, trip, 0)
    b_last, s_last, _ = item(n_items - 1)
    stage_b(b_last, s_last, (n_items - 1) % 2)

    lam = (jnp.exp(jnp.sum(lq1_ref[...] * lk1_ref[...], axis=1, keepdims=True))
           - jnp.exp(jnp.sum(lq2_ref[...] * lk2_ref[...], axis=1, keepdims=True))
           + lam_init)
    for sel, o_ref in enumerate((o0_ref, o1_ref, o2_ref, o3_ref)):
        o = (acc_ref[sel, 0, :HEAD_W, :] / acc_ref[sel, 0, HEAD_W:HEAD_W + 1, :]
             - lam * (acc_ref[sel, 1, :HEAD_W, :] / acc_ref[sel, 1, HEAD_W:HEAD_W + 1, :]))
        ms = jnp.mean(o * o, axis=0, keepdims=True)
        o = o * lax.rsqrt(ms + NORM_EPS) * g_ref[...] * (1.0 - lam_init)
        o_ref[...] = o.astype(o_ref.dtype)


def _diff_attention(qk, v_t, lq1, lk1, lq2, lk2, g_col, layer, lam_init, blk):
    l = qk.shape[0]
    a = v_t.shape[0]
    n_heads = a // HEAD_W
    n_qblk = l // blk
    assert NQ == 4 and n_qblk % NQ == 0 and (2 * n_qblk) % UNROLL == 0
    quarter = n_qblk // NQ
    lam_spec = pl.BlockSpec((None, 1, HEAD_DIM), lambda h, i: (layer, 0, 0))
    out_sds = jax.ShapeDtypeStruct((a, l // NQ), BF16)

    def q_spec(sel):
        return pl.BlockSpec((blk, HEAD_W), lambda h, i: (_query_blocks(i, n_qblk)[sel], h))

    def o_spec(sel):
        return pl.BlockSpec((HEAD_W, blk),
                            lambda h, i: (h, _query_blocks(i, n_qblk)[sel] - sel * quarter))

    return pl.pallas_call(
        functools.partial(_attn_kernel, blk=blk, n_qblk=n_qblk, lam_init=lam_init),
        grid=(n_heads, quarter),
        in_specs=[lam_spec, lam_spec, lam_spec, lam_spec] + [q_spec(sel) for sel in range(NQ)] + [
            pl.BlockSpec((l, HEAD_W), lambda h, i: (0, n_heads + h)),
            pl.BlockSpec((HEAD_W, l), lambda h, i: (h, 0)),
            pl.BlockSpec((None, HEAD_W, 1), lambda h, i: (layer, h, 0))],
        out_specs=[o_spec(sel) for sel in range(NQ)],
        out_shape=[out_sds] * NQ,
        scratch_shapes=[pltpu.VMEM((NQ, 2, blk, HEAD_W), BF16),
                        pltpu.VMEM((2, 2, blk, blk), F32),
                        pltpu.VMEM((2, 2, 1, blk), F32),
                        pltpu.VMEM((NQ, 2, 1, blk), F32),
                        pltpu.VMEM((NQ, 2, HEAD_W + ONES_ROWS, blk), F32)],
        compiler_params=_cparams("parallel", "arbitrary"),
        name="diff_attn",
    )(lq1, lk1, lq2, lk2, qk, qk, qk, qk, qk, v_t, g_col)


def _complex_pow(ar, ai, e, n_bits):
    shape = jnp.broadcast_shapes(ar.shape, e.shape)
    pr = jnp.ones(shape, F32)
    pi = jnp.zeros(shape, F32)
    fr, fi = ar, ai
    for b in range(n_bits):
        bit = ((e >> b) & 1) == 1
        nr = pr * fr - pi * fi
        ni = pr * fi + pi * fr
        pr = jnp.where(bit, nr, pr)
        pi = jnp.where(bit, ni, pi)
        if b + 1 < n_bits:
            fr, fi = fr * fr - fi * fi, 2.0 * fr * fi
    return pr, pi


def _gelu_tanh(y):
    k0 = math.sqrt(2.0 / math.pi)
    return 0.5 * y * (1.0 + jnp.tanh(k0 * (y + 0.044715 * (y * y * y))))


def _row_to_col(row):
    n = row.shape[1]
    eye = (lax.broadcasted_iota(jnp.int32, (n, n), 0)
           == lax.broadcasted_iota(jnp.int32, (n, n), 1))
    return jnp.sum(jnp.where(eye, jnp.broadcast_to(row, (n, n)), 0.0), axis=1, keepdims=True)


def _ssm_kernel(u_ref, ldt_ref, lr_ref, li_ref, bt_re_ref, bt_im_ref, ct_re_ref, ct_im_ref,
                d_ref, y_ref, m_ref, w_re_ref, w_im_ref, s_re_ref, s_im_ref, h_re_ref, h_im_ref,
                fold_ref, ub_ref, *, n_chunks):
    t = CHUNK
    n_ch = SSM_GROUP_CH
    w = n_ch * t
    n_bits = t.bit_length() - 1
    dt = jnp.exp(ldt_ref[...])

    lr = lr_ref[...]
    li = li_ref[...]
    mag = jnp.exp(lr * dt)
    ar = mag * jnp.cos(li * dt)
    ai = mag * jnp.sin(li * dt)
    den = lr * lr + li * li
    nr = ar - 1.0
    fr = (nr * lr + ai * li) / den
    fi = (ai * lr - nr * li) / den
    bbt_r = fr * bt_re_ref[...] - fi * bt_im_ref[...]
    bbt_i = fr * bt_im_ref[...] + fi * bt_re_ref[...]

    rev = (t - 1) - lax.broadcasted_iota(jnp.int32, (t, 1), 0)
    pr, pi = _complex_pow(ar, ai, rev, n_bits)
    for c in range(n_ch):
        br = bbt_r[c:c + 1, :]
        bi = bbt_i[c:c + 1, :]
        w_re_ref[c * t:(c + 1) * t, :] = (pr * br - pi * bi).astype(BF16)
        w_im_ref[c * t:(c + 1) * t, :] = (pr * bi + pi * br).astype(BF16)
    at_r, at_i = _complex_pow(ar, ai, jnp.full((1, 1), t, jnp.int32), n_bits + 1)

    ar_c = _row_to_col(ar)
    ai_c = _row_to_col(ai)
    tau_tile = lax.broadcasted_iota(jnp.int32, (1, LANES), 1) % t
    qr, qi = _complex_pow(ar_c, ai_c, tau_tile, n_bits)
    qr = jnp.concatenate([qr] * (w // LANES), axis=1)
    qi = jnp.concatenate([qi] * (w // LANES), axis=1)
    expand = (lax.broadcasted_iota(jnp.int32, (n_ch, w), 1) // t
              == lax.broadcasted_iota(jnp.int32, (n_ch, w), 0)).astype(F32)
    c_re = jnp.dot(ct_re_ref[...], expand, preferred_element_type=F32, precision=HIGHEST)
    c_im = jnp.dot(ct_im_ref[...], expand, preferred_element_type=F32, precision=HIGHEST)
    d_rep = jnp.dot(jnp.broadcast_to(d_ref[...], (8, n_ch)), expand,
                    preferred_element_type=F32, precision=HIGHEST)[0:1, :]
    ca_r = c_re * qr - c_im * qi
    ca_i = c_re * qi + c_im * qr
    z = (jnp.dot(bbt_r, ca_r, preferred_element_type=F32, precision=HIGHEST)
         - jnp.dot(bbt_i, ca_i, preferred_element_type=F32, precision=HIGHEST))
    v_r = (ca_r * ar_c - ca_i * ai_c).astype(BF16)
    v_i = (ca_r * ai_c + ca_i * ar_c).astype(BF16)

    tau = lax.broadcasted_iota(jnp.int32, (1, w), 1) % t
    s_row = lax.broadcasted_iota(jnp.int32, (t, 1), 0)
    causal = tau >= s_row
    for c in range(n_ch):
        strip = jnp.broadcast_to(z[c:c + 1, :], (t, w))
        shifted = pltpu.roll(strip, 0, 1, stride=1, stride_axis=0)
        m_ref[c * t:(c + 1) * t, :] = jnp.where(causal, shifted, 0.0).astype(BF16)

    pairs = n_chunks // 2
    half_lane = lax.broadcasted_iota(jnp.int32, (pairs, LANES), 1) < t
    for j in range(pairs):
        fold_ref[j * n_ch:(j + 1) * n_ch, :] = u_ref[:, j * LANES:(j + 1) * LANES].astype(F32)
    for a in range(n_ch // 2):
        lo = fold_ref[pl.ds(2 * a, pairs, stride=n_ch), :]
        hi = fold_ref[pl.ds(2 * a + 1, pairs, stride=n_ch), :]
        even = jnp.where(half_lane, lo, pltpu.roll(hi, t, 1))
        odd = jnp.where(half_lane, pltpu.roll(lo, t, 1), hi)
        ub_ref[0:pairs, a * LANES:(a + 1) * LANES] = even.astype(BF16)
        ub_ref[pairs:n_chunks, a * LANES:(a + 1) * LANES] = odd.astype(BF16)

    ub = ub_ref[...]
    y = jnp.dot(ub, m_ref[...], preferred_element_type=F32)
    s_re_ref[...] = jnp.dot(ub, w_re_ref[...], preferred_element_type=F32)
    s_im_ref[...] = jnp.dot(ub, w_im_ref[...], preferred_element_type=F32)

    def step(j, carry):
        hr, hi = carry
        for r in (j, pairs + j):
            h_re_ref[pl.ds(r, 1), :] = hr
            h_im_ref[pl.ds(r, 1), :] = hi
            sr = s_re_ref[pl.ds(r, 1), :]
            si = s_im_ref[pl.ds(r, 1), :]
            hr, hi = at_r * hr - at_i * hi + sr, at_r * hi + at_i * hr + si
        return hr, hi

    zero = jnp.zeros((1, SSM_STATE), F32)
    lax.fori_loop(0, pairs, step, (zero, zero))

    y = y + jnp.dot(h_re_ref[...].astype(BF16), v_r, preferred_element_type=F32)
    y = y - jnp.dot(h_im_ref[...].astype(BF16), v_i, preferred_element_type=F32)
    y = _gelu_tanh(y + d_rep * ub.astype(F32))

    for a in range(n_ch // 2):
        even = y[0:pairs, a * LANES:(a + 1) * LANES]
        odd = y[pairs:n_chunks, a * LANES:(a + 1) * LANES]
        fold_ref[pl.ds(2 * a, pairs, stride=n_ch), :] = jnp.where(
            half_lane, even, pltpu.roll(odd, t, 1))
        fold_ref[pl.ds(2 * a + 1, pairs, stride=n_ch), :] = jnp.where(
            half_lane, pltpu.roll(even, t, 1), odd)
    for j in range(pairs):
        y_ref[:, j * LANES:(j + 1) * LANES] = fold_ref[j * n_ch:(j + 1) * n_ch, :].astype(y_ref.dtype)


def _ssm_core(u_t, ldt, lam_re, lam_im, bt_re, bt_im, ct_re, ct_im, d_skip, layer):
    width, l = u_t.shape
    p = SSM_STATE
    c = SSM_GROUP_CH
    g = width // c
    assert 2 * CHUNK == LANES and l % (16 * LANES) == 0
    n_chunks = l // CHUNK
    w = c * CHUNK

    def pspec(*shape):
        return pl.BlockSpec((None, None) + shape, lambda i: (layer, i, 0, 0))

    data = pl.BlockSpec((c, l), lambda i: (i, 0))
    return pl.pallas_call(
        functools.partial(_ssm_kernel, n_chunks=n_chunks),
        grid=(g,),
        in_specs=[data, pspec(1, 1), pspec(1, p), pspec(1, p), pspec(c, p), pspec(c, p),
                  pspec(p, c), pspec(p, c), pspec(1, c)],
        out_specs=data,
        out_shape=jax.ShapeDtypeStruct((width, l), BF16),
        scratch_shapes=[pltpu.VMEM((w, w), BF16),
                        pltpu.VMEM((w, p), BF16), pltpu.VMEM((w, p), BF16),
                        pltpu.VMEM((n_chunks, p), F32), pltpu.VMEM((n_chunks, p), F32),
                        pltpu.VMEM((n_chunks, p), F32), pltpu.VMEM((n_chunks, p), F32),
                        pltpu.VMEM((l // LANES * c, LANES), F32),
                        pltpu.VMEM((n_chunks, w), BF16)],
        compiler_params=_cparams("parallel"),
        name="ssm_core",
    )(u_t, ldt, lam_re, lam_im, bt_re, bt_im, ct_re, ct_im, d_skip)


def _ssm_post_kernel(y_ref, w_ref, b_ref, g_ref, o_ref):
    yb = y_ref[...]
    y = yb.astype(F32)
    z = jnp.dot(w_ref[...], yb, preferred_element_type=F32) + b_ref[...]
    o = y * (1.0 / (1.0 + jnp.exp(-z)))
    w, tl = o.shape
    o3 = o.reshape(w // SSM_GROUP_CH, SSM_GROUP_CH, tl)
    ms = jnp.mean(o3 * o3, axis=1, keepdims=True)
    o3 = o3 * lax.rsqrt(ms + NORM_EPS)
    o_ref[...] = (o3.reshape(w, tl) * g_ref[...]).astype(o_ref.dtype)


def _ssm_post(y_t, glu_w_t, glu_b_col, g_col, layer, tl):
    w, l = y_t.shape
    col = pl.BlockSpec((None, w, 1), lambda i: (layer, 0, 0))
    return pl.pallas_call(
        _ssm_post_kernel,
        grid=(l // tl,),
        in_specs=[pl.BlockSpec((w, tl), lambda i: (0, i)),
                  pl.BlockSpec((None, w, w), lambda i: (layer, 0, 0)), col, col],
        out_specs=pl.BlockSpec((w, tl), lambda i: (0, i)),
        out_shape=jax.ShapeDtypeStruct((w, l), BF16),
        compiler_params=_cparams("parallel"),
        name="ssm_post",
    )(y_t, glu_w_t, glu_b_col, g_col)


def _layer_norm(y, g, b):
    mu = jnp.mean(y, axis=-1, keepdims=True)
    yc = y - mu
    var = jnp.mean(yc * yc, axis=-1, keepdims=True)
    return yc * lax.rsqrt(var + NORM_EPS) * g + b


def _out_proj_kernel(a0_ref, a1_ref, a2_ref, a3_ref, s_ref, wa_ref, ws_ref, x_ref, g_ref, b_ref,
                     o_ref, ob_ref):
    tn_dims = (((0,), (0,)), ((), ()))
    piece = pl.program_id(0) // (pl.num_programs(0) // NQ)
    attn = jnp.where(piece == 0, a0_ref[...], jnp.where(
        piece == 1, a1_ref[...], jnp.where(piece == 2, a2_ref[...], a3_ref[...])))
    mix = lax.dot_general(attn, wa_ref[...], tn_dims, preferred_element_type=F32)
    mix = mix + lax.dot_general(s_ref[...], ws_ref[...], tn_dims, preferred_element_type=F32)
    y = _layer_norm(DEEPNORM_ALPHA * x_ref[...] + mix, g_ref[...], b_ref[...])
    o_ref[...] = y
    ob_ref[...] = y.astype(BF16)


def _out_proj_ln(attn_pieces, ssm_t, w_out_b, x, g, b, layer, tm):
    l, d = x.shape
    a = attn_pieces[0].shape[0]
    s = ssm_t.shape[0]
    assert a == s
    per = l // tm // NQ

    def piece_spec(c):
        return pl.BlockSpec((a, tm), lambda i: (0, jnp.clip(i - c * per, 0, per - 1)))

    row = pl.BlockSpec((None, 1, d), lambda i: (layer, 0, 0))
    return pl.pallas_call(
        _out_proj_kernel,
        grid=(l // tm,),
        in_specs=[piece_spec(c) for c in range(NQ)] + [
                  pl.BlockSpec((s, tm), lambda i: (0, i)),
                  pl.BlockSpec((None, a, d), lambda i: (layer, 0, 0)),
                  pl.BlockSpec((None, s, d), lambda i: (layer, 1, 0)),
                  pl.BlockSpec((tm, d), lambda i: (i, 0)), row, row],
        out_specs=[pl.BlockSpec((tm, d), lambda i: (i, 0)),
                   pl.BlockSpec((tm, d), lambda i: (i, 0))],
        out_shape=[jax.ShapeDtypeStruct((l, d), F32), jax.ShapeDtypeStruct((l, d), BF16)],
        compiler_params=_cparams("parallel"),
        name="out_proj_ln",
    )(*attn_pieces, ssm_t, w_out_b, w_out_b, x, g, b)


def _mlp_kernel(xb_ref, x_ref, wu_ref, wd_ref, g_ref, b_ref, o_ref, ob_ref, acc_ref):
    j = pl.program_id(1)

    @pl.when(j == 0)
    def _():
        acc_ref[...] = jnp.zeros(acc_ref.shape, F32)

    h = jnp.dot(xb_ref[...], wu_ref[...], preferred_element_type=F32)
    h = jnp.maximum(h, 0.0)
    h = (h * h).astype(BF16)
    acc_ref[...] += jnp.dot(h, wd_ref[...], preferred_element_type=F32)

    @pl.when(j == pl.num_programs(1) - 1)
    def _():
        y = _layer_norm(DEEPNORM_ALPHA * x_ref[...] + acc_ref[...], g_ref[...], b_ref[...])
        o_ref[...] = y
        ob_ref[...] = y.astype(BF16)


def _mlp_ln(xb, x, w_up_b, w_down_b, g, b, layer, tm, tf):
    l, d = x.shape
    f = w_up_b.shape[2]
    row = pl.BlockSpec((None, 1, d), lambda i, j: (layer, 0, 0))
    return pl.pallas_call(
        _mlp_kernel,
        grid=(l // tm, f // tf),
        in_specs=[pl.BlockSpec((tm, d), lambda i, j: (i, 0)),
                  pl.BlockSpec((tm, d), lambda i, j: (i, 0)),
                  pl.BlockSpec((None, d, tf), lambda i, j: (layer, 0, j)),
                  pl.BlockSpec((None, tf, d), lambda i, j: (layer, j, 0)), row, row],
        out_specs=[pl.BlockSpec((tm, d), lambda i, j: (i, 0)),
                   pl.BlockSpec((tm, d), lambda i, j: (i, 0))],
        out_shape=[jax.ShapeDtypeStruct((l, d), F32), jax.ShapeDtypeStruct((l, d), BF16)],
        scratch_shapes=[pltpu.VMEM((tm, d), F32)],
        compiler_params=_cparams("parallel", "arbitrary"),
        name="mlp_ln",
    )(xb, x, w_up_b, w_down_b, g, b)


def _pick(n, pref):
    while n % pref:
        pref //= 2
    return pref


def kernel(x, w_in, lambda_q1, lambda_k1, lambda_q2, lambda_k2, attn_norm_g, ssm_lambda_re, ssm_lambda_im, ssm_log_dt, ssm_b_re, ssm_b_im, ssm_c_re, ssm_c_im, ssm_d, glu_w, glu_b, ssm_norm_g, w_out, ln1_g, ln1_b, w_up, w_down, ln2_g, ln2_b):
    bsz, seq, d = x.shape
    depth = w_in.shape[0]
    attn_w = attn_norm_g.shape[1]
    ssm_w = ssm_d.shape[1]
    n_groups = ssm_w // SSM_GROUP_CH
    assert bsz == 1 and seq % CHUNK == 0
    n_chunks = seq // CHUNK
    p = SSM_STATE
    c = SSM_GROUP_CH

    w_in_b = w_in.astype(BF16)
    w_vu_t = w_in_b[:, :, 2 * attn_w:].transpose(0, 2, 1)
    w_out_b = w_out.astype(BF16)
    w_up_b = w_up.astype(BF16)
    w_down_b = w_down.astype(BF16)
    glu_w_t = glu_w.transpose(0, 2, 1).astype(BF16)
    lq1 = lambda_q1.reshape(depth, 1, HEAD_DIM)
    lk1 = lambda_k1.reshape(depth, 1, HEAD_DIM)
    lq2 = lambda_q2.reshape(depth, 1, HEAD_DIM)
    lk2 = lambda_k2.reshape(depth, 1, HEAD_DIM)
    attn_g_col = attn_norm_g.reshape(depth, attn_w, 1)
    ldt = ssm_log_dt.reshape(depth, n_groups, 1, 1)
    lam_re = ssm_lambda_re.reshape(depth, n_groups, 1, p)
    lam_im = ssm_lambda_im.reshape(depth, n_groups, 1, p)
    bt_re = ssm_b_re.transpose(0, 1, 3, 2)
    bt_im = ssm_b_im.transpose(0, 1, 3, 2)
    ct_re = ssm_c_re.transpose(0, 1, 3, 2)
    ct_im = ssm_c_im.transpose(0, 1, 3, 2)
    d_skip = ssm_d.reshape(depth, n_groups, 1, c)
    glu_b_col = glu_b.reshape(depth, ssm_w, 1)
    ssm_g_col = ssm_norm_g.reshape(depth, ssm_w, 1)
    ln1g, ln1b = ln1_g.reshape(depth, 1, d), ln1_b.reshape(depth, 1, d)
    ln2g, ln2b = ln2_g.reshape(depth, 1, d), ln2_b.reshape(depth, 1, d)

    blk = _pick(seq, 512)
    xf = x.reshape(seq, d)
    xb = xf.astype(BF16)
    for l in range(depth):
        lam_init = 0.8 - 0.6 * math.exp(-0.3 * l)
        qk = _proj_qk(xb, w_in_b, l, 2 * attn_w, attn_w, _pick(seq, 1024), 1024)
        v_t, u_t = _proj_vu(w_vu_t, xb, l, attn_w, _pick(seq, 512))

        attn_pieces = _diff_attention(qk, v_t, lq1, lk1, lq2, lk2, attn_g_col, l, lam_init, blk)

        y_t = _ssm_core(u_t, ldt, lam_re, lam_im, bt_re, bt_im, ct_re, ct_im, d_skip, l)
        ssm_t = _ssm_post(y_t, glu_w_t, glu_b_col, ssm_g_col, l, _pick(seq, 512))

        xf, xb = _out_proj_ln(attn_pieces, ssm_t, w_out_b, xf, ln1g, ln1b, l, _pick(seq, 512))
        xf, xb = _mlp_ln(xb, xf, w_up_b, w_down_b, ln2g, ln2b, l, _pick(seq, 512), 1024)
    return xf.reshape(bsz, seq, d)
```

```python
import functools
import math

import jax
import jax.numpy as jnp
from jax import lax
from jax.experimental import pallas as pl
from jax.experimental.pallas import tpu as pltpu

F32 = jnp.float32
BF16 = jnp.bfloat16

DEPTH = 4
HEAD_DIM = 64
HEAD_W = 2 * HEAD_DIM
CHUNK = 64
SSM_GROUP_CH = 16
SSM_STATE = 64
DEEPNORM_ALPHA = (2.0 * DEPTH) ** 0.25
NORM_EPS = 1e-5
MASK_VALUE = -1e30
QK_SCALE_LOG2E = HEAD_DIM ** -0.5 * math.log2(math.e)
ONES_ROWS = 16
UNROLL = 8
NQ = 4
LANES = 128
VMEM_LIMIT = 56 * 1024 * 1024


def _cparams(*sem):
    return pltpu.CompilerParams(dimension_semantics=sem, vmem_limit_bytes=VMEM_LIMIT)


def _mm_qk_kernel(a_ref, b_ref, o_ref, *, q_blocks):
    acc = jnp.dot(a_ref[...], b_ref[...], preferred_element_type=F32)
    scale = jnp.where(pl.program_id(1) < q_blocks, QK_SCALE_LOG2E, 1.0)
    o_ref[...] = (acc * scale).astype(o_ref.dtype)


def _proj_qk(xb, w_in_b, layer, n_cols, q_cols, tm, tn):
    m, k = xb.shape
    return pl.pallas_call(
        functools.partial(_mm_qk_kernel, q_blocks=q_cols // tn),
        grid=(m // tm, n_cols // tn),
        in_specs=[pl.BlockSpec((tm, k), lambda i, j: (i, 0)),
                  pl.BlockSpec((None, k, tn), lambda i, j: (layer, 0, j))],
        out_specs=pl.BlockSpec((tm, tn), lambda i, j: (i, j)),
        out_shape=jax.ShapeDtypeStruct((m, n_cols), BF16),
        compiler_params=_cparams("parallel", "arbitrary"),
        name="proj_qk",
    )(xb, w_in_b)


def _mm_vu_kernel(w_ref, x_ref, v_ref, u_ref):
    nt_dims = (((1,), (1,)), ((), ()))
    rows_v = v_ref.shape[0]
    x = x_ref[...]
    v_ref[...] = lax.dot_general(w_ref[:rows_v, :], x, nt_dims,
                                 preferred_element_type=F32).astype(v_ref.dtype)
    u_ref[...] = lax.dot_general(w_ref[rows_v:, :], x, nt_dims,
                                 preferred_element_type=F32).astype(u_ref.dtype)


def _proj_vu(w_vu_t, xb, layer, rows_v, tl):
    _, r, k = w_vu_t.shape
    l = xb.shape[0]
    rows_u = r - rows_v
    return pl.pallas_call(
        _mm_vu_kernel,
        grid=(l // tl,),
        in_specs=[pl.BlockSpec((None, r, k), lambda i: (layer, 0, 0)),
                  pl.BlockSpec((tl, k), lambda i: (i, 0))],
        out_specs=[pl.BlockSpec((rows_v, tl), lambda i: (0, i)),
                   pl.BlockSpec((rows_u, tl), lambda i: (0, i))],
        out_shape=[jax.ShapeDtypeStruct((rows_v, l), BF16),
                   jax.ShapeDtypeStruct((rows_u, l), BF16)],
        compiler_params=_cparams("parallel"),
        name="proj_vu",
    )(w_vu_t, xb)


def _query_blocks(i, n_qblk):
    quarter = n_qblk // NQ
    return (i, 2 * quarter - 1 - i, 2 * quarter + i, n_qblk - 1 - i)


def _attn_kernel(lq1_ref, lk1_ref, lq2_ref, lk2_ref, q0_ref, q1_ref, q2_ref, q3_ref,
                 k_ref, vt_ref, g_ref, o0_ref, o1_ref, o2_ref, o3_ref,
                 qz_ref, s_ref, mb_ref, m_ref, acc_ref, *, blk, n_qblk, lam_init):
    i = pl.program_id(1)
    q_blocks = _query_blocks(i, n_qblk)
    starts = (0, q_blocks[0], q_blocks[0] + q_blocks[1], q_blocks[0] + q_blocks[1] + q_blocks[2])
    n_items = NQ + 2 * n_qblk - 2
    nt_dims = (((1,), (1,)), ((), ()))

    lane = lax.broadcasted_iota(jnp.int32, (blk, HEAD_W), 1)
    for sel, q_ref in enumerate((q0_ref, q1_ref, q2_ref, q3_ref)):
        q = q_ref[...]
        zero = jnp.zeros_like(q)
        qz_ref[sel, 0] = jnp.where(lane < HEAD_DIM, q, zero)
        qz_ref[sel, 1] = jnp.where(lane >= HEAD_DIM, q, zero)
    ones = jnp.ones((ONES_ROWS, blk), BF16)

    def stage_a(block, sel, slot, diagonal):
        st = pl.multiple_of(block * blk, blk)
        kb = k_ref[pl.ds(st, blk), :]
        for mp in range(2):
            s = lax.dot_general(kb, qz_ref[sel, mp], nt_dims, preferred_element_type=F32)
            if diagonal:
                key_chunk = lax.broadcasted_iota(jnp.int32, (blk, blk), 0) // CHUNK
                qry_chunk = lax.broadcasted_iota(jnp.int32, (blk, blk), 1) // CHUNK
                s = jnp.where(key_chunk <= qry_chunk, s, MASK_VALUE)
            s_ref[slot, mp] = s
            mb_ref[slot, mp] = jnp.max(s, axis=0, keepdims=True)

    def stage_b(block, sel, slot, first):
        st = pl.multiple_of(block * blk, blk)
        v_ext = jnp.concatenate([vt_ref[:, pl.ds(st, blk)], ones], axis=0)
        for mp in range(2):
            if first:
                m_new = mb_ref[slot, mp]
            else:
                m_old = m_ref[sel, mp]
                m_new = jnp.maximum(m_old, mb_ref[slot, mp])
            p = jnp.exp2(s_ref[slot, mp] - m_new).astype(BF16)
            pv = jnp.dot(v_ext, p, preferred_element_type=F32)
            if first:
                acc_ref[sel, mp] = pv
            else:
                acc_ref[sel, mp] = jnp.exp2(m_old - m_new) * acc_ref[sel, mp] + pv
            m_ref[sel, mp] = m_new

    def item(k):
        if isinstance(k, int) and k < NQ:
            return q_blocks[k], k, True
        idx = k - NQ
        sel = sum((idx >= st).astype(jnp.int32) for st in starts[1:])
        first = jnp.where(sel == 0, starts[0], jnp.where(sel == 1, starts[1],
                          jnp.where(sel == 2, starts[2], starts[3])))
        return idx - first, sel, False

    def run_items(base, count):
        for q in range(count):
            nb, ns, nd = item(base + q + 1)
            stage_a(nb, ns, q % 2, nd)
            cb, cs, cd = item(base + q)
            stage_b(cb, cs, (q + 1) % 2, cd)

    stage_a(q_blocks[0], 0, 0, True)
    nb, ns, nd = item(1)
    stage_a(nb, ns, 1, nd)
    stage_b(q_blocks[0], 0, 0, True)
    run_items(1, UNROLL)

    def trip(t, carry):
        run_items(1 + UNROLL * t, UNROLL)
        return carry

    lax.fori_loop(1, (n_items - 2) // UNROLL, trip, 0)
    b_last, s_last, _ = item(n_items - 1)
    stage_b(b_last, s_last, (n_items - 1) % 2, False)

    lam = (jnp.exp(jnp.sum(lq1_ref[...] * lk1_ref[...], axis=1, keepdims=True))
           - jnp.exp(jnp.sum(lq2_ref[...] * lk2_ref[...], axis=1, keepdims=True))
           + lam_init)
    for sel, o_ref in enumerate((o0_ref, o1_ref, o2_ref, o3_ref)):
        o = (acc_ref[sel, 0, :HEAD_W, :] / acc_ref[sel, 0, HEAD_W:HEAD_W + 1, :]
             - lam * (acc_ref[sel, 1, :HEAD_W, :] / acc_ref[sel, 1, HEAD_W:HEAD_W + 1, :]))
        ms = jnp.mean(o * o, axis=0, keepdims=True)
        o = o * lax.rsqrt(ms + NORM_EPS) * g_ref[...] * (1.0 - lam_init)
        o_ref[...] = o.astype(o_ref.dtype)


def _diff_attention(qk, v_t, lq1, lk1, lq2, lk2, g_col, layer, lam_init, blk):
    l = qk.shape[0]
    a = v_t.shape[0]
    n_heads = a // HEAD_W
    n_qblk = l // blk
    assert NQ == 4 and n_qblk % NQ == 0 and (2 * n_qblk) % UNROLL == 0
    quarter = n_qblk // NQ
    lam_spec = pl.BlockSpec((None, 1, HEAD_DIM), lambda h, i: (layer, 0, 0))
    out_sds = jax.ShapeDtypeStruct((a, l // NQ), BF16)

    def q_spec(sel):
        return pl.BlockSpec((blk, HEAD_W), lambda h, i: (_query_blocks(i, n_qblk)[sel], h))

    def o_spec(sel):
        return pl.BlockSpec((HEAD_W, blk),
                            lambda h, i: (h, _query_blocks(i, n_qblk)[sel] - sel * quarter))

    return pl.pallas_call(
        functools.partial(_attn_kernel, blk=blk, n_qblk=n_qblk, lam_init=lam_init),
        grid=(n_heads, quarter),
        in_specs=[lam_spec, lam_spec, lam_spec, lam_spec] + [q_spec(sel) for sel in range(NQ)] + [
            pl.BlockSpec((l, HEAD_W), lambda h, i: (0, n_heads + h)),
            pl.BlockSpec((HEAD_W, l), lambda h, i: (h, 0)),
            pl.BlockSpec((None, HEAD_W, 1), lambda h, i: (layer, h, 0))],
        out_specs=[o_spec(sel) for sel in range(NQ)],
        out_shape=[out_sds] * NQ,
        scratch_shapes=[pltpu.VMEM((NQ, 2, blk, HEAD_W), BF16),
                        pltpu.VMEM((2, 2, blk, blk), F32),
                        pltpu.VMEM((2, 2, 1, blk), F32),
                        pltpu.VMEM((NQ, 2, 1, blk), F32),
                        pltpu.VMEM((NQ, 2, HEAD_W + ONES_ROWS, blk), F32)],
        compiler_params=_cparams("parallel", "arbitrary"),
        name="diff_attn",
    )(lq1, lk1, lq2, lk2, qk, qk, qk, qk, qk, v_t, g_col)


def _complex_pow(ar, ai, e, n_bits):
    shape = jnp.broadcast_shapes(ar.shape, e.shape)
    pr = jnp.ones(shape, F32)
    pi = jnp.zeros(shape, F32)
    fr, fi = ar, ai
    for b in range(n_bits):
        bit = ((e >> b) & 1) == 1
        nr = pr * fr - pi * fi
        ni = pr * fi + pi * fr
        pr = jnp.where(bit, nr, pr)
        pi = jnp.where(bit, ni, pi)
        if b + 1 < n_bits:
            fr, fi = fr * fr - fi * fi, 2.0 * fr * fi
    return pr, pi


def _gelu_tanh(y):
    k0 = math.sqrt(2.0 / math.pi)
    return 0.5 * y * (1.0 + jnp.tanh(k0 * (y + 0.044715 * (y * y * y))))


def _spread_channels(x, t):
    n_ch = x.shape[1]
    expand = (lax.broadcasted_iota(jnp.int32, (n_ch, n_ch * t), 1) // t
              == lax.broadcasted_iota(jnp.int32, (n_ch, n_ch * t), 0)).astype(BF16)
    out = None
    rest = x
    for _ in range(3):
        piece = rest.astype(BF16)
        rest = rest - piece.astype(F32)
        part = jnp.dot(piece, expand, preferred_element_type=F32)
        out = part if out is None else out + part
    return out


def _dot_split(a, b):
    a_hi = a.astype(BF16)
    b_hi = b.astype(BF16)
    a_lo = (a - a_hi.astype(F32)).astype(BF16)
    b_lo = (b - b_hi.astype(F32)).astype(BF16)
    small = (jnp.dot(a_hi, b_lo, preferred_element_type=F32)
             + jnp.dot(a_lo, b_hi, preferred_element_type=F32))
    return jnp.dot(a_hi, b_hi, preferred_element_type=F32) + small


def _row_to_col(row):
    n = row.shape[1]
    eye = (lax.broadcasted_iota(jnp.int32, (n, n), 0)
           == lax.broadcasted_iota(jnp.int32, (n, n), 1))
    return jnp.sum(jnp.where(eye, jnp.broadcast_to(row, (n, n)), 0.0), axis=1, keepdims=True)


def _ssm_build(prm, m_ref, w_re_ref, w_im_ref, v_ref, at_ref, drep_ref):
    ldt_ref, lr_ref, li_ref, bt_re_ref, bt_im_ref, ct_re_ref, ct_im_ref, d_ref = prm
    t = CHUNK
    n_ch = SSM_GROUP_CH
    w = n_ch * t
    n_bits = t.bit_length() - 1
    dt = jnp.exp(ldt_ref[...])

    lr = lr_ref[...]
    li = li_ref[...]
    mag = jnp.exp(lr * dt)
    ar = mag * jnp.cos(li * dt)
    ai = mag * jnp.sin(li * dt)
    den = lr * lr + li * li
    nr = ar - 1.0
    fr = (nr * lr + ai * li) / den
    fi = (ai * lr - nr * li) / den
    bbt_r = fr * bt_re_ref[...] - fi * bt_im_ref[...]
    bbt_i = fr * bt_im_ref[...] + fi * bt_re_ref[...]

    rev = (t - 1) - lax.broadcasted_iota(jnp.int32, (t, 1), 0)
    pr, pi = _complex_pow(ar, ai, rev, n_bits)
    for c in range(n_ch):
        br = bbt_r[c:c + 1, :]
        bi = bbt_i[c:c + 1, :]
        w_re_ref[c * t:(c + 1) * t, :] = (pr * br - pi * bi).astype(BF16)
        w_im_ref[c * t:(c + 1) * t, :] = (pr * bi + pi * br).astype(BF16)
    at_r, at_i = _complex_pow(ar, ai, jnp.full((1, 1), t, jnp.int32), n_bits + 1)
    at_ref[0] = at_r
    at_ref[1] = at_i

    ar_c = _row_to_col(ar)
    ai_c = _row_to_col(ai)
    tau_tile = lax.broadcasted_iota(jnp.int32, (1, LANES), 1) % t
    qr, qi = _complex_pow(ar_c, ai_c, tau_tile, n_bits)
    qr = jnp.concatenate([qr] * (w // LANES), axis=1)
    qi = jnp.concatenate([qi] * (w // LANES), axis=1)
    c_re = _spread_channels(ct_re_ref[...], t)
    c_im = _spread_channels(ct_im_ref[...], t)
    drep_ref[...] = _spread_channels(jnp.broadcast_to(d_ref[...], (8, n_ch)), t)[0:1, :]
    ca_r = c_re * qr - c_im * qi
    ca_i = c_re * qi + c_im * qr
    z = _dot_split(jnp.concatenate([bbt_r, -bbt_i], axis=1),
                   jnp.concatenate([ca_r, ca_i], axis=0))
    v_ref[0] = (ca_r * ar_c - ca_i * ai_c).astype(BF16)
    v_ref[1] = (ca_r * ai_c + ca_i * ar_c).astype(BF16)

    tau = lax.broadcasted_iota(jnp.int32, (1, w), 1) % t
    s_row = lax.broadcasted_iota(jnp.int32, (t, 1), 0)
    causal = tau >= s_row
    for c in range(n_ch):
        strip = jnp.broadcast_to(z[c:c + 1, :], (t, w))
        shifted = pltpu.roll(strip, 0, 1, stride=1, stride_axis=0)
        m_ref[c * t:(c + 1) * t, :] = jnp.where(causal, shifted, 0.0).astype(BF16)


def _ssm_fold(u_ref, fold_ref, ub_ref, n_chunks):
    t = CHUNK
    n_ch = SSM_GROUP_CH
    pairs = n_chunks // 2
    half_lane = lax.broadcasted_iota(jnp.int32, (pairs, LANES), 1) < t
    for j in range(pairs):
        fold_ref[j * n_ch:(j + 1) * n_ch, :] = u_ref[:, j * LANES:(j + 1) * LANES].astype(F32)
    for a in range(n_ch // 2):
        lo = fold_ref[pl.ds(2 * a, pairs, stride=n_ch), :]
        hi = fold_ref[pl.ds(2 * a + 1, pairs, stride=n_ch), :]
        even = jnp.where(half_lane, lo, pltpu.roll(hi, t, 1))
        odd = jnp.where(half_lane, pltpu.roll(lo, t, 1), hi)
        ub_ref[0:pairs, a * LANES:(a + 1) * LANES] = even.astype(BF16)
        ub_ref[pairs:n_chunks, a * LANES:(a + 1) * LANES] = odd.astype(BF16)


def _ssm_unfold(y, fold_ref, y_ref, n_chunks):
    t = CHUNK
    n_ch = SSM_GROUP_CH
    pairs = n_chunks // 2
    half_lane = lax.broadcasted_iota(jnp.int32, (pairs, LANES), 1) < t
    for a in range(n_ch // 2):
        even = y[0:pairs, a * LANES:(a + 1) * LANES]
        odd = y[pairs:n_chunks, a * LANES:(a + 1) * LANES]
        fold_ref[pl.ds(2 * a, pairs, stride=n_ch), :] = jnp.where(
            half_lane, even, pltpu.roll(odd, t, 1))
        fold_ref[pl.ds(2 * a + 1, pairs, stride=n_ch), :] = jnp.where(
            half_lane, pltpu.roll(even, t, 1), odd)
    for j in range(pairs):
        y_ref[:, j * LANES:(j + 1) * LANES] = fold_ref[j * n_ch:(j + 1) * n_ch, :].astype(y_ref.dtype)


def _ssm_kernel(*refs, n_chunks):
    n_prm = 8
    u_ref = refs[0]
    prm_first = refs[1:1 + n_prm]
    prm_b = refs[1 + n_prm:1 + 2 * n_prm]
    prm_next = refs[1 + 2 * n_prm:1 + 3 * n_prm]
    y_ref = refs[1 + 3 * n_prm]
    (m_ref, w_re_ref, w_im_ref, v_ref, at_ref, drep_ref,
     s_re_ref, s_im_ref, h_re_ref, h_im_ref, fold_ref, ub_ref, y_scr) = refs[2 + 3 * n_prm:]
    n_ch = SSM_GROUP_CH
    pairs = n_chunks // 2

    def build(prm, slot):
        _ssm_build(prm, m_ref.at[slot], w_re_ref.at[slot], w_im_ref.at[slot], v_ref.at[slot],
                   at_ref.at[slot], drep_ref.at[slot])

    def data(slot, build_next):
        rows = slice(slot * n_ch, (slot + 1) * n_ch)
        _ssm_fold(u_ref.at[rows, :], fold_ref, ub_ref, n_chunks)
        build_next()
        ub = ub_ref[...]
        y_scr[...] = jnp.dot(ub, m_ref[slot], preferred_element_type=F32)
        s_re_ref[...] = jnp.dot(ub, w_re_ref[slot], preferred_element_type=F32)
        s_im_ref[...] = jnp.dot(ub, w_im_ref[slot], preferred_element_type=F32)
        at_r = at_ref[slot, 0]
        at_i = at_ref[slot, 1]

        def step(j, carry):
            hr, hi = carry
            for r in (j, pairs + j):
                h_re_ref[pl.ds(r, 1), :] = hr
                h_im_ref[pl.ds(r, 1), :] = hi
                sr = s_re_ref[pl.ds(r, 1), :]
                si = s_im_ref[pl.ds(r, 1), :]
                hr, hi = at_r * hr - at_i * hi + sr, at_r * hi + at_i * hr + si
            return hr, hi

        zero = jnp.zeros((1, SSM_STATE), F32)
        lax.fori_loop(0, pairs, step, (zero, zero))

        y = y_scr[...] + jnp.dot(h_re_ref[...].astype(BF16), v_ref[slot, 0],
                                 preferred_element_type=F32)
        y = y - jnp.dot(h_im_ref[...].astype(BF16), v_ref[slot, 1], preferred_element_type=F32)
        y = _gelu_tanh(y + drep_ref[slot] * ub_ref[...].astype(F32))
        _ssm_unfold(y, fold_ref, y_ref.at[rows, :], n_chunks)

    @pl.when(pl.program_id(0) == 0)
    def _():
        build(prm_first, 0)

    data(0, lambda: build(prm_b, 1))
    data(1, lambda: build(prm_next, 0))


def _ssm_core(u_t, params, layer):
    width, l = u_t.shape
    p = SSM_STATE
    c = SSM_GROUP_CH
    g = width // c
    assert 2 * CHUNK == LANES and l % (16 * LANES) == 0
    assert g % 2 == 0
    n_chunks = l // CHUNK
    w = c * CHUNK
    shapes = [(1, 1), (1, p), (1, p), (c, p), (c, p), (p, c), (p, c), (1, c)]

    def pspecs(group_of_step):
        return [pl.BlockSpec((None, None) + shp, lambda k: (layer, group_of_step(k), 0, 0))
                for shp in shapes]

    data = pl.BlockSpec((2 * c, l), lambda k: (k, 0))
    return pl.pallas_call(
        functools.partial(_ssm_kernel, n_chunks=n_chunks),
        grid=(g // 2,),
        in_specs=([data] + pspecs(lambda k: 0) + pspecs(lambda k: 2 * k + 1)
                  + pspecs(lambda k: jnp.minimum(2 * k + 2, g - 1))),
        out_specs=data,
        out_shape=jax.ShapeDtypeStruct((width, l), BF16),
        scratch_shapes=[pltpu.VMEM((2, w, w), BF16),
                        pltpu.VMEM((2, w, p), BF16), pltpu.VMEM((2, w, p), BF16),
                        pltpu.VMEM((2, 2, p, w), BF16),
                        pltpu.VMEM((2, 2, 1, p), F32),
                        pltpu.VMEM((2, 1, w), F32),
                        pltpu.VMEM((n_chunks, p), F32), pltpu.VMEM((n_chunks, p), F32),
                        pltpu.VMEM((n_chunks, p), F32), pltpu.VMEM((n_chunks, p), F32),
                        pltpu.VMEM((l // LANES * c, LANES), F32),
                        pltpu.VMEM((n_chunks, w), BF16),
                        pltpu.VMEM((n_chunks, w), F32)],
        compiler_params=_cparams("arbitrary"),
        name="ssm_core",
    )(u_t, *params, *params, *params)


def _ssm_post_kernel(y_ref, w_ref, b_ref, g_ref, o_ref):
    yb = y_ref[...]
    y = yb.astype(F32)
    z = jnp.dot(w_ref[...], yb, preferred_element_type=F32) + b_ref[...]
    o = y * (1.0 / (1.0 + jnp.exp(-z)))
    w, tl = o.shape
    o3 = o.reshape(w // SSM_GROUP_CH, SSM_GROUP_CH, tl)
    ms = jnp.mean(o3 * o3, axis=1, keepdims=True)
    o3 = o3 * lax.rsqrt(ms + NORM_EPS)
    o_ref[...] = (o3.reshape(w, tl) * g_ref[...]).astype(o_ref.dtype)


def _ssm_post(y_t, glu_w_t, glu_b_col, g_col, layer, tl):
    w, l = y_t.shape
    col = pl.BlockSpec((None, w, 1), lambda i: (layer, 0, 0))
    return pl.pallas_call(
        _ssm_post_kernel,
        grid=(l // tl,),
        in_specs=[pl.BlockSpec((w, tl), lambda i: (0, i)),
                  pl.BlockSpec((None, w, w), lambda i: (layer, 0, 0)), col, col],
        out_specs=pl.BlockSpec((w, tl), lambda i: (0, i)),
        out_shape=jax.ShapeDtypeStruct((w, l), BF16),
        compiler_params=_cparams("parallel"),
        name="ssm_post",
    )(y_t, glu_w_t, glu_b_col, g_col)


def _layer_norm(y, g, b):
    mu = jnp.mean(y, axis=-1, keepdims=True)
    yc = y - mu
    var = jnp.mean(yc * yc, axis=-1, keepdims=True)
    return yc * lax.rsqrt(var + NORM_EPS) * g + b


def _out_proj_kernel(a0_ref, a1_ref, a2_ref, a3_ref, s_ref, wa_ref, ws_ref, x_ref, g_ref, b_ref,
                     o_ref, ob_ref):
    tn_dims = (((0,), (0,)), ((), ()))
    piece = pl.program_id(0) // (pl.num_programs(0) // NQ)
    attn = jnp.where(piece == 0, a0_ref[...], jnp.where(
        piece == 1, a1_ref[...], jnp.where(piece == 2, a2_ref[...], a3_ref[...])))
    mix = lax.dot_general(attn, wa_ref[...], tn_dims, preferred_element_type=F32)
    mix = mix + lax.dot_general(s_ref[...], ws_ref[...], tn_dims, preferred_element_type=F32)
    y = _layer_norm(DEEPNORM_ALPHA * x_ref[...] + mix, g_ref[...], b_ref[...])
    o_ref[...] = y
    ob_ref[...] = y.astype(BF16)


def _out_proj_ln(attn_pieces, ssm_t, w_out_b, x, g, b, layer, tm):
    l, d = x.shape
    a = attn_pieces[0].shape[0]
    s = ssm_t.shape[0]
    assert a == s
    per = l // tm // NQ

    def piece_spec(c):
        return pl.BlockSpec((a, tm), lambda i: (0, jnp.clip(i - c * per, 0, per - 1)))

    row = pl.BlockSpec((None, 1, d), lambda i: (layer, 0, 0))
    return pl.pallas_call(
        _out_proj_kernel,
        grid=(l // tm,),
        in_specs=[piece_spec(c) for c in range(NQ)] + [
                  pl.BlockSpec((s, tm), lambda i: (0, i)),
                  pl.BlockSpec((None, a, d), lambda i: (layer, 0, 0)),
                  pl.BlockSpec((None, s, d), lambda i: (layer, 1, 0)),
                  pl.BlockSpec((tm, d), lambda i: (i, 0)), row, row],
        out_specs=[pl.BlockSpec((tm, d), lambda i: (i, 0)),
                   pl.BlockSpec((tm, d), lambda i: (i, 0))],
        out_shape=[jax.ShapeDtypeStruct((l, d), F32), jax.ShapeDtypeStruct((l, d), BF16)],
        compiler_params=_cparams("parallel"),
        name="out_proj_ln",
    )(*attn_pieces, ssm_t, w_out_b, w_out_b, x, g, b)


def _mlp_kernel(xb_ref, x_ref, wu_ref, wd_ref, g_ref, b_ref, o_ref, ob_ref, acc_ref):
    j = pl.program_id(1)

    @pl.when(j == 0)
    def _():
        acc_ref[...] = jnp.zeros(acc_ref.shape, F32)

    h = jnp.dot(xb_ref[...], wu_ref[...], preferred_element_type=F32)
    h = jnp.maximum(h, 0.0)
    h = (h * h).astype(BF16)
    acc_ref[...] += jnp.dot(h, wd_ref[...], preferred_element_type=F32)

    @pl.when(j == pl.num_programs(1) - 1)
    def _():
        y = _layer_norm(DEEPNORM_ALPHA * x_ref[...] + acc_ref[...], g_ref[...], b_ref[...])
        o_ref[...] = y
        ob_ref[...] = y.astype(BF16)


def _mlp_ln(xb, x, w_up_b, w_down_b, g, b, layer, tm, tf):
    l, d = x.shape
    f = w_up_b.shape[2]
    row = pl.BlockSpec((None, 1, d), lambda i, j: (layer, 0, 0))
    return pl.pallas_call(
        _mlp_kernel,
        grid=(l // tm, f // tf),
        in_specs=[pl.BlockSpec((tm, d), lambda i, j: (i, 0)),
                  pl.BlockSpec((tm, d), lambda i, j: (i, 0)),
                  pl.BlockSpec((None, d, tf), lambda i, j: (layer, 0, j)),
                  pl.BlockSpec((None, tf, d), lambda i, j: (layer, j, 0)), row, row],
        out_specs=[pl.BlockSpec((tm, d), lambda i, j: (i, 0)),
                   pl.BlockSpec((tm, d), lambda i, j: (i, 0))],
        out_shape=[jax.ShapeDtypeStruct((l, d), F32), jax.ShapeDtypeStruct((l, d), BF16)],
        scratch_shapes=[pltpu.VMEM((tm, d), F32)],
        compiler_params=_cparams("parallel", "arbitrary"),
        name="mlp_ln",
    )(xb, x, w_up_b, w_down_b, g, b)


def _pick(n, pref):
    while n % pref:
        pref //= 2
    return pref


def kernel(x, w_in, lambda_q1, lambda_k1, lambda_q2, lambda_k2, attn_norm_g, ssm_lambda_re, ssm_lambda_im, ssm_log_dt, ssm_b_re, ssm_b_im, ssm_c_re, ssm_c_im, ssm_d, glu_w, glu_b, ssm_norm_g, w_out, ln1_g, ln1_b, w_up, w_down, ln2_g, ln2_b):
    bsz, seq, d = x.shape
    depth = w_in.shape[0]
    attn_w = attn_norm_g.shape[1]
    ssm_w = ssm_d.shape[1]
    n_groups = ssm_w // SSM_GROUP_CH
    assert bsz == 1 and seq % CHUNK == 0
    n_chunks = seq // CHUNK
    p = SSM_STATE
    c = SSM_GROUP_CH

    w_in_b = w_in.astype(BF16)
    w_vu_t = w_in_b[:, :, 2 * attn_w:].transpose(0, 2, 1)
    w_out_b = w_out.astype(BF16)
    w_up_b = w_up.astype(BF16)
    w_down_b = w_down.astype(BF16)
    glu_w_t = glu_w.transpose(0, 2, 1).astype(BF16)
    lq1 = lambda_q1.reshape(depth, 1, HEAD_DIM)
    lk1 = lambda_k1.reshape(depth, 1, HEAD_DIM)
    lq2 = lambda_q2.reshape(depth, 1, HEAD_DIM)
    lk2 = lambda_k2.reshape(depth, 1, HEAD_DIM)
    attn_g_col = attn_norm_g.reshape(depth, attn_w, 1)
    ldt = ssm_log_dt.reshape(depth, n_groups, 1, 1)
    lam_re = ssm_lambda_re.reshape(depth, n_groups, 1, p)
    lam_im = ssm_lambda_im.reshape(depth, n_groups, 1, p)
    bt_re = ssm_b_re.transpose(0, 1, 3, 2)
    bt_im = ssm_b_im.transpose(0, 1, 3, 2)
    ct_re = ssm_c_re.transpose(0, 1, 3, 2)
    ct_im = ssm_c_im.transpose(0, 1, 3, 2)
    d_skip = ssm_d.reshape(depth, n_groups, 1, c)
    glu_b_col = glu_b.reshape(depth, ssm_w, 1)
    ssm_g_col = ssm_norm_g.reshape(depth, ssm_w, 1)
    ln1g, ln1b = ln1_g.reshape(depth, 1, d), ln1_b.reshape(depth, 1, d)
    ln2g, ln2b = ln2_g.reshape(depth, 1, d), ln2_b.reshape(depth, 1, d)

    blk = _pick(seq, 512)
    xf = x.reshape(seq, d)
    xb = xf.astype(BF16)
    for l in range(depth):
        lam_init = 0.8 - 0.6 * math.exp(-0.3 * l)
        qk = _proj_qk(xb, w_in_b, l, 2 * attn_w, attn_w, _pick(seq, 1024), 1024)
        v_t, u_t = _proj_vu(w_vu_t, xb, l, attn_w, _pick(seq, 512))

        attn_pieces = _diff_attention(qk, v_t, lq1, lk1, lq2, lk2, attn_g_col, l, lam_init, blk)

        y_t = _ssm_core(u_t, (ldt, lam_re, lam_im, bt_re, bt_im, ct_re, ct_im, d_skip), l)
        ssm_t = _ssm_post(y_t, glu_w_t, glu_b_col, ssm_g_col, l, _pick(seq, 512))

        xf, xb = _out_proj_ln(attn_pieces, ssm_t, w_out_b, xf, ln1g, ln1b, l, _pick(seq, 512))
        xf, xb = _mlp_ln(xb, xf, w_up_b, w_down_b, ln2g, ln2b, l, _pick(seq, 512), 1024)
    return xf.reshape(bsz, seq, d)
```

```python
import functools
import math

import jax
import jax.numpy as jnp
from jax import lax
from jax.experimental import pallas as pl
from jax.experimental.pallas import tpu as pltpu

F32 = jnp.float32
BF16 = jnp.bfloat16

DEPTH = 4
HEAD_DIM = 64
HEAD_W = 2 * HEAD_DIM
CHUNK = 64
SSM_GROUP_CH = 16
SSM_STATE = 64
DEEPNORM_ALPHA = (2.0 * DEPTH) ** 0.25
NORM_EPS = 1e-5
MASK_VALUE = -1e30
QK_SCALE_LOG2E = HEAD_DIM ** -0.5 * math.log2(math.e)
ONES_ROWS = 16
UNROLL = 10
NQ = 8
ATTN_BLOCK = 512
LANES = 128
VMEM_LIMIT = 56 * 1024 * 1024


def _cparams(*sem):
    return pltpu.CompilerParams(dimension_semantics=sem, vmem_limit_bytes=VMEM_LIMIT)


def _mm_qk_kernel(a_ref, b_ref, o_ref, *, q_blocks):
    acc = jnp.dot(a_ref[...], b_ref[...], preferred_element_type=F32)
    scale = jnp.where(pl.program_id(1) < q_blocks, QK_SCALE_LOG2E, 1.0)
    o_ref[...] = (acc * scale).astype(o_ref.dtype)


def _proj_qk(xb, w_in_b, layer, n_cols, q_cols, tm, tn):
    m, k = xb.shape
    return pl.pallas_call(
        functools.partial(_mm_qk_kernel, q_blocks=q_cols // tn),
        grid=(m // tm, n_cols // tn),
        in_specs=[pl.BlockSpec((tm, k), lambda i, j: (i, 0)),
                  pl.BlockSpec((None, k, tn), lambda i, j: (layer, 0, j))],
        out_specs=pl.BlockSpec((tm, tn), lambda i, j: (i, j)),
        out_shape=jax.ShapeDtypeStruct((m, n_cols), BF16),
        compiler_params=_cparams("parallel", "arbitrary"),
        name="proj_qk",
    )(xb, w_in_b)


def _mm_vu_kernel(w_ref, x_ref, v_ref, u_ref):
    nt_dims = (((1,), (1,)), ((), ()))
    rows_v = v_ref.shape[0]
    x = x_ref[...]
    v_ref[...] = lax.dot_general(w_ref[:rows_v, :], x, nt_dims,
                                 preferred_element_type=F32).astype(v_ref.dtype)
    u_ref[...] = lax.dot_general(w_ref[rows_v:, :], x, nt_dims,
                                 preferred_element_type=F32).astype(u_ref.dtype)


def _proj_vu(w_vu_t, xb, layer, rows_v, tl):
    _, r, k = w_vu_t.shape
    l = xb.shape[0]
    rows_u = r - rows_v
    return pl.pallas_call(
        _mm_vu_kernel,
        grid=(l // tl,),
        in_specs=[pl.BlockSpec((None, r, k), lambda i: (layer, 0, 0)),
                  pl.BlockSpec((tl, k), lambda i: (i, 0))],
        out_specs=[pl.BlockSpec((rows_v, tl), lambda i: (0, i)),
                   pl.BlockSpec((rows_u, tl), lambda i: (0, i))],
        out_shape=[jax.ShapeDtypeStruct((rows_v, l), BF16),
                   jax.ShapeDtypeStruct((rows_u, l), BF16)],
        compiler_params=_cparams("parallel"),
        name="proj_vu",
    )(w_vu_t, xb)


def _query_blocks(i, n_qblk):
    per = n_qblk // NQ
    blocks = []
    for j in range(NQ // 2):
        blocks += [2 * j * per + i, (2 * j + 2) * per - 1 - i]
    return tuple(blocks)


def _attn_kernel(*refs, blk, n_qblk, lam_init):
    lq1_ref, lk1_ref, lq2_ref, lk2_ref = refs[:4]
    q_refs = refs[4:4 + NQ]
    k_ref, vt_ref, g_ref = refs[4 + NQ:7 + NQ]
    o_refs = refs[7 + NQ:7 + 2 * NQ]
    qz_ref, s_ref, mb_ref, m_ref, acc_ref = refs[7 + 2 * NQ:]
    i = pl.program_id(1)
    q_blocks = _query_blocks(i, n_qblk)
    starts = [0]
    for b in q_blocks[:-1]:
        starts.append(starts[-1] + b)
    n_items = NQ + NQ // 2 * (n_qblk - 1)
    nt_dims = (((1,), (1,)), ((), ()))

    lane = lax.broadcasted_iota(jnp.int32, (blk, HEAD_W), 1)
    for sel, q_ref in enumerate(q_refs):
        q = q_ref[...]
        zero = jnp.zeros_like(q)
        qz_ref[sel, 0] = jnp.where(lane < HEAD_DIM, q, zero)
        qz_ref[sel, 1] = jnp.where(lane >= HEAD_DIM, q, zero)
    ones = jnp.ones((ONES_ROWS, blk), BF16)

    def stage_a(block, sel, slot, diagonal):
        st = pl.multiple_of(block * blk, blk)
        kb = k_ref[pl.ds(st, blk), :]
        for mp in range(2):
            s = lax.dot_general(kb, qz_ref[sel, mp], nt_dims, preferred_element_type=F32)
            if diagonal:
                key_chunk = lax.broadcasted_iota(jnp.int32, (blk, blk), 0) // CHUNK
                qry_chunk = lax.broadcasted_iota(jnp.int32, (blk, blk), 1) // CHUNK
                s = jnp.where(key_chunk <= qry_chunk, s, MASK_VALUE)
            s_ref[slot, mp] = s
            mb_ref[slot, mp] = jnp.max(s, axis=0, keepdims=True)

    def stage_b(block, sel, slot, first):
        st = pl.multiple_of(block * blk, blk)
        v_ext = jnp.concatenate([vt_ref[:, pl.ds(st, blk)], ones], axis=0)
        for mp in range(2):
            if first:
                m_new = mb_ref[slot, mp]
            else:
                m_old = m_ref[sel, mp]
                m_new = jnp.maximum(m_old, mb_ref[slot, mp])
            p = jnp.exp2(s_ref[slot, mp] - m_new).astype(BF16)
            pv = jnp.dot(v_ext, p, preferred_element_type=F32)
            if first:
                acc_ref[sel, mp] = pv
            else:
                acc_ref[sel, mp] = jnp.exp2(m_old - m_new) * acc_ref[sel, mp] + pv
            m_ref[sel, mp] = m_new

    def item(k):
        if isinstance(k, int) and k < NQ:
            return q_blocks[k], k, True
        idx = k - NQ
        sel = sum((idx >= st).astype(jnp.int32) for st in starts[1:])
        first = starts[NQ - 1]
        for j in range(NQ - 2, -1, -1):
            first = jnp.where(sel == j, starts[j], first)
        return idx - first, sel, False

    def run_items(base, count):
        for q in range(count):
            nb, ns, nd = item(base + q + 1)
            stage_a(nb, ns, q % 2, nd)
            cb, cs, cd = item(base + q)
            stage_b(cb, cs, (q + 1) % 2, cd)

    stage_a(q_blocks[0], 0, 0, True)
    nb, ns, nd = item(1)
    stage_a(nb, ns, 1, nd)
    stage_b(q_blocks[0], 0, 0, True)
    run_items(1, UNROLL)

    def trip(t, carry):
        run_items(1 + UNROLL * t, UNROLL)
        return carry

    lax.fori_loop(1, (n_items - 2) // UNROLL, trip, 0)
    b_last, s_last, _ = item(n_items - 1)
    stage_b(b_last, s_last, (n_items - 1) % 2, False)

    lam = (jnp.exp(jnp.sum(lq1_ref[...] * lk1_ref[...], axis=1, keepdims=True))
           - jnp.exp(jnp.sum(lq2_ref[...] * lk2_ref[...], axis=1, keepdims=True))
           + lam_init)
    for sel, o_ref in enumerate(o_refs):
        o = (acc_ref[sel, 0, :HEAD_W, :] / acc_ref[sel, 0, HEAD_W:HEAD_W + 1, :]
             - lam * (acc_ref[sel, 1, :HEAD_W, :] / acc_ref[sel, 1, HEAD_W:HEAD_W + 1, :]))
        ms = jnp.mean(o * o, axis=0, keepdims=True)
        o = o * lax.rsqrt(ms + NORM_EPS) * g_ref[...] * (1.0 - lam_init)
        o_ref[...] = o.astype(o_ref.dtype)


def _diff_attention(qk, v_t, lq1, lk1, lq2, lk2, g_col, layer, lam_init, blk):
    l = qk.shape[0]
    a = v_t.shape[0]
    n_heads = a // HEAD_W
    n_qblk = l // blk
    assert NQ % 2 == 0 and UNROLL % 2 == 0 and n_qblk % NQ == 0
    assert UNROLL >= NQ - 1
    assert (NQ + NQ // 2 * (n_qblk - 1) - 2) % UNROLL == 0
    per = n_qblk // NQ
    lam_spec = pl.BlockSpec((None, 1, HEAD_DIM), lambda h, i: (layer, 0, 0))
    out_sds = jax.ShapeDtypeStruct((a, l // NQ), BF16)

    def q_spec(sel):
        return pl.BlockSpec((blk, HEAD_W), lambda h, i: (_query_blocks(i, n_qblk)[sel], h))

    def o_spec(sel):
        return pl.BlockSpec((HEAD_W, blk),
                            lambda h, i: (h, _query_blocks(i, n_qblk)[sel] - sel * per))

    return pl.pallas_call(
        functools.partial(_attn_kernel, blk=blk, n_qblk=n_qblk, lam_init=lam_init),
        grid=(n_heads, per),
        in_specs=[lam_spec, lam_spec, lam_spec, lam_spec] + [q_spec(sel) for sel in range(NQ)] + [
            pl.BlockSpec((l, HEAD_W), lambda h, i: (0, n_heads + h)),
            pl.BlockSpec((HEAD_W, l), lambda h, i: (h, 0)),
            pl.BlockSpec((None, HEAD_W, 1), lambda h, i: (layer, h, 0))],
        out_specs=[o_spec(sel) for sel in range(NQ)],
        out_shape=[out_sds] * NQ,
        scratch_shapes=[pltpu.VMEM((NQ, 2, blk, HEAD_W), BF16),
                        pltpu.VMEM((2, 2, blk, blk), F32),
                        pltpu.VMEM((2, 2, 1, blk), F32),
                        pltpu.VMEM((NQ, 2, 1, blk), F32),
                        pltpu.VMEM((NQ, 2, HEAD_W + ONES_ROWS, blk), F32)],
        compiler_params=_cparams("parallel", "arbitrary"),
        name="diff_attn",
    )(lq1, lk1, lq2, lk2, *([qk] * (NQ + 1)), v_t, g_col)


def _complex_pow(ar, ai, e, n_bits):
    shape = jnp.broadcast_shapes(ar.shape, e.shape)
    pr = jnp.ones(shape, F32)
    pi = jnp.zeros(shape, F32)
    fr, fi = ar, ai
    for b in range(n_bits):
        bit = ((e >> b) & 1) == 1
        nr = pr * fr - pi * fi
        ni = pr * fi + pi * fr
        pr = jnp.where(bit, nr, pr)
        pi = jnp.where(bit, ni, pi)
        if b + 1 < n_bits:
            fr, fi = fr * fr - fi * fi, 2.0 * fr * fi
    return pr, pi


def _gelu_tanh(y):
    k0 = math.sqrt(2.0 / math.pi)
    return 0.5 * y * (1.0 + jnp.tanh(k0 * (y + 0.044715 * (y * y * y))))


def _spread_channels(x, t):
    n_ch = x.shape[1]
    expand = (lax.broadcasted_iota(jnp.int32, (n_ch, n_ch * t), 1) // t
              == lax.broadcasted_iota(jnp.int32, (n_ch, n_ch * t), 0)).astype(BF16)
    out = None
    rest = x
    for _ in range(3):
        piece = rest.astype(BF16)
        rest = rest - piece.astype(F32)
        part = jnp.dot(piece, expand, preferred_element_type=F32)
        out = part if out is None else out + part
    return out


def _dot_split(a, b):
    a_hi = a.astype(BF16)
    b_hi = b.astype(BF16)
    a_lo = (a - a_hi.astype(F32)).astype(BF16)
    b_lo = (b - b_hi.astype(F32)).astype(BF16)
    small = (jnp.dot(a_hi, b_lo, preferred_element_type=F32)
             + jnp.dot(a_lo, b_hi, preferred_element_type=F32))
    return jnp.dot(a_hi, b_hi, preferred_element_type=F32) + small


def _row_to_col(row):
    n = row.shape[1]
    eye = (lax.broadcasted_iota(jnp.int32, (n, n), 0)
           == lax.broadcasted_iota(jnp.int32, (n, n), 1))
    return jnp.sum(jnp.where(eye, jnp.broadcast_to(row, (n, n)), 0.0), axis=1, keepdims=True)


def _ssm_build(prm, m_ref, w_re_ref, w_im_ref, v_ref, at_ref, drep_ref):
    ldt_ref, lr_ref, li_ref, bt_re_ref, bt_im_ref, ct_re_ref, ct_im_ref, d_ref = prm
    t = CHUNK
    n_ch = SSM_GROUP_CH
    w = n_ch * t
    n_bits = t.bit_length() - 1
    dt = jnp.exp(ldt_ref[...])

    lr = lr_ref[...]
    li = li_ref[...]
    mag = jnp.exp(lr * dt)
    ar = mag * jnp.cos(li * dt)
    ai = mag * jnp.sin(li * dt)
    den = lr * lr + li * li
    nr = ar - 1.0
    fr = (nr * lr + ai * li) / den
    fi = (ai * lr - nr * li) / den
    bbt_r = fr * bt_re_ref[...] - fi * bt_im_ref[...]
    bbt_i = fr * bt_im_ref[...] + fi * bt_re_ref[...]

    rev = (t - 1) - lax.broadcasted_iota(jnp.int32, (t, 1), 0)
    pr, pi = _complex_pow(ar, ai, rev, n_bits)
    for c in range(n_ch):
        br = bbt_r[c:c + 1, :]
        bi = bbt_i[c:c + 1, :]
        w_re_ref[c * t:(c + 1) * t, :] = (pr * br - pi * bi).astype(BF16)
        w_im_ref[c * t:(c + 1) * t, :] = (pr * bi + pi * br).astype(BF16)
    at_r, at_i = _complex_pow(ar, ai, jnp.full((1, 1), t, jnp.int32), n_bits + 1)
    at_ref[0] = at_r
    at_ref[1] = at_i

    ar_c = _row_to_col(ar)
    ai_c = _row_to_col(ai)
    tau_tile = lax.broadcasted_iota(jnp.int32, (1, LANES), 1) % t
    qr, qi = _complex_pow(ar_c, ai_c, tau_tile, n_bits)
    qr = jnp.concatenate([qr] * (w // LANES), axis=1)
    qi = jnp.concatenate([qi] * (w // LANES), axis=1)
    c_re = _spread_channels(ct_re_ref[...], t)
    c_im = _spread_channels(ct_im_ref[...], t)
    drep_ref[...] = _spread_channels(jnp.broadcast_to(d_ref[...], (8, n_ch)), t)[0:1, :]
    ca_r = c_re * qr - c_im * qi
    ca_i = c_re * qi + c_im * qr
    z = _dot_split(jnp.concatenate([bbt_r, -bbt_i], axis=1),
                   jnp.concatenate([ca_r, ca_i], axis=0))
    v_ref[0] = (ca_r * ar_c - ca_i * ai_c).astype(BF16)
    v_ref[1] = (ca_r * ai_c + ca_i * ar_c).astype(BF16)

    tau = lax.broadcasted_iota(jnp.int32, (1, w), 1) % t
    s_row = lax.broadcasted_iota(jnp.int32, (t, 1), 0)
    causal = tau >= s_row
    for c in range(n_ch):
        strip = jnp.broadcast_to(z[c:c + 1, :], (t, w))
        shifted = pltpu.roll(strip, 0, 1, stride=1, stride_axis=0)
        m_ref[c * t:(c + 1) * t, :] = jnp.where(causal, shifted, 0.0).astype(BF16)


def _ssm_fold(u_ref, fold_ref, ub_ref, n_chunks):
    t = CHUNK
    n_ch = SSM_GROUP_CH
    pairs = n_chunks // 2
    half_lane = lax.broadcasted_iota(jnp.int32, (pairs, LANES), 1) < t
    for j in range(pairs):
        fold_ref[j * n_ch:(j + 1) * n_ch, :] = u_ref[:, j * LANES:(j + 1) * LANES].astype(F32)
    for a in range(n_ch // 2):
        lo = fold_ref[pl.ds(2 * a, pairs, stride=n_ch), :]
        hi = fold_ref[pl.ds(2 * a + 1, pairs, stride=n_ch), :]
        even = jnp.where(half_lane, lo, pltpu.roll(hi, t, 1))
        odd = jnp.where(half_lane, pltpu.roll(lo, t, 1), hi)
        ub_ref[0:pairs, a * LANES:(a + 1) * LANES] = even.astype(BF16)
        ub_ref[pairs:n_chunks, a * LANES:(a + 1) * LANES] = odd.astype(BF16)


def _ssm_unfold(y, fold_ref, y_ref, n_chunks):
    t = CHUNK
    n_ch = SSM_GROUP_CH
    pairs = n_chunks // 2
    half_lane = lax.broadcasted_iota(jnp.int32, (pairs, LANES), 1) < t
    for a in range(n_ch // 2):
        even = y[0:pairs, a * LANES:(a + 1) * LANES]
        odd = y[pairs:n_chunks, a * LANES:(a + 1) * LANES]
        fold_ref[pl.ds(2 * a, pairs, stride=n_ch), :] = jnp.where(
            half_lane, even, pltpu.roll(odd, t, 1))
        fold_ref[pl.ds(2 * a + 1, pairs, stride=n_ch), :] = jnp.where(
            half_lane, pltpu.roll(even, t, 1), odd)
    for j in range(pairs):
        y_ref[:, j * LANES:(j + 1) * LANES] = fold_ref[j * n_ch:(j + 1) * n_ch, :].astype(y_ref.dtype)


def _ssm_kernel(*refs, n_chunks):
    n_prm = 8
    u_ref = refs[0]
    prm_all = refs[1:1 + n_prm]
    y_ref = refs[1 + n_prm]
    (m_ref, w_re_ref, w_im_ref, v_ref, at_ref, drep_ref,
     s_re_ref, s_im_ref, h_re_ref, h_im_ref, fold_ref, ub_ref, y_scr) = refs[2 + n_prm:]
    n_ch = SSM_GROUP_CH
    pairs = n_chunks // 2
    k = pl.program_id(0)
    n_groups = 2 * pl.num_programs(0)

    def prm_of(group):
        return tuple(r.at[group] for r in prm_all)

    def build(prm, slot):
        _ssm_build(prm, m_ref.at[slot], w_re_ref.at[slot], w_im_ref.at[slot], v_ref.at[slot],
                   at_ref.at[slot], drep_ref.at[slot])

    def data(slot, build_next):
        rows = slice(slot * n_ch, (slot + 1) * n_ch)
        _ssm_fold(u_ref.at[rows, :], fold_ref, ub_ref, n_chunks)
        build_next()
        ub = ub_ref[...]
        y_scr[...] = jnp.dot(ub, m_ref[slot], preferred_element_type=F32)
        s_re_ref[...] = jnp.dot(ub, w_re_ref[slot], preferred_element_type=F32)
        s_im_ref[...] = jnp.dot(ub, w_im_ref[slot], preferred_element_type=F32)
        at_r = at_ref[slot, 0]
        at_i = at_ref[slot, 1]

        def step(j, carry):
            hr, hi = carry
            for r in (j, pairs + j):
                h_re_ref[pl.ds(r, 1), :] = hr
                h_im_ref[pl.ds(r, 1), :] = hi
                sr = s_re_ref[pl.ds(r, 1), :]
                si = s_im_ref[pl.ds(r, 1), :]
                hr, hi = at_r * hr - at_i * hi + sr, at_r * hi + at_i * hr + si
            return hr, hi

        zero = jnp.zeros((1, SSM_STATE), F32)
        lax.fori_loop(0, pairs, step, (zero, zero))

        y = y_scr[...] + jnp.dot(h_re_ref[...].astype(BF16), v_ref[slot, 0],
                                 preferred_element_type=F32)
        y = y - jnp.dot(h_im_ref[...].astype(BF16), v_ref[slot, 1], preferred_element_type=F32)
        y = _gelu_tanh(y + drep_ref[slot] * ub_ref[...].astype(F32))
        _ssm_unfold(y, fold_ref, y_ref.at[rows, :], n_chunks)

    @pl.when(k == 0)
    def _():
        build(prm_of(0), 0)

    data(0, lambda: build(prm_of(2 * k + 1), 1))
    data(1, lambda: build(prm_of(jnp.minimum(2 * k + 2, n_groups - 1)), 0))


def _ssm_core(u_t, params, layer):
    width, l = u_t.shape
    p = SSM_STATE
    c = SSM_GROUP_CH
    g = width // c
    assert 2 * CHUNK == LANES and l % (16 * LANES) == 0
    assert g % 2 == 0
    n_chunks = l // CHUNK
    w = c * CHUNK
    shapes = [(1, 1), (1, p), (1, p), (c, p), (c, p), (p, c), (p, c), (1, c)]

    pspecs = [pl.BlockSpec((None, g) + shp, lambda k: (layer, 0, 0, 0)) for shp in shapes]
    data = pl.BlockSpec((2 * c, l), lambda k: (k, 0))
    return pl.pallas_call(
        functools.partial(_ssm_kernel, n_chunks=n_chunks),
        grid=(g // 2,),
        in_specs=[data] + pspecs,
        out_specs=data,
        out_shape=jax.ShapeDtypeStruct((width, l), BF16),
        scratch_shapes=[pltpu.VMEM((2, w, w), BF16),
                        pltpu.VMEM((2, w, p), BF16), pltpu.VMEM((2, w, p), BF16),
                        pltpu.VMEM((2, 2, p, w), BF16),
                        pltpu.VMEM((2, 2, 1, p), F32),
                        pltpu.VMEM((2, 1, w), F32),
                        pltpu.VMEM((n_chunks, p), F32), pltpu.VMEM((n_chunks, p), F32),
                        pltpu.VMEM((n_chunks, p), F32), pltpu.VMEM((n_chunks, p), F32),
                        pltpu.VMEM((l // LANES * c, LANES), F32),
                        pltpu.VMEM((n_chunks, w), BF16),
                        pltpu.VMEM((n_chunks, w), F32)],
        compiler_params=_cparams("arbitrary"),
        name="ssm_core",
    )(u_t, *params)


def _ssm_post_kernel(y_ref, w_ref, b_ref, g_ref, o_ref):
    yb = y_ref[...]
    y = yb.astype(F32)
    z = jnp.dot(w_ref[...], yb, preferred_element_type=F32) + b_ref[...]
    o = y * (1.0 / (1.0 + jnp.exp(-z)))
    w, tl = o.shape
    o3 = o.reshape(w // SSM_GROUP_CH, SSM_GROUP_CH, tl)
    ms = jnp.mean(o3 * o3, axis=1, keepdims=True)
    o3 = o3 * lax.rsqrt(ms + NORM_EPS)
    o_ref[...] = (o3.reshape(w, tl) * g_ref[...]).astype(o_ref.dtype)


def _ssm_post(y_t, glu_w_t, glu_b_col, g_col, layer, tl):
    w, l = y_t.shape
    col = pl.BlockSpec((None, w, 1), lambda i: (layer, 0, 0))
    return pl.pallas_call(
        _ssm_post_kernel,
        grid=(l // tl,),
        in_specs=[pl.BlockSpec((w, tl), lambda i: (0, i)),
                  pl.BlockSpec((None, w, w), lambda i: (layer, 0, 0)), col, col],
        out_specs=pl.BlockSpec((w, tl), lambda i: (0, i)),
        out_shape=jax.ShapeDtypeStruct((w, l), BF16),
        compiler_params=_cparams("parallel"),
        name="ssm_post",
    )(y_t, glu_w_t, glu_b_col, g_col)


def _layer_norm(y, g, b):
    mu = jnp.mean(y, axis=-1, keepdims=True)
    yc = y - mu
    var = jnp.mean(yc * yc, axis=-1, keepdims=True)
    return yc * lax.rsqrt(var + NORM_EPS) * g + b


def _out_proj_kernel(*refs):
    a_refs = refs[:NQ]
    s_ref, wa_ref, ws_ref, x_ref, g_ref, b_ref, o_ref, ob_ref = refs[NQ:]
    tn_dims = (((0,), (0,)), ((), ()))
    piece = pl.program_id(0) // (pl.num_programs(0) // NQ)
    attn = a_refs[NQ - 1][...]
    for c in range(NQ - 2, -1, -1):
        attn = jnp.where(piece == c, a_refs[c][...], attn)
    mix = lax.dot_general(attn, wa_ref[...], tn_dims, preferred_element_type=F32)
    mix = mix + lax.dot_general(s_ref[...], ws_ref[...], tn_dims, preferred_element_type=F32)
    y = _layer_norm(DEEPNORM_ALPHA * x_ref[...] + mix, g_ref[...], b_ref[...])
    o_ref[...] = y
    ob_ref[...] = y.astype(BF16)


def _out_proj_ln(attn_pieces, ssm_t, w_out_b, x, g, b, layer, tm):
    l, d = x.shape
    a = attn_pieces[0].shape[0]
    s = ssm_t.shape[0]
    assert a == s
    per = l // tm // NQ

    def piece_spec(c):
        return pl.BlockSpec((a, tm), lambda i: (0, jnp.clip(i - c * per, 0, per - 1)))

    row = pl.BlockSpec((None, 1, d), lambda i: (layer, 0, 0))
    return pl.pallas_call(
        _out_proj_kernel,
        grid=(l // tm,),
        in_specs=[piece_spec(c) for c in range(NQ)] + [
                  pl.BlockSpec((s, tm), lambda i: (0, i)),
                  pl.BlockSpec((None, a, d), lambda i: (layer, 0, 0)),
                  pl.BlockSpec((None, s, d), lambda i: (layer, 1, 0)),
                  pl.BlockSpec((tm, d), lambda i: (i, 0)), row, row],
        out_specs=[pl.BlockSpec((tm, d), lambda i: (i, 0)),
                   pl.BlockSpec((tm, d), lambda i: (i, 0))],
        out_shape=[jax.ShapeDtypeStruct((l, d), F32), jax.ShapeDtypeStruct((l, d), BF16)],
        compiler_params=_cparams("parallel"),
        name="out_proj_ln",
    )(*attn_pieces, ssm_t, w_out_b, w_out_b, x, g, b)


def _mlp_kernel(xb_ref, x_ref, wu_ref, wd_ref, g_ref, b_ref, o_ref, ob_ref, acc_ref):
    j = pl.program_id(1)

    @pl.when(j == 0)
    def _():
        acc_ref[...] = jnp.zeros(acc_ref.shape, F32)

    h = jnp.dot(xb_ref[...], wu_ref[...], preferred_element_type=F32)
    h = jnp.maximum(h, 0.0)
    h = (h * h).astype(BF16)
    acc_ref[...] += jnp.dot(h, wd_ref[...], preferred_element_type=F32)

    @pl.when(j == pl.num_programs(1) - 1)
    def _():
        y = _layer_norm(DEEPNORM_ALPHA * x_ref[...] + acc_ref[...], g_ref[...], b_ref[...])
        o_ref[...] = y
        ob_ref[...] = y.astype(BF16)


def _mlp_ln(xb, x, w_up_b, w_down_b, g, b, layer, tm, tf):
    l, d = x.shape
    f = w_up_b.shape[2]
    row = pl.BlockSpec((None, 1, d), lambda i, j: (layer, 0, 0))
    return pl.pallas_call(
        _mlp_kernel,
        grid=(l // tm, f // tf),
        in_specs=[pl.BlockSpec((tm, d), lambda i, j: (i, 0)),
                  pl.BlockSpec((tm, d), lambda i, j: (i, 0)),
                  pl.BlockSpec((None, d, tf), lambda i, j: (layer, 0, j)),
                  pl.BlockSpec((None, tf, d), lambda i, j: (layer, j, 0)), row, row],
        out_specs=[pl.BlockSpec((tm, d), lambda i, j: (i, 0)),
                   pl.BlockSpec((tm, d), lambda i, j: (i, 0))],
        out_shape=[jax.ShapeDtypeStruct((l, d), F32), jax.ShapeDtypeStruct((l, d), BF16)],
        scratch_shapes=[pltpu.VMEM((tm, d), F32)],
        compiler_params=_cparams("parallel", "arbitrary"),
        name="mlp_ln",
    )(xb, x, w_up_b, w_down_b, g, b)


def _pick(n, pref):
    while n % pref:
        pref //= 2
    return pref


def kernel(x, w_in, lambda_q1, lambda_k1, lambda_q2, lambda_k2, attn_norm_g, ssm_lambda_re, ssm_lambda_im, ssm_log_dt, ssm_b_re, ssm_b_im, ssm_c_re, ssm_c_im, ssm_d, glu_w, glu_b, ssm_norm_g, w_out, ln1_g, ln1_b, w_up, w_down, ln2_g, ln2_b):
    bsz, seq, d = x.shape
    depth = w_in.shape[0]
    attn_w = attn_norm_g.shape[1]
    ssm_w = ssm_d.shape[1]
    n_groups = ssm_w // SSM_GROUP_CH
    assert bsz == 1 and seq % CHUNK == 0
    n_chunks = seq // CHUNK
    p = SSM_STATE
    c = SSM_GROUP_CH

    w_in_b = w_in.astype(BF16)
    w_vu_t = w_in_b[:, :, 2 * attn_w:].transpose(0, 2, 1)
    w_out_b = w_out.astype(BF16)
    w_up_b = w_up.astype(BF16)
    w_down_b = w_down.astype(BF16)
    glu_w_t = glu_w.transpose(0, 2, 1).astype(BF16)
    lq1 = lambda_q1.reshape(depth, 1, HEAD_DIM)
    lk1 = lambda_k1.reshape(depth, 1, HEAD_DIM)
    lq2 = lambda_q2.reshape(depth, 1, HEAD_DIM)
    lk2 = lambda_k2.reshape(depth, 1, HEAD_DIM)
    attn_g_col = attn_norm_g.reshape(depth, attn_w, 1)
    ldt = ssm_log_dt.reshape(depth, n_groups, 1, 1)
    lam_re = ssm_lambda_re.reshape(depth, n_groups, 1, p)
    lam_im = ssm_lambda_im.reshape(depth, n_groups, 1, p)
    bt_re = ssm_b_re.transpose(0, 1, 3, 2)
    bt_im = ssm_b_im.transpose(0, 1, 3, 2)
    ct_re = ssm_c_re.transpose(0, 1, 3, 2)
    ct_im = ssm_c_im.transpose(0, 1, 3, 2)
    d_skip = ssm_d.reshape(depth, n_groups, 1, c)
    glu_b_col = glu_b.reshape(depth, ssm_w, 1)
    ssm_g_col = ssm_norm_g.reshape(depth, ssm_w, 1)
    ln1g, ln1b = ln1_g.reshape(depth, 1, d), ln1_b.reshape(depth, 1, d)
    ln2g, ln2b = ln2_g.reshape(depth, 1, d), ln2_b.reshape(depth, 1, d)

    blk = _pick(seq, ATTN_BLOCK)
    xf = x.reshape(seq, d)
    xb = xf.astype(BF16)
    for l in range(depth):
        lam_init = 0.8 - 0.6 * math.exp(-0.3 * l)
        qk = _proj_qk(xb, w_in_b, l, 2 * attn_w, attn_w, _pick(seq, 1024), 1024)
        v_t, u_t = _proj_vu(w_vu_t, xb, l, attn_w, _pick(seq, 512))

        attn_pieces = _diff_attention(qk, v_t, lq1, lk1, lq2, lk2, attn_g_col, l, lam_init, blk)

        y_t = _ssm_core(u_t, (ldt, lam_re, lam_im, bt_re, bt_im, ct_re, ct_im, d_skip), l)
        ssm_t = _ssm_post(y_t, glu_w_t, glu_b_col, ssm_g_col, l, _pick(seq, 512))

        xf, xb = _out_proj_ln(attn_pieces, ssm_t, w_out_b, xf, ln1g, ln1b, l,
                              _pick(seq // NQ, 512))
        xf, xb = _mlp_ln(xb, xf, w_up_b, w_down_b, ln2g, ln2b, l, _pick(seq, 512), 1024)
    return xf.reshape(bsz, seq, d)
```

```python
import functools
import math

import jax
import jax.numpy as jnp
from jax import lax
from jax.experimental import pallas as pl
from jax.experimental.pallas import tpu as pltpu

F32 = jnp.float32
BF16 = jnp.bfloat16

DEPTH = 4
HEAD_DIM = 64
HEAD_W = 2 * HEAD_DIM
CHUNK = 64
SSM_GROUP_CH = 16
SSM_STATE = 64
DEEPNORM_ALPHA = (2.0 * DEPTH) ** 0.25
NORM_EPS = 1e-5
MASK_VALUE = -1e30
QK_SCALE_LOG2E = HEAD_DIM ** -0.5 * math.log2(math.e)
ONES_ROWS = 16
UNROLL = 10
NQ = 8
ATTN_BLOCK = 512
LANES = 128
VMEM_LIMIT = 56 * 1024 * 1024


def _cparams(*sem):
    return pltpu.CompilerParams(dimension_semantics=sem, vmem_limit_bytes=VMEM_LIMIT)


def _mm_qk_kernel(a_ref, b_ref, o_ref, *, q_blocks):
    acc = jnp.dot(a_ref[...], b_ref[...], preferred_element_type=F32)
    scale = jnp.where(pl.program_id(1) < q_blocks, QK_SCALE_LOG2E, 1.0)
    o_ref[...] = (acc * scale).astype(o_ref.dtype)


def _proj_qk(xb, w_in_b, layer, n_cols, q_cols, tm, tn):
    m, k = xb.shape
    return pl.pallas_call(
        functools.partial(_mm_qk_kernel, q_blocks=q_cols // tn),
        grid=(m // tm, n_cols // tn),
        in_specs=[pl.BlockSpec((tm, k), lambda i, j: (i, 0)),
                  pl.BlockSpec((None, k, tn), lambda i, j: (layer, 0, j))],
        out_specs=pl.BlockSpec((tm, tn), lambda i, j: (i, j)),
        out_shape=jax.ShapeDtypeStruct((m, n_cols), BF16),
        compiler_params=_cparams("parallel", "arbitrary"),
        name="proj_qk",
    )(xb, w_in_b)


def _mm_vu_kernel(w_ref, x_ref, v_ref, u_ref):
    nt_dims = (((1,), (1,)), ((), ()))
    rows_v = v_ref.shape[0]
    x = x_ref[...]
    v_ref[...] = lax.dot_general(w_ref[:rows_v, :], x, nt_dims,
                                 preferred_element_type=F32).astype(v_ref.dtype)
    u_ref[...] = lax.dot_general(w_ref[rows_v:, :], x, nt_dims,
                                 preferred_element_type=F32).astype(u_ref.dtype)


def _proj_vu(w_vu_t, xb, layer, rows_v, tl):
    _, r, k = w_vu_t.shape
    l = xb.shape[0]
    rows_u = r - rows_v
    return pl.pallas_call(
        _mm_vu_kernel,
        grid=(l // tl,),
        in_specs=[pl.BlockSpec((None, r, k), lambda i: (layer, 0, 0)),
                  pl.BlockSpec((tl, k), lambda i: (i, 0))],
        out_specs=[pl.BlockSpec((rows_v, tl), lambda i: (0, i)),
                   pl.BlockSpec((rows_u, tl), lambda i: (0, i))],
        out_shape=[jax.ShapeDtypeStruct((rows_v, l), BF16),
                   jax.ShapeDtypeStruct((rows_u, l), BF16)],
        compiler_params=_cparams("parallel"),
        name="proj_vu",
    )(w_vu_t, xb)


def _query_blocks(i, n_qblk):
    per = n_qblk // NQ
    blocks = []
    for j in range(NQ // 2):
        blocks += [2 * j * per + i, (2 * j + 2) * per - 1 - i]
    return tuple(blocks)


def _attn_kernel(*refs, blk, n_qblk, lam_init):
    lq1_ref, lk1_ref, lq2_ref, lk2_ref = refs[:4]
    q_refs = refs[4:4 + NQ]
    k_ref, vt_ref, g_ref = refs[4 + NQ:7 + NQ]
    o_refs = refs[7 + NQ:7 + 2 * NQ]
    qz_ref, s_ref, mb_ref, m_ref, acc_ref = refs[7 + 2 * NQ:]
    i = pl.program_id(1)
    q_blocks = _query_blocks(i, n_qblk)
    starts = [0]
    for b in q_blocks[:-1]:
        starts.append(starts[-1] + b)
    n_items = NQ + NQ // 2 * (n_qblk - 1)
    nt_dims = (((1,), (1,)), ((), ()))

    lane = lax.broadcasted_iota(jnp.int32, (blk, HEAD_W), 1)
    for sel, q_ref in enumerate(q_refs):
        q = q_ref[...]
        zero = jnp.zeros_like(q)
        qz_ref[sel, 0] = jnp.where(lane < HEAD_DIM, q, zero)
        qz_ref[sel, 1] = jnp.where(lane >= HEAD_DIM, q, zero)
    ones = jnp.ones((ONES_ROWS, blk), BF16)

    def stage_a(block, sel, slot, diagonal):
        st = pl.multiple_of(block * blk, blk)
        kb = k_ref[pl.ds(st, blk), :]
        for mp in range(2):
            s = lax.dot_general(kb, qz_ref[sel, mp], nt_dims, preferred_element_type=F32)
            if diagonal:
                key_chunk = lax.broadcasted_iota(jnp.int32, (blk, blk), 0) // CHUNK
                qry_chunk = lax.broadcasted_iota(jnp.int32, (blk, blk), 1) // CHUNK
                s = jnp.where(key_chunk <= qry_chunk, s, MASK_VALUE)
            s_ref[slot, mp] = s
            mb_ref[slot, mp] = jnp.max(s, axis=0, keepdims=True)

    def stage_b(block, sel, slot, first):
        st = pl.multiple_of(block * blk, blk)
        v_ext = jnp.concatenate([vt_ref[:, pl.ds(st, blk)], ones], axis=0)
        for mp in range(2):
            if first:
                m_new = mb_ref[slot, mp]
            else:
                m_old = m_ref[sel, mp]
                m_new = jnp.maximum(m_old, mb_ref[slot, mp])
            p = jnp.exp2(s_ref[slot, mp] - m_new).astype(BF16)
            pv = jnp.dot(v_ext, p, preferred_element_type=F32)
            if first:
                acc_ref[sel, mp] = pv
            else:
                acc_ref[sel, mp] = jnp.exp2(m_old - m_new) * acc_ref[sel, mp] + pv
            m_ref[sel, mp] = m_new

    def item(k):
        if isinstance(k, int) and k < NQ:
            return q_blocks[k], k, True
        idx = k - NQ
        sel = sum((idx >= st).astype(jnp.int32) for st in starts[1:])
        first = starts[NQ - 1]
        for j in range(NQ - 2, -1, -1):
            first = jnp.where(sel == j, starts[j], first)
        return idx - first, sel, False

    def run_items(base, count):
        for q in range(count):
            nb, ns, nd = item(base + q + 1)
            stage_a(nb, ns, q % 2, nd)
            cb, cs, cd = item(base + q)
            stage_b(cb, cs, (q + 1) % 2, cd)

    stage_a(q_blocks[0], 0, 0, True)
    nb, ns, nd = item(1)
    stage_a(nb, ns, 1, nd)
    stage_b(q_blocks[0], 0, 0, True)
    run_items(1, UNROLL)

    def trip(t, carry):
        run_items(1 + UNROLL * t, UNROLL)
        return carry

    lax.fori_loop(1, (n_items - 2) // UNROLL, trip, 0)
    b_last, s_last, _ = item(n_items - 1)
    stage_b(b_last, s_last, (n_items - 1) % 2, False)

    lam = (jnp.exp(jnp.sum(lq1_ref[...] * lk1_ref[...], axis=1, keepdims=True))
           - jnp.exp(jnp.sum(lq2_ref[...] * lk2_ref[...], axis=1, keepdims=True))
           + lam_init)
    for sel, o_ref in enumerate(o_refs):
        o = (acc_ref[sel, 0, :HEAD_W, :] / acc_ref[sel, 0, HEAD_W:HEAD_W + 1, :]
             - lam * (acc_ref[sel, 1, :HEAD_W, :] / acc_ref[sel, 1, HEAD_W:HEAD_W + 1, :]))
        ms = jnp.mean(o * o, axis=0, keepdims=True)
        o = o * lax.rsqrt(ms + NORM_EPS) * g_ref[...] * (1.0 - lam_init)
        o_ref[...] = o.astype(o_ref.dtype)


def _diff_attention(qk, v_t, lq1, lk1, lq2, lk2, g_col, layer, lam_init, blk):
    l = qk.shape[0]
    a = v_t.shape[0]
    n_heads = a // HEAD_W
    n_qblk = l // blk
    assert NQ % 2 == 0 and UNROLL % 2 == 0 and n_qblk % NQ == 0
    assert UNROLL >= NQ - 1
    assert (NQ + NQ // 2 * (n_qblk - 1) - 2) % UNROLL == 0
    per = n_qblk // NQ
    lam_spec = pl.BlockSpec((None, 1, HEAD_DIM), lambda h, i: (layer, 0, 0))
    out_sds = jax.ShapeDtypeStruct((a, l // NQ), BF16)

    def q_spec(sel):
        return pl.BlockSpec((blk, HEAD_W), lambda h, i: (_query_blocks(i, n_qblk)[sel], h))

    def o_spec(sel):
        return pl.BlockSpec((HEAD_W, blk),
                            lambda h, i: (h, _query_blocks(i, n_qblk)[sel] - sel * per))

    return pl.pallas_call(
        functools.partial(_attn_kernel, blk=blk, n_qblk=n_qblk, lam_init=lam_init),
        grid=(n_heads, per),
        in_specs=[lam_spec, lam_spec, lam_spec, lam_spec] + [q_spec(sel) for sel in range(NQ)] + [
            pl.BlockSpec((l, HEAD_W), lambda h, i: (0, n_heads + h)),
            pl.BlockSpec((HEAD_W, l), lambda h, i: (h, 0)),
            pl.BlockSpec((None, HEAD_W, 1), lambda h, i: (layer, h, 0))],
        out_specs=[o_spec(sel) for sel in range(NQ)],
        out_shape=[out_sds] * NQ,
        scratch_shapes=[pltpu.VMEM((NQ, 2, blk, HEAD_W), BF16),
                        pltpu.VMEM((2, 2, blk, blk), F32),
                        pltpu.VMEM((2, 2, 1, blk), F32),
                        pltpu.VMEM((NQ, 2, 1, blk), F32),
                        pltpu.VMEM((NQ, 2, HEAD_W + ONES_ROWS, blk), F32)],
        compiler_params=_cparams("parallel", "arbitrary"),
        name="diff_attn",
    )(lq1, lk1, lq2, lk2, *([qk] * (NQ + 1)), v_t, g_col)


def _complex_pow(ar, ai, e, n_bits):
    shape = jnp.broadcast_shapes(ar.shape, e.shape)
    pr = jnp.ones(shape, F32)
    pi = jnp.zeros(shape, F32)
    fr, fi = ar, ai
    for b in range(n_bits):
        bit = ((e >> b) & 1) == 1
        nr = pr * fr - pi * fi
        ni = pr * fi + pi * fr
        pr = jnp.where(bit, nr, pr)
        pi = jnp.where(bit, ni, pi)
        if b + 1 < n_bits:
            fr, fi = fr * fr - fi * fi, 2.0 * fr * fi
    return pr, pi


def _gelu_tanh(y):
    k0 = math.sqrt(2.0 / math.pi)
    return 0.5 * y * (1.0 + jnp.tanh(k0 * (y + 0.044715 * (y * y * y))))


def _spread_channels(x, t):
    n_ch = x.shape[1]
    expand = (lax.broadcasted_iota(jnp.int32, (n_ch, n_ch * t), 1) // t
              == lax.broadcasted_iota(jnp.int32, (n_ch, n_ch * t), 0)).astype(BF16)
    out = None
    rest = x
    for _ in range(3):
        piece = rest.astype(BF16)
        rest = rest - piece.astype(F32)
        part = jnp.dot(piece, expand, preferred_element_type=F32)
        out = part if out is None else out + part
    return out


def _dot_split(a, b):
    a_hi = a.astype(BF16)
    b_hi = b.astype(BF16)
    a_lo = (a - a_hi.astype(F32)).astype(BF16)
    b_lo = (b - b_hi.astype(F32)).astype(BF16)
    small = (jnp.dot(a_hi, b_lo, preferred_element_type=F32)
             + jnp.dot(a_lo, b_hi, preferred_element_type=F32))
    return jnp.dot(a_hi, b_hi, preferred_element_type=F32) + small


def _row_to_col(row):
    n = row.shape[1]
    eye = (lax.broadcasted_iota(jnp.int32, (n, n), 0)
           == lax.broadcasted_iota(jnp.int32, (n, n), 1))
    return jnp.sum(jnp.where(eye, jnp.broadcast_to(row, (n, n)), 0.0), axis=1, keepdims=True)


def _ssm_build(prm, m_ref, w_re_ref, w_im_ref, v_ref, at_ref, drep_ref):
    ldt_ref, lr_ref, li_ref, bt_re_ref, bt_im_ref, ct_re_ref, ct_im_ref, d_ref = prm
    t = CHUNK
    n_ch = SSM_GROUP_CH
    w = n_ch * t
    n_bits = t.bit_length() - 1
    dt = jnp.exp(ldt_ref[...])

    lr = lr_ref[...]
    li = li_ref[...]
    mag = jnp.exp(lr * dt)
    ar = mag * jnp.cos(li * dt)
    ai = mag * jnp.sin(li * dt)
    den = lr * lr + li * li
    nr = ar - 1.0
    fr = (nr * lr + ai * li) / den
    fi = (ai * lr - nr * li) / den
    bbt_r = fr * bt_re_ref[...] - fi * bt_im_ref[...]
    bbt_i = fr * bt_im_ref[...] + fi * bt_re_ref[...]

    rev = (t - 1) - lax.broadcasted_iota(jnp.int32, (t, 1), 0)
    pr, pi = _complex_pow(ar, ai, rev, n_bits)
    for c in range(n_ch):
        br = bbt_r[c:c + 1, :]
        bi = bbt_i[c:c + 1, :]
        w_re_ref[c * t:(c + 1) * t, :] = (pr * br - pi * bi).astype(BF16)
        w_im_ref[c * t:(c + 1) * t, :] = (pr * bi + pi * br).astype(BF16)
    at_r, at_i = _complex_pow(ar, ai, jnp.full((1, 1), t, jnp.int32), n_bits + 1)
    at_ref[0] = at_r
    at_ref[1] = at_i

    ar_c = _row_to_col(ar)
    ai_c = _row_to_col(ai)
    tau_tile = lax.broadcasted_iota(jnp.int32, (1, LANES), 1) % t
    qr, qi = _complex_pow(ar_c, ai_c, tau_tile, n_bits)
    qr = jnp.concatenate([qr] * (w // LANES), axis=1)
    qi = jnp.concatenate([qi] * (w // LANES), axis=1)
    c_re = _spread_channels(ct_re_ref[...], t)
    c_im = _spread_channels(ct_im_ref[...], t)
    drep_ref[...] = _spread_channels(jnp.broadcast_to(d_ref[...], (8, n_ch)), t)[0:1, :]
    ca_r = c_re * qr - c_im * qi
    ca_i = c_re * qi + c_im * qr
    z = _dot_split(jnp.concatenate([bbt_r, -bbt_i], axis=1),
                   jnp.concatenate([ca_r, ca_i], axis=0))
    v_ref[0] = (ca_r * ar_c - ca_i * ai_c).astype(BF16)
    v_ref[1] = (ca_r * ai_c + ca_i * ar_c).astype(BF16)

    tau = lax.broadcasted_iota(jnp.int32, (1, w), 1) % t
    s_row = lax.broadcasted_iota(jnp.int32, (t, 1), 0)
    causal = tau >= s_row
    for c in range(n_ch):
        strip = jnp.broadcast_to(z[c:c + 1, :], (t, w))
        shifted = pltpu.roll(strip, 0, 1, stride=1, stride_axis=0)
        m_ref[c * t:(c + 1) * t, :] = jnp.where(causal, shifted, 0.0).astype(BF16)


def _ssm_fold(u_ref, fold_ref, ub_ref, n_chunks):
    t = CHUNK
    n_ch = SSM_GROUP_CH
    pairs = n_chunks // 2
    half_lane = lax.broadcasted_iota(jnp.int32, (pairs, LANES), 1) < t
    for j in range(pairs):
        fold_ref[j * n_ch:(j + 1) * n_ch, :] = u_ref[:, j * LANES:(j + 1) * LANES].astype(F32)
    for a in range(n_ch // 2):
        lo = fold_ref[pl.ds(2 * a, pairs, stride=n_ch), :]
        hi = fold_ref[pl.ds(2 * a + 1, pairs, stride=n_ch), :]
        even = jnp.where(half_lane, lo, pltpu.roll(hi, t, 1))
        odd = jnp.where(half_lane, pltpu.roll(lo, t, 1), hi)
        ub_ref[0:pairs, a * LANES:(a + 1) * LANES] = even.astype(BF16)
        ub_ref[pairs:n_chunks, a * LANES:(a + 1) * LANES] = odd.astype(BF16)


def _ssm_unfold(y, fold_ref, y_ref, n_chunks):
    t = CHUNK
    n_ch = SSM_GROUP_CH
    pairs = n_chunks // 2
    half_lane = lax.broadcasted_iota(jnp.int32, (pairs, LANES), 1) < t
    for a in range(n_ch // 2):
        even = y[0:pairs, a * LANES:(a + 1) * LANES]
        odd = y[pairs:n_chunks, a * LANES:(a + 1) * LANES]
        fold_ref[pl.ds(2 * a, pairs, stride=n_ch), :] = jnp.where(
            half_lane, even, pltpu.roll(odd, t, 1))
        fold_ref[pl.ds(2 * a + 1, pairs, stride=n_ch), :] = jnp.where(
            half_lane, pltpu.roll(even, t, 1), odd)
    for j in range(pairs):
        y_ref[:, j * LANES:(j + 1) * LANES] = fold_ref[j * n_ch:(j + 1) * n_ch, :].astype(y_ref.dtype)


def _ssm_kernel(*refs, n_chunks):
    n_prm = 8
    u_ref = refs[0]
    prm_all = refs[1:1 + n_prm]
    y_ref = refs[1 + n_prm]
    (m_ref, w_re_ref, w_im_ref, v_ref, at_ref, drep_ref,
     s_re_ref, s_im_ref, h_re_ref, h_im_ref, fold_ref, ub_ref, y_scr) = refs[2 + n_prm:]
    n_ch = SSM_GROUP_CH
    pairs = n_chunks // 2
    k = pl.program_id(0)
    n_groups = 2 * pl.num_programs(0)

    def prm_of(group):
        return tuple(r.at[group] for r in prm_all)

    def build(prm, slot):
        _ssm_build(prm, m_ref.at[slot], w_re_ref.at[slot], w_im_ref.at[slot], v_ref.at[slot],
                   at_ref.at[slot], drep_ref.at[slot])

    def data(slot, build_next):
        rows = slice(slot * n_ch, (slot + 1) * n_ch)
        _ssm_fold(u_ref.at[rows, :], fold_ref, ub_ref, n_chunks)
        build_next()
        ub = ub_ref[...]
        y_scr[...] = jnp.dot(ub, m_ref[slot], preferred_element_type=F32)
        s_re_ref[...] = jnp.dot(ub, w_re_ref[slot], preferred_element_type=F32)
        s_im_ref[...] = jnp.dot(ub, w_im_ref[slot], preferred_element_type=F32)
        at_r = at_ref[slot, 0]
        at_i = at_ref[slot, 1]

        def step(j, carry):
            hr, hi = carry
            for r in (j, pairs + j):
                h_re_ref[pl.ds(r, 1), :] = hr
                h_im_ref[pl.ds(r, 1), :] = hi
                sr = s_re_ref[pl.ds(r, 1), :]
                si = s_im_ref[pl.ds(r, 1), :]
                hr, hi = at_r * hr - at_i * hi + sr, at_r * hi + at_i * hr + si
            return hr, hi

        zero = jnp.zeros((1, SSM_STATE), F32)
        lax.fori_loop(0, pairs, step, (zero, zero))

        y = y_scr[...] + jnp.dot(h_re_ref[...].astype(BF16), v_ref[slot, 0],
                                 preferred_element_type=F32)
        y = y - jnp.dot(h_im_ref[...].astype(BF16), v_ref[slot, 1], preferred_element_type=F32)
        y = _gelu_tanh(y + drep_ref[slot] * ub_ref[...].astype(F32))
        _ssm_unfold(y, fold_ref, y_ref.at[rows, :], n_chunks)

    @pl.when(k == 0)
    def _():
        build(prm_of(0), 0)

    data(0, lambda: build(prm_of(2 * k + 1), 1))
    data(1, lambda: build(prm_of(jnp.minimum(2 * k + 2, n_groups - 1)), 0))


def _ssm_core(u_t, params, layer):
    width, l = u_t.shape
    p = SSM_STATE
    c = SSM_GROUP_CH
    g = width // c
    assert 2 * CHUNK == LANES and l % (16 * LANES) == 0
    assert g % 2 == 0
    n_chunks = l // CHUNK
    w = c * CHUNK
    shapes = [(1, 1), (1, p), (1, p), (c, p), (c, p), (p, c), (p, c), (1, c)]

    pspecs = [pl.BlockSpec((None, g) + shp, lambda k: (layer, 0, 0, 0)) for shp in shapes]
    data = pl.BlockSpec((2 * c, l), lambda k: (k, 0))
    return pl.pallas_call(
        functools.partial(_ssm_kernel, n_chunks=n_chunks),
        grid=(g // 2,),
        in_specs=[data] + pspecs,
        out_specs=data,
        out_shape=jax.ShapeDtypeStruct((width, l), BF16),
        scratch_shapes=[pltpu.VMEM((2, w, w), BF16),
                        pltpu.VMEM((2, w, p), BF16), pltpu.VMEM((2, w, p), BF16),
                        pltpu.VMEM((2, 2, p, w), BF16),
                        pltpu.VMEM((2, 2, 1, p), F32),
                        pltpu.VMEM((2, 1, w), F32),
                        pltpu.VMEM((n_chunks, p), F32), pltpu.VMEM((n_chunks, p), F32),
                        pltpu.VMEM((n_chunks, p), F32), pltpu.VMEM((n_chunks, p), F32),
                        pltpu.VMEM((l // LANES * c, LANES), F32),
                        pltpu.VMEM((n_chunks, w), BF16),
                        pltpu.VMEM((n_chunks, w), F32)],
        compiler_params=_cparams("arbitrary"),
        name="ssm_core",
    )(u_t, *params)


def _layer_norm(y, g, b):
    mu = jnp.mean(y, axis=-1, keepdims=True)
    yc = y - mu
    var = jnp.mean(yc * yc, axis=-1, keepdims=True)
    return yc * lax.rsqrt(var + NORM_EPS) * g + b


def _glu_group_norm(yb, w, b_col, g_col):
    tn_dims = (((0,), (0,)), ((), ()))
    y = yb.astype(F32)
    z = lax.dot_general(w, yb, tn_dims, preferred_element_type=F32) + b_col
    o = y * (1.0 / (1.0 + jnp.exp(-z)))
    width, tl = o.shape
    o3 = o.reshape(width // SSM_GROUP_CH, SSM_GROUP_CH, tl)
    ms = jnp.mean(o3 * o3, axis=1, keepdims=True)
    o3 = o3 * lax.rsqrt(ms + NORM_EPS)
    return o3.reshape(width, tl) * g_col


def _out_proj_kernel(*refs):
    a_refs = refs[:NQ]
    (y_ref, glu_w_ref, glu_b_ref, sg_ref, wa_ref, ws_ref, x_ref, g_ref, b_ref,
     o_ref, ob_ref) = refs[NQ:]
    tn_dims = (((0,), (0,)), ((), ()))
    piece = pl.program_id(0) // (pl.num_programs(0) // NQ)
    attn = a_refs[NQ - 1][...]
    for c in range(NQ - 2, -1, -1):
        attn = jnp.where(piece == c, a_refs[c][...], attn)
    mix = lax.dot_general(attn, wa_ref[...], tn_dims, preferred_element_type=F32)
    ssm = _glu_group_norm(y_ref[...], glu_w_ref[...], glu_b_ref[...], sg_ref[...]).astype(BF16)
    mix = mix + lax.dot_general(ssm, ws_ref[...], tn_dims, preferred_element_type=F32)
    y = _layer_norm(DEEPNORM_ALPHA * x_ref[...] + mix, g_ref[...], b_ref[...])
    o_ref[...] = y
    ob_ref[...] = y.astype(BF16)


def _out_proj_ln(attn_pieces, y_t, glu_w_b, glu_b_col, ssm_g_col, w_out_b, x, g, b, layer, tm):
    l, d = x.shape
    a = attn_pieces[0].shape[0]
    s = y_t.shape[0]
    assert a == s
    per = l // tm // NQ

    def piece_spec(c):
        return pl.BlockSpec((a, tm), lambda i: (0, jnp.clip(i - c * per, 0, per - 1)))

    row = pl.BlockSpec((None, 1, d), lambda i: (layer, 0, 0))
    col = pl.BlockSpec((None, s, 1), lambda i: (layer, 0, 0))
    return pl.pallas_call(
        _out_proj_kernel,
        grid=(l // tm,),
        in_specs=[piece_spec(c) for c in range(NQ)] + [
                  pl.BlockSpec((s, tm), lambda i: (0, i)),
                  pl.BlockSpec((None, s, s), lambda i: (layer, 0, 0)), col, col,
                  pl.BlockSpec((None, a, d), lambda i: (layer, 0, 0)),
                  pl.BlockSpec((None, s, d), lambda i: (layer, 1, 0)),
                  pl.BlockSpec((tm, d), lambda i: (i, 0)), row, row],
        out_specs=[pl.BlockSpec((tm, d), lambda i: (i, 0)),
                   pl.BlockSpec((tm, d), lambda i: (i, 0))],
        out_shape=[jax.ShapeDtypeStruct((l, d), F32), jax.ShapeDtypeStruct((l, d), BF16)],
        compiler_params=_cparams("parallel"),
        name="out_proj_ln",
    )(*attn_pieces, y_t, glu_w_b, glu_b_col, ssm_g_col, w_out_b, w_out_b, x, g, b)


def _mlp_kernel(xb_ref, x_ref, wu_ref, wd_ref, g_ref, b_ref, o_ref, ob_ref, acc_ref):
    j = pl.program_id(1)

    @pl.when(j == 0)
    def _():
        acc_ref[...] = jnp.zeros(acc_ref.shape, F32)

    h = jnp.dot(xb_ref[...], wu_ref[...], preferred_element_type=F32)
    h = jnp.maximum(h, 0.0)
    h = (h * h).astype(BF16)
    acc_ref[...] += jnp.dot(h, wd_ref[...], preferred_element_type=F32)

    @pl.when(j == pl.num_programs(1) - 1)
    def _():
        y = _layer_norm(DEEPNORM_ALPHA * x_ref[...] + acc_ref[...], g_ref[...], b_ref[...])
        o_ref[...] = y
        ob_ref[...] = y.astype(BF16)


def _mlp_ln(xb, x, w_up_b, w_down_b, g, b, layer, tm, tf):
    l, d = x.shape
    f = w_up_b.shape[2]
    row = pl.BlockSpec((None, 1, d), lambda i, j: (layer, 0, 0))
    return pl.pallas_call(
        _mlp_kernel,
        grid=(l // tm, f // tf),
        in_specs=[pl.BlockSpec((tm, d), lambda i, j: (i, 0)),
                  pl.BlockSpec((tm, d), lambda i, j: (i, 0)),
                  pl.BlockSpec((None, d, tf), lambda i, j: (layer, 0, j)),
                  pl.BlockSpec((None, tf, d), lambda i, j: (layer, j, 0)), row, row],
        out_specs=[pl.BlockSpec((tm, d), lambda i, j: (i, 0)),
                   pl.BlockSpec((tm, d), lambda i, j: (i, 0))],
        out_shape=[jax.ShapeDtypeStruct((l, d), F32), jax.ShapeDtypeStruct((l, d), BF16)],
        scratch_shapes=[pltpu.VMEM((tm, d), F32)],
        compiler_params=_cparams("parallel", "arbitrary"),
        name="mlp_ln",
    )(xb, x, w_up_b, w_down_b, g, b)


def _pick(n, pref):
    while n % pref:
        pref //= 2
    return pref


def kernel(x, w_in, lambda_q1, lambda_k1, lambda_q2, lambda_k2, attn_norm_g, ssm_lambda_re, ssm_lambda_im, ssm_log_dt, ssm_b_re, ssm_b_im, ssm_c_re, ssm_c_im, ssm_d, glu_w, glu_b, ssm_norm_g, w_out, ln1_g, ln1_b, w_up, w_down, ln2_g, ln2_b):
    bsz, seq, d = x.shape
    depth = w_in.shape[0]
    attn_w = attn_norm_g.shape[1]
    ssm_w = ssm_d.shape[1]
    n_groups = ssm_w // SSM_GROUP_CH
    assert bsz == 1 and seq % CHUNK == 0
    n_chunks = seq // CHUNK
    p = SSM_STATE
    c = SSM_GROUP_CH

    w_in_b = w_in.astype(BF16)
    w_vu_t = w_in_b[:, :, 2 * attn_w:].transpose(0, 2, 1)
    w_out_b = w_out.astype(BF16)
    w_up_b = w_up.astype(BF16)
    w_down_b = w_down.astype(BF16)
    glu_w_b = glu_w.astype(BF16)
    lq1 = lambda_q1.reshape(depth, 1, HEAD_DIM)
    lk1 = lambda_k1.reshape(depth, 1, HEAD_DIM)
    lq2 = lambda_q2.reshape(depth, 1, HEAD_DIM)
    lk2 = lambda_k2.reshape(depth, 1, HEAD_DIM)
    attn_g_col = attn_norm_g.reshape(depth, attn_w, 1)
    ldt = ssm_log_dt.reshape(depth, n_groups, 1, 1)
    lam_re = ssm_lambda_re.reshape(depth, n_groups, 1, p)
    lam_im = ssm_lambda_im.reshape(depth, n_groups, 1, p)
    bt_re = ssm_b_re.transpose(0, 1, 3, 2)
    bt_im = ssm_b_im.transpose(0, 1, 3, 2)
    ct_re = ssm_c_re.transpose(0, 1, 3, 2)
    ct_im = ssm_c_im.transpose(0, 1, 3, 2)
    d_skip = ssm_d.reshape(depth, n_groups, 1, c)
    glu_b_col = glu_b.reshape(depth, ssm_w, 1)
    ssm_g_col = ssm_norm_g.reshape(depth, ssm_w, 1)
    ln1g, ln1b = ln1_g.reshape(depth, 1, d), ln1_b.reshape(depth, 1, d)
    ln2g, ln2b = ln2_g.reshape(depth, 1, d), ln2_b.reshape(depth, 1, d)

    blk = _pick(seq, ATTN_BLOCK)
    xf = x.reshape(seq, d)
    xb = xf.astype(BF16)
    for l in range(depth):
        lam_init = 0.8 - 0.6 * math.exp(-0.3 * l)
        qk = _proj_qk(xb, w_in_b, l, 2 * attn_w, attn_w, _pick(seq, 1024), 1024)
        v_t, u_t = _proj_vu(w_vu_t, xb, l, attn_w, _pick(seq, 512))

        attn_pieces = _diff_attention(qk, v_t, lq1, lk1, lq2, lk2, attn_g_col, l, lam_init, blk)

        y_t = _ssm_core(u_t, (ldt, lam_re, lam_im, bt_re, bt_im, ct_re, ct_im, d_skip), l)
        xf, xb = _out_proj_ln(attn_pieces, y_t, glu_w_b, glu_b_col, ssm_g_col, w_out_b, xf,
                              ln1g, ln1b, l, _pick(seq // NQ, 512))
        xf, xb = _mlp_ln(xb, xf, w_up_b, w_down_b, ln2g, ln2b, l, _pick(seq, 512), 1024)
    return xf.reshape(bsz, seq, d)
```

```python
import functools
import math

import jax
import jax.numpy as jnp
from jax import lax
from jax.experimental import pallas as pl
from jax.experimental.pallas import tpu as pltpu

F32 = jnp.float32
BF16 = jnp.bfloat16

DEPTH = 4
HEAD_DIM = 64
HEAD_W = 2 * HEAD_DIM
CHUNK = 64
SSM_GROUP_CH = 16
SSM_STATE = 64
DEEPNORM_ALPHA = (2.0 * DEPTH) ** 0.25
NORM_EPS = 1e-5
MASK_VALUE = -1e30
QK_SCALE_LOG2E = HEAD_DIM ** -0.5 * math.log2(math.e)
ONES_ROWS = 16
UNROLL = 10
NQ = 8
ATTN_BLOCK = 512
LANES = 128
VMEM_LIMIT = 56 * 1024 * 1024


def _cparams(*sem):
    return pltpu.CompilerParams(dimension_semantics=sem, vmem_limit_bytes=VMEM_LIMIT)


def _mm_qk_kernel(a_ref, b_ref, o_ref, *, q_cols):
    acc = jnp.dot(a_ref[...], b_ref[...], preferred_element_type=F32)
    tn = o_ref.shape[1]
    col = pl.program_id(1) * tn + lax.broadcasted_iota(jnp.int32, (1, tn), 1)
    scale = jnp.where(col < q_cols, QK_SCALE_LOG2E, 1.0)
    o_ref[...] = (acc * scale).astype(o_ref.dtype)


def _proj_qk(xb, w_in_b, layer, n_cols, q_cols, tm, tn):
    m, k = xb.shape
    return pl.pallas_call(
        functools.partial(_mm_qk_kernel, q_cols=q_cols),
        grid=(m // tm, n_cols // tn),
        in_specs=[pl.BlockSpec((tm, k), lambda i, j: (i, 0)),
                  pl.BlockSpec((None, k, tn), lambda i, j: (layer, 0, j))],
        out_specs=pl.BlockSpec((tm, tn), lambda i, j: (i, j)),
        out_shape=jax.ShapeDtypeStruct((m, n_cols), BF16),
        compiler_params=_cparams("parallel", "arbitrary"),
        name="proj_qk",
    )(xb, w_in_b)


def _mm_vu_kernel(w_ref, x_ref, v_ref, u_ref):
    nt_dims = (((1,), (1,)), ((), ()))
    rows_v = v_ref.shape[0]
    x = x_ref[...]
    v_ref[...] = lax.dot_general(w_ref[:rows_v, :], x, nt_dims,
                                 preferred_element_type=F32).astype(v_ref.dtype)
    u_ref[...] = lax.dot_general(w_ref[rows_v:, :], x, nt_dims,
                                 preferred_element_type=F32).astype(u_ref.dtype)


def _proj_vu(w_vu_t, xb, layer, rows_v, tl):
    _, r, k = w_vu_t.shape
    l = xb.shape[0]
    rows_u = r - rows_v
    return pl.pallas_call(
        _mm_vu_kernel,
        grid=(l // tl,),
        in_specs=[pl.BlockSpec((None, r, k), lambda i: (layer, 0, 0)),
                  pl.BlockSpec((tl, k), lambda i: (i, 0))],
        out_specs=[pl.BlockSpec((rows_v, tl), lambda i: (0, i)),
                   pl.BlockSpec((rows_u, tl), lambda i: (0, i))],
        out_shape=[jax.ShapeDtypeStruct((rows_v, l), BF16),
                   jax.ShapeDtypeStruct((rows_u, l), BF16)],
        compiler_params=_cparams("parallel"),
        name="proj_vu",
    )(w_vu_t, xb)


def _query_blocks(i, n_qblk):
    per = n_qblk // NQ
    blocks = []
    for j in range(NQ // 2):
        blocks += [2 * j * per + i, (2 * j + 2) * per - 1 - i]
    return tuple(blocks)


def _attn_kernel(*refs, blk, n_qblk, lam_init):
    lq1_ref, lk1_ref, lq2_ref, lk2_ref = refs[:4]
    q_refs = refs[4:4 + NQ]
    k_ref, vt_ref, g_ref = refs[4 + NQ:7 + NQ]
    o_refs = refs[7 + NQ:7 + 2 * NQ]
    qz_ref, s_ref, mb_ref, m_ref, acc_ref = refs[7 + 2 * NQ:]
    i = pl.program_id(1)
    q_blocks = _query_blocks(i, n_qblk)
    starts = [0]
    for b in q_blocks[:-1]:
        starts.append(starts[-1] + b)
    n_items = NQ + NQ // 2 * (n_qblk - 1)
    nt_dims = (((1,), (1,)), ((), ()))

    lane = lax.broadcasted_iota(jnp.int32, (blk, HEAD_W), 1)
    for sel, q_ref in enumerate(q_refs):
        q = q_ref[...]
        zero = jnp.zeros_like(q)
        qz_ref[sel, 0] = jnp.where(lane < HEAD_DIM, q, zero)
        qz_ref[sel, 1] = jnp.where(lane >= HEAD_DIM, q, zero)
    ones = jnp.ones((ONES_ROWS, blk), BF16)

    def stage_a(block, sel, slot, diagonal):
        st = pl.multiple_of(block * blk, blk)
        kb = k_ref[pl.ds(st, blk), :]
        for mp in range(2):
            s = lax.dot_general(kb, qz_ref[sel, mp], nt_dims, preferred_element_type=F32)
            if diagonal:
                key_chunk = lax.broadcasted_iota(jnp.int32, (blk, blk), 0) // CHUNK
                qry_chunk = lax.broadcasted_iota(jnp.int32, (blk, blk), 1) // CHUNK
                s = jnp.where(key_chunk <= qry_chunk, s, MASK_VALUE)
            s_ref[slot, mp] = s
            mb_ref[slot, mp] = jnp.max(s, axis=0, keepdims=True)

    def stage_b(block, sel, slot, first):
        st = pl.multiple_of(block * blk, blk)
        v_ext = jnp.concatenate([vt_ref[:, pl.ds(st, blk)], ones], axis=0)
        for mp in range(2):
            if first:
                m_new = mb_ref[slot, mp]
            else:
                m_old = m_ref[sel, mp]
                m_new = jnp.maximum(m_old, mb_ref[slot, mp])
            p = jnp.exp2(s_ref[slot, mp] - m_new).astype(BF16)
            pv = jnp.dot(v_ext, p, preferred_element_type=F32)
            if first:
                acc_ref[sel, mp] = pv
            else:
                acc_ref[sel, mp] = jnp.exp2(m_old - m_new) * acc_ref[sel, mp] + pv
            m_ref[sel, mp] = m_new

    def item(k):
        if isinstance(k, int) and k < NQ:
            return q_blocks[k], k, True
        idx = k - NQ
        sel = sum((idx >= st).astype(jnp.int32) for st in starts[1:])
        first = starts[NQ - 1]
        for j in range(NQ - 2, -1, -1):
            first = jnp.where(sel == j, starts[j], first)
        return idx - first, sel, False

    def run_items(base, count):
        for q in range(count):
            nb, ns, nd = item(base + q + 1)
            stage_a(nb, ns, q % 2, nd)
            cb, cs, cd = item(base + q)
            stage_b(cb, cs, (q + 1) % 2, cd)

    stage_a(q_blocks[0], 0, 0, True)
    nb, ns, nd = item(1)
    stage_a(nb, ns, 1, nd)
    stage_b(q_blocks[0], 0, 0, True)
    run_items(1, UNROLL)

    def trip(t, carry):
        run_items(1 + UNROLL * t, UNROLL)
        return carry

    lax.fori_loop(1, (n_items - 2) // UNROLL, trip, 0)
    b_last, s_last, _ = item(n_items - 1)
    stage_b(b_last, s_last, (n_items - 1) % 2, False)

    lam = (jnp.exp(jnp.sum(lq1_ref[...] * lk1_ref[...], axis=1, keepdims=True))
           - jnp.exp(jnp.sum(lq2_ref[...] * lk2_ref[...], axis=1, keepdims=True))
           + lam_init)
    for sel, o_ref in enumerate(o_refs):
        o = (acc_ref[sel, 0, :HEAD_W, :] / acc_ref[sel, 0, HEAD_W:HEAD_W + 1, :]
             - lam * (acc_ref[sel, 1, :HEAD_W, :] / acc_ref[sel, 1, HEAD_W:HEAD_W + 1, :]))
        ms = jnp.mean(o * o, axis=0, keepdims=True)
        o = o * lax.rsqrt(ms + NORM_EPS) * g_ref[...] * (1.0 - lam_init)
        o_ref[...] = o.astype(o_ref.dtype)


def _diff_attention(qk, v_t, lq1, lk1, lq2, lk2, g_col, layer, lam_init, blk):
    l = qk.shape[0]
    a = v_t.shape[0]
    n_heads = a // HEAD_W
    n_qblk = l // blk
    assert NQ % 2 == 0 and UNROLL % 2 == 0 and n_qblk % NQ == 0
    assert UNROLL >= NQ - 1
    assert (NQ + NQ // 2 * (n_qblk - 1) - 2) % UNROLL == 0
    per = n_qblk // NQ
    lam_spec = pl.BlockSpec((None, 1, HEAD_DIM), lambda h, i: (layer, 0, 0))
    out_sds = jax.ShapeDtypeStruct((a, l // NQ), BF16)

    def q_spec(sel):
        return pl.BlockSpec((blk, HEAD_W), lambda h, i: (_query_blocks(i, n_qblk)[sel], h))

    def o_spec(sel):
        return pl.BlockSpec((HEAD_W, blk),
                            lambda h, i: (h, _query_blocks(i, n_qblk)[sel] - sel * per))

    return pl.pallas_call(
        functools.partial(_attn_kernel, blk=blk, n_qblk=n_qblk, lam_init=lam_init),
        grid=(n_heads, per),
        in_specs=[lam_spec, lam_spec, lam_spec, lam_spec] + [q_spec(sel) for sel in range(NQ)] + [
            pl.BlockSpec((l, HEAD_W), lambda h, i: (0, n_heads + h)),
            pl.BlockSpec((HEAD_W, l), lambda h, i: (h, 0)),
            pl.BlockSpec((None, HEAD_W, 1), lambda h, i: (layer, h, 0))],
        out_specs=[o_spec(sel) for sel in range(NQ)],
        out_shape=[out_sds] * NQ,
        scratch_shapes=[pltpu.VMEM((NQ, 2, blk, HEAD_W), BF16),
                        pltpu.VMEM((2, 2, blk, blk), F32),
                        pltpu.VMEM((2, 2, 1, blk), F32),
                        pltpu.VMEM((NQ, 2, 1, blk), F32),
                        pltpu.VMEM((NQ, 2, HEAD_W + ONES_ROWS, blk), F32)],
        compiler_params=_cparams("parallel", "arbitrary"),
        name="diff_attn",
    )(lq1, lk1, lq2, lk2, *([qk] * (NQ + 1)), v_t, g_col)


def _complex_pow(ar, ai, e, n_bits):
    shape = jnp.broadcast_shapes(ar.shape, e.shape)
    pr = jnp.ones(shape, F32)
    pi = jnp.zeros(shape, F32)
    fr, fi = ar, ai
    for b in range(n_bits):
        bit = ((e >> b) & 1) == 1
        nr = pr * fr - pi * fi
        ni = pr * fi + pi * fr
        pr = jnp.where(bit, nr, pr)
        pi = jnp.where(bit, ni, pi)
        if b + 1 < n_bits:
            fr, fi = fr * fr - fi * fi, 2.0 * fr * fi
    return pr, pi


def _gelu_tanh(y):
    k0 = math.sqrt(2.0 / math.pi)
    return 0.5 * y * (1.0 + jnp.tanh(k0 * (y + 0.044715 * (y * y * y))))


def _spread_channels(x, t):
    n_ch = x.shape[1]
    expand = (lax.broadcasted_iota(jnp.int32, (n_ch, n_ch * t), 1) // t
              == lax.broadcasted_iota(jnp.int32, (n_ch, n_ch * t), 0)).astype(BF16)
    out = None
    rest = x
    for _ in range(3):
        piece = rest.astype(BF16)
        rest = rest - piece.astype(F32)
        part = jnp.dot(piece, expand, preferred_element_type=F32)
        out = part if out is None else out + part
    return out


def _dot_split(a, b):
    a_hi = a.astype(BF16)
    b_hi = b.astype(BF16)
    a_lo = (a - a_hi.astype(F32)).astype(BF16)
    b_lo = (b - b_hi.astype(F32)).astype(BF16)
    small = (jnp.dot(a_hi, b_lo, preferred_element_type=F32)
             + jnp.dot(a_lo, b_hi, preferred_element_type=F32))
    return jnp.dot(a_hi, b_hi, preferred_element_type=F32) + small


def _row_to_col(row):
    n = row.shape[1]
    eye = (lax.broadcasted_iota(jnp.int32, (n, n), 0)
           == lax.broadcasted_iota(jnp.int32, (n, n), 1))
    return jnp.sum(jnp.where(eye, jnp.broadcast_to(row, (n, n)), 0.0), axis=1, keepdims=True)


def _ssm_build(prm, m_ref, w_re_ref, w_im_ref, v_ref, at_ref, drep_ref):
    ldt_ref, lr_ref, li_ref, bt_re_ref, bt_im_ref, ct_re_ref, ct_im_ref, d_ref = prm
    t = CHUNK
    n_ch = SSM_GROUP_CH
    w = n_ch * t
    n_bits = t.bit_length() - 1
    dt = jnp.exp(ldt_ref[...])

    lr = lr_ref[...]
    li = li_ref[...]
    mag = jnp.exp(lr * dt)
    ar = mag * jnp.cos(li * dt)
    ai = mag * jnp.sin(li * dt)
    den = lr * lr + li * li
    nr = ar - 1.0
    fr = (nr * lr + ai * li) / den
    fi = (ai * lr - nr * li) / den
    bbt_r = fr * bt_re_ref[...] - fi * bt_im_ref[...]
    bbt_i = fr * bt_im_ref[...] + fi * bt_re_ref[...]

    rev = (t - 1) - lax.broadcasted_iota(jnp.int32, (t, 1), 0)
    pr, pi = _complex_pow(ar, ai, rev, n_bits)
    for c in range(n_ch):
        br = bbt_r[c:c + 1, :]
        bi = bbt_i[c:c + 1, :]
        w_re_ref[c * t:(c + 1) * t, :] = (pr * br - pi * bi).astype(BF16)
        w_im_ref[c * t:(c + 1) * t, :] = (pr * bi + pi * br).astype(BF16)
    at_r, at_i = _complex_pow(ar, ai, jnp.full((1, 1), t, jnp.int32), n_bits + 1)
    at_ref[0] = at_r
    at_ref[1] = at_i

    ar_c = _row_to_col(ar)
    ai_c = _row_to_col(ai)
    tau_tile = lax.broadcasted_iota(jnp.int32, (1, LANES), 1) % t
    qr, qi = _complex_pow(ar_c, ai_c, tau_tile, n_bits)
    qr = jnp.concatenate([qr] * (w // LANES), axis=1)
    qi = jnp.concatenate([qi] * (w // LANES), axis=1)
    c_re = _spread_channels(ct_re_ref[...], t)
    c_im = _spread_channels(ct_im_ref[...], t)
    drep_ref[...] = _spread_channels(jnp.broadcast_to(d_ref[...], (8, n_ch)), t)[0:1, :]
    ca_r = c_re * qr - c_im * qi
    ca_i = c_re * qi + c_im * qr
    z = _dot_split(jnp.concatenate([bbt_r, -bbt_i], axis=1),
                   jnp.concatenate([ca_r, ca_i], axis=0))
    v_ref[0] = (ca_r * ar_c - ca_i * ai_c).astype(BF16)
    v_ref[1] = (ca_r * ai_c + ca_i * ar_c).astype(BF16)

    tau = lax.broadcasted_iota(jnp.int32, (1, w), 1) % t
    s_row = lax.broadcasted_iota(jnp.int32, (t, 1), 0)
    causal = tau >= s_row
    for c in range(n_ch):
        strip = jnp.broadcast_to(z[c:c + 1, :], (t, w))
        shifted = pltpu.roll(strip, 0, 1, stride=1, stride_axis=0)
        m_ref[c * t:(c + 1) * t, :] = jnp.where(causal, shifted, 0.0).astype(BF16)


def _ssm_fold(u_ref, fold_ref, ub_ref, n_chunks):
    t = CHUNK
    n_ch = SSM_GROUP_CH
    pairs = n_chunks // 2
    half_lane = lax.broadcasted_iota(jnp.int32, (pairs, LANES), 1) < t
    for j in range(pairs):
        fold_ref[j * n_ch:(j + 1) * n_ch, :] = u_ref[:, j * LANES:(j + 1) * LANES].astype(F32)
    for a in range(n_ch // 2):
        lo = fold_ref[pl.ds(2 * a, pairs, stride=n_ch), :]
        hi = fold_ref[pl.ds(2 * a + 1, pairs, stride=n_ch), :]
        even = jnp.where(half_lane, lo, pltpu.roll(hi, t, 1))
        odd = jnp.where(half_lane, pltpu.roll(lo, t, 1), hi)
        ub_ref[0:pairs, a * LANES:(a + 1) * LANES] = even.astype(BF16)
        ub_ref[pairs:n_chunks, a * LANES:(a + 1) * LANES] = odd.astype(BF16)


def _ssm_unfold(y, fold_ref, y_ref, n_chunks):
    t = CHUNK
    n_ch = SSM_GROUP_CH
    pairs = n_chunks // 2
    half_lane = lax.broadcasted_iota(jnp.int32, (pairs, LANES), 1) < t
    for a in range(n_ch // 2):
        even = y[0:pairs, a * LANES:(a + 1) * LANES]
        odd = y[pairs:n_chunks, a * LANES:(a + 1) * LANES]
        fold_ref[pl.ds(2 * a, pairs, stride=n_ch), :] = jnp.where(
            half_lane, even, pltpu.roll(odd, t, 1))
        fold_ref[pl.ds(2 * a + 1, pairs, stride=n_ch), :] = jnp.where(
            half_lane, pltpu.roll(even, t, 1), odd)
    for j in range(pairs):
        y_ref[:, j * LANES:(j + 1) * LANES] = fold_ref[j * n_ch:(j + 1) * n_ch, :].astype(y_ref.dtype)


def _ssm_kernel(*refs, n_chunks):
    n_prm = 8
    u_ref = refs[0]
    prm_all = refs[1:1 + n_prm]
    y_ref = refs[1 + n_prm]
    (m_ref, w_re_ref, w_im_ref, v_ref, at_ref, drep_ref,
     s_re_ref, s_im_ref, h_re_ref, h_im_ref, fold_ref, ub_ref, y_scr) = refs[2 + n_prm:]
    n_ch = SSM_GROUP_CH
    pairs = n_chunks // 2
    k = pl.program_id(0)
    n_groups = 2 * pl.num_programs(0)

    def prm_of(group):
        return tuple(r.at[group] for r in prm_all)

    def build(prm, slot):
        _ssm_build(prm, m_ref.at[slot], w_re_ref.at[slot], w_im_ref.at[slot], v_ref.at[slot],
                   at_ref.at[slot], drep_ref.at[slot])

    def data(slot, build_next):
        rows = slice(slot * n_ch, (slot + 1) * n_ch)
        _ssm_fold(u_ref.at[rows, :], fold_ref, ub_ref, n_chunks)
        build_next()
        ub = ub_ref[...]
        y_scr[...] = jnp.dot(ub, m_ref[slot], preferred_element_type=F32)
        s_re_ref[...] = jnp.dot(ub, w_re_ref[slot], preferred_element_type=F32)
        s_im_ref[...] = jnp.dot(ub, w_im_ref[slot], preferred_element_type=F32)
        at_r = at_ref[slot, 0]
        at_i = at_ref[slot, 1]

        def step(j, carry):
            hr, hi = carry
            for r in (j, pairs + j):
                h_re_ref[pl.ds(r, 1), :] = hr
                h_im_ref[pl.ds(r, 1), :] = hi
                sr = s_re_ref[pl.ds(r, 1), :]
                si = s_im_ref[pl.ds(r, 1), :]
                hr, hi = at_r * hr - at_i * hi + sr, at_r * hi + at_i * hr + si
            return hr, hi

        zero = jnp.zeros((1, SSM_STATE), F32)
        lax.fori_loop(0, pairs, step, (zero, zero))

        y = y_scr[...] + jnp.dot(h_re_ref[...].astype(BF16), v_ref[slot, 0],
                                 preferred_element_type=F32)
        y = y - jnp.dot(h_im_ref[...].astype(BF16), v_ref[slot, 1], preferred_element_type=F32)
        y = _gelu_tanh(y + drep_ref[slot] * ub_ref[...].astype(F32))
        _ssm_unfold(y, fold_ref, y_ref.at[rows, :], n_chunks)

    @pl.when(k == 0)
    def _():
        build(prm_of(0), 0)

    data(0, lambda: build(prm_of(2 * k + 1), 1))
    data(1, lambda: build(prm_of(jnp.minimum(2 * k + 2, n_groups - 1)), 0))


def _ssm_core(u_t, params, layer):
    width, l = u_t.shape
    p = SSM_STATE
    c = SSM_GROUP_CH
    g = width // c
    assert 2 * CHUNK == LANES and l % (16 * LANES) == 0
    assert g % 2 == 0
    n_chunks = l // CHUNK
    w = c * CHUNK
    shapes = [(1, 1), (1, p), (1, p), (c, p), (c, p), (p, c), (p, c), (1, c)]

    pspecs = [pl.BlockSpec((None, g) + shp, lambda k: (layer, 0, 0, 0)) for shp in shapes]
    data = pl.BlockSpec((2 * c, l), lambda k: (k, 0))
    return pl.pallas_call(
        functools.partial(_ssm_kernel, n_chunks=n_chunks),
        grid=(g // 2,),
        in_specs=[data] + pspecs,
        out_specs=data,
        out_shape=jax.ShapeDtypeStruct((width, l), BF16),
        scratch_shapes=[pltpu.VMEM((2, w, w), BF16),
                        pltpu.VMEM((2, w, p), BF16), pltpu.VMEM((2, w, p), BF16),
                        pltpu.VMEM((2, 2, p, w), BF16),
                        pltpu.VMEM((2, 2, 1, p), F32),
                        pltpu.VMEM((2, 1, w), F32),
                        pltpu.VMEM((n_chunks, p), F32), pltpu.VMEM((n_chunks, p), F32),
                        pltpu.VMEM((n_chunks, p), F32), pltpu.VMEM((n_chunks, p), F32),
                        pltpu.VMEM((l // LANES * c, LANES), F32),
                        pltpu.VMEM((n_chunks, w), BF16),
                        pltpu.VMEM((n_chunks, w), F32)],
        compiler_params=_cparams("arbitrary"),
        name="ssm_core",
    )(u_t, *params)


def _layer_norm(y, g, b):
    mu = jnp.mean(y, axis=-1, keepdims=True)
    yc = y - mu
    var = jnp.mean(yc * yc, axis=-1, keepdims=True)
    return yc * lax.rsqrt(var + NORM_EPS) * g + b


def _glu_group_norm(yb, w, b_col, g_col):
    tn_dims = (((0,), (0,)), ((), ()))
    y = yb.astype(F32)
    z = lax.dot_general(w, yb, tn_dims, preferred_element_type=F32) + b_col
    o = y * (1.0 / (1.0 + jnp.exp(-z)))
    width, tl = o.shape
    o3 = o.reshape(width // SSM_GROUP_CH, SSM_GROUP_CH, tl)
    ms = jnp.mean(o3 * o3, axis=1, keepdims=True)
    o3 = o3 * lax.rsqrt(ms + NORM_EPS)
    return o3.reshape(width, tl) * g_col


def _out_proj_kernel(*refs):
    a_refs = refs[:NQ]
    (y_ref, glu_w_ref, glu_b_ref, sg_ref, wa_ref, ws_ref, x_ref, g_ref, b_ref,
     o_ref, ob_ref) = refs[NQ:]
    tn_dims = (((0,), (0,)), ((), ()))
    piece = pl.program_id(0) // (pl.num_programs(0) // NQ)
    attn = a_refs[NQ - 1][...]
    for c in range(NQ - 2, -1, -1):
        attn = jnp.where(piece == c, a_refs[c][...], attn)
    mix = lax.dot_general(attn, wa_ref[...], tn_dims, preferred_element_type=F32)
    ssm = _glu_group_norm(y_ref[...], glu_w_ref[...], glu_b_ref[...], sg_ref[...]).astype(BF16)
    mix = mix + lax.dot_general(ssm, ws_ref[...], tn_dims, preferred_element_type=F32)
    y = _layer_norm(DEEPNORM_ALPHA * x_ref[...] + mix, g_ref[...], b_ref[...])
    o_ref[...] = y
    ob_ref[...] = y.astype(BF16)


def _out_proj_ln(attn_pieces, y_t, glu_w_b, glu_b_col, ssm_g_col, w_out_b, x, g, b, layer, tm):
    l, d = x.shape
    a = attn_pieces[0].shape[0]
    s = y_t.shape[0]
    assert a == s
    per = l // tm // NQ

    def piece_spec(c):
        return pl.BlockSpec((a, tm), lambda i: (0, jnp.clip(i - c * per, 0, per - 1)))

    row = pl.BlockSpec((None, 1, d), lambda i: (layer, 0, 0))
    col = pl.BlockSpec((None, s, 1), lambda i: (layer, 0, 0))
    return pl.pallas_call(
        _out_proj_kernel,
        grid=(l // tm,),
        in_specs=[piece_spec(c) for c in range(NQ)] + [
                  pl.BlockSpec((s, tm), lambda i: (0, i)),
                  pl.BlockSpec((None, s, s), lambda i: (layer, 0, 0)), col, col,
                  pl.BlockSpec((None, a, d), lambda i: (layer, 0, 0)),
                  pl.BlockSpec((None, s, d), lambda i: (layer, 1, 0)),
                  pl.BlockSpec((tm, d), lambda i: (i, 0)), row, row],
        out_specs=[pl.BlockSpec((tm, d), lambda i: (i, 0)),
                   pl.BlockSpec((tm, d), lambda i: (i, 0))],
        out_shape=[jax.ShapeDtypeStruct((l, d), F32), jax.ShapeDtypeStruct((l, d), BF16)],
        compiler_params=_cparams("parallel"),
        name="out_proj_ln",
    )(*attn_pieces, y_t, glu_w_b, glu_b_col, ssm_g_col, w_out_b, w_out_b, x, g, b)


def _mlp_kernel(xb_ref, x_ref, wu_ref, wd_ref, g_ref, b_ref, o_ref, ob_ref, acc_ref):
    j = pl.program_id(1)

    @pl.when(j == 0)
    def _():
        acc_ref[...] = jnp.zeros(acc_ref.shape, F32)

    h = jnp.dot(xb_ref[...], wu_ref[...], preferred_element_type=F32)
    h = jnp.maximum(h, 0.0)
    h = (h * h).astype(BF16)
    acc_ref[...] += jnp.dot(h, wd_ref[...], preferred_element_type=F32)

    @pl.when(j == pl.num_programs(1) - 1)
    def _():
        y = _layer_norm(DEEPNORM_ALPHA * x_ref[...] + acc_ref[...], g_ref[...], b_ref[...])
        o_ref[...] = y
        ob_ref[...] = y.astype(BF16)


def _mlp_ln(xb, x, w_up_b, w_down_b, g, b, layer, tm, tf):
    l, d = x.shape
    f = w_up_b.shape[2]
    row = pl.BlockSpec((None, 1, d), lambda i, j: (layer, 0, 0))
    return pl.pallas_call(
        _mlp_kernel,
        grid=(l // tm, f // tf),
        in_specs=[pl.BlockSpec((tm, d), lambda i, j: (i, 0)),
                  pl.BlockSpec((tm, d), lambda i, j: (i, 0)),
                  pl.BlockSpec((None, d, tf), lambda i, j: (layer, 0, j)),
                  pl.BlockSpec((None, tf, d), lambda i, j: (layer, j, 0)), row, row],
        out_specs=[pl.BlockSpec((tm, d), lambda i, j: (i, 0)),
                   pl.BlockSpec((tm, d), lambda i, j: (i, 0))],
        out_shape=[jax.ShapeDtypeStruct((l, d), F32), jax.ShapeDtypeStruct((l, d), BF16)],
        scratch_shapes=[pltpu.VMEM((tm, d), F32)],
        compiler_params=_cparams("parallel", "arbitrary"),
        name="mlp_ln",
    )(xb, x, w_up_b, w_down_b, g, b)


def _pick(n, pref):
    while n % pref:
        pref //= 2
    return pref


def kernel(x, w_in, lambda_q1, lambda_k1, lambda_q2, lambda_k2, attn_norm_g, ssm_lambda_re, ssm_lambda_im, ssm_log_dt, ssm_b_re, ssm_b_im, ssm_c_re, ssm_c_im, ssm_d, glu_w, glu_b, ssm_norm_g, w_out, ln1_g, ln1_b, w_up, w_down, ln2_g, ln2_b):
    bsz, seq, d = x.shape
    depth = w_in.shape[0]
    attn_w = attn_norm_g.shape[1]
    ssm_w = ssm_d.shape[1]
    n_groups = ssm_w // SSM_GROUP_CH
    assert bsz == 1 and seq % CHUNK == 0
    n_chunks = seq // CHUNK
    p = SSM_STATE
    c = SSM_GROUP_CH

    w_in_b = w_in.astype(BF16)
    w_vu_t = w_in_b[:, :, 2 * attn_w:].transpose(0, 2, 1)
    w_out_b = w_out.astype(BF16)
    w_up_b = w_up.astype(BF16)
    w_down_b = w_down.astype(BF16)
    glu_w_b = glu_w.astype(BF16)
    lq1 = lambda_q1.reshape(depth, 1, HEAD_DIM)
    lk1 = lambda_k1.reshape(depth, 1, HEAD_DIM)
    lq2 = lambda_q2.reshape(depth, 1, HEAD_DIM)
    lk2 = lambda_k2.reshape(depth, 1, HEAD_DIM)
    attn_g_col = attn_norm_g.reshape(depth, attn_w, 1)
    ldt = ssm_log_dt.reshape(depth, n_groups, 1, 1)
    lam_re = ssm_lambda_re.reshape(depth, n_groups, 1, p)
    lam_im = ssm_lambda_im.reshape(depth, n_groups, 1, p)
    bt_re = ssm_b_re.transpose(0, 1, 3, 2)
    bt_im = ssm_b_im.transpose(0, 1, 3, 2)
    ct_re = ssm_c_re.transpose(0, 1, 3, 2)
    ct_im = ssm_c_im.transpose(0, 1, 3, 2)
    d_skip = ssm_d.reshape(depth, n_groups, 1, c)
    glu_b_col = glu_b.reshape(depth, ssm_w, 1)
    ssm_g_col = ssm_norm_g.reshape(depth, ssm_w, 1)
    ln1g, ln1b = ln1_g.reshape(depth, 1, d), ln1_b.reshape(depth, 1, d)
    ln2g, ln2b = ln2_g.reshape(depth, 1, d), ln2_b.reshape(depth, 1, d)

    blk = _pick(seq, ATTN_BLOCK)
    xf = x.reshape(seq, d)
    xb = xf.astype(BF16)
    for l in range(depth):
        lam_init = 0.8 - 0.6 * math.exp(-0.3 * l)
        qk = _proj_qk(xb, w_in_b, l, 2 * attn_w, attn_w, _pick(seq, 1024), 2 * attn_w)
        v_t, u_t = _proj_vu(w_vu_t, xb, l, attn_w, _pick(seq, 1024))

        attn_pieces = _diff_attention(qk, v_t, lq1, lk1, lq2, lk2, attn_g_col, l, lam_init, blk)

        y_t = _ssm_core(u_t, (ldt, lam_re, lam_im, bt_re, bt_im, ct_re, ct_im, d_skip), l)
        xf, xb = _out_proj_ln(attn_pieces, y_t, glu_w_b, glu_b_col, ssm_g_col, w_out_b, xf,
                              ln1g, ln1b, l, _pick(seq // NQ, 512))
        xf, xb = _mlp_ln(xb, xf, w_up_b, w_down_b, ln2g, ln2b, l, _pick(seq, 512), 1024)
    return xf.reshape(bsz, seq, d)
```

```python
import functools
import math

import jax
import jax.numpy as jnp
from jax import lax
from jax.experimental import pallas as pl
from jax.experimental.pallas import tpu as pltpu

F32 = jnp.float32
BF16 = jnp.bfloat16

DEPTH = 4
HEAD_DIM = 64
HEAD_W = 2 * HEAD_DIM
CHUNK = 64
SSM_GROUP_CH = 16
SSM_STATE = 64
DEEPNORM_ALPHA = (2.0 * DEPTH) ** 0.25
NORM_EPS = 1e-5
MASK_VALUE = -1e30
QK_SCALE_LOG2E = HEAD_DIM ** -0.5 * math.log2(math.e)
ONES_ROWS = 16
UNROLL = 10
NQ = 8
ATTN_BLOCK = 512
LANES = 128
VMEM_LIMIT = 56 * 1024 * 1024


def _cparams(*sem):
    return pltpu.CompilerParams(dimension_semantics=sem, vmem_limit_bytes=VMEM_LIMIT)


def _mm_qk_kernel(a_ref, b_ref, o_ref, *, q_cols):
    acc = jnp.dot(a_ref[...], b_ref[...], preferred_element_type=F32)
    tn = o_ref.shape[1]
    col = pl.program_id(1) * tn + lax.broadcasted_iota(jnp.int32, (1, tn), 1)
    scale = jnp.where(col < q_cols, QK_SCALE_LOG2E, 1.0)
    o_ref[...] = (acc * scale).astype(o_ref.dtype)


def _proj_qk(xb, w_in_b, layer, n_cols, q_cols, tm, tn):
    m, k = xb.shape
    return pl.pallas_call(
        functools.partial(_mm_qk_kernel, q_cols=q_cols),
        grid=(m // tm, n_cols // tn),
        in_specs=[pl.BlockSpec((tm, k), lambda i, j: (i, 0)),
                  pl.BlockSpec((None, k, tn), lambda i, j: (layer, 0, j))],
        out_specs=pl.BlockSpec((tm, tn), lambda i, j: (i, j)),
        out_shape=jax.ShapeDtypeStruct((m, n_cols), BF16),
        compiler_params=_cparams("parallel", "arbitrary"),
        name="proj_qk",
    )(xb, w_in_b)


def _mm_vu_kernel(w_ref, x_ref, v_ref, u_ref):
    nt_dims = (((1,), (1,)), ((), ()))
    rows_v = v_ref.shape[0]
    x = x_ref[...]
    v_ref[...] = lax.dot_general(w_ref[:rows_v, :], x, nt_dims,
                                 preferred_element_type=F32).astype(v_ref.dtype)
    u_ref[...] = lax.dot_general(w_ref[rows_v:, :], x, nt_dims,
                                 preferred_element_type=F32).astype(u_ref.dtype)


def _proj_vu(w_vu_t, xb, layer, rows_v, tl):
    _, r, k = w_vu_t.shape
    l = xb.shape[0]
    rows_u = r - rows_v
    return pl.pallas_call(
        _mm_vu_kernel,
        grid=(l // tl,),
        in_specs=[pl.BlockSpec((None, r, k), lambda i: (layer, 0, 0)),
                  pl.BlockSpec((tl, k), lambda i: (i, 0))],
        out_specs=[pl.BlockSpec((rows_v, tl), lambda i: (0, i)),
                   pl.BlockSpec((rows_u, tl), lambda i: (0, i))],
        out_shape=[jax.ShapeDtypeStruct((rows_v, l), BF16),
                   jax.ShapeDtypeStruct((rows_u, l), BF16)],
        compiler_params=_cparams("parallel"),
        name="proj_vu",
    )(w_vu_t, xb)


def _query_blocks(i, n_qblk):
    per = n_qblk // NQ
    blocks = []
    for j in range(NQ // 2):
        blocks += [2 * j * per + i, (2 * j + 2) * per - 1 - i]
    return tuple(blocks)


def _attn_kernel(*refs, blk, n_qblk, lam_init):
    lq1_ref, lk1_ref, lq2_ref, lk2_ref = refs[:4]
    q_refs = refs[4:4 + NQ]
    k_ref, vt_ref, g_ref = refs[4 + NQ:7 + NQ]
    o_refs = refs[7 + NQ:7 + 2 * NQ]
    qz_ref, s_ref, mb_ref, m_ref, acc_ref = refs[7 + 2 * NQ:]
    i = pl.program_id(1)
    q_blocks = _query_blocks(i, n_qblk)
    starts = [0]
    for b in q_blocks[:-1]:
        starts.append(starts[-1] + b)
    n_items = NQ + NQ // 2 * (n_qblk - 1)
    nt_dims = (((1,), (1,)), ((), ()))

    lane = lax.broadcasted_iota(jnp.int32, (blk, HEAD_W), 1)
    for sel, q_ref in enumerate(q_refs):
        q = q_ref[...]
        zero = jnp.zeros_like(q)
        qz_ref[sel, 0] = jnp.where(lane < HEAD_DIM, q, zero)
        qz_ref[sel, 1] = jnp.where(lane >= HEAD_DIM, q, zero)
    ones = jnp.ones((ONES_ROWS, blk), BF16)

    def stage_a(block, sel, slot, diagonal):
        st = pl.multiple_of(block * blk, blk)
        kb = k_ref[pl.ds(st, blk), :]
        for mp in range(2):
            s = lax.dot_general(kb, qz_ref[sel, mp], nt_dims, preferred_element_type=F32)
            if diagonal:
                key_chunk = lax.broadcasted_iota(jnp.int32, (blk, blk), 0) // CHUNK
                qry_chunk = lax.broadcasted_iota(jnp.int32, (blk, blk), 1) // CHUNK
                s = jnp.where(key_chunk <= qry_chunk, s, MASK_VALUE)
            s_ref[slot, mp] = s
            mb_ref[slot, mp] = jnp.max(s, axis=0, keepdims=True)

    def stage_b(block, sel, slot, first):
        st = pl.multiple_of(block * blk, blk)
        v_ext = jnp.concatenate([vt_ref[:, pl.ds(st, blk)], ones], axis=0)
        for mp in range(2):
            if first:
                m_new = mb_ref[slot, mp]
            else:
                m_old = m_ref[sel, mp]
                m_new = jnp.maximum(m_old, mb_ref[slot, mp])
            p = jnp.exp2(s_ref[slot, mp] - m_new).astype(BF16)
            pv = jnp.dot(v_ext, p, preferred_element_type=F32)
            if first:
                acc_ref[sel, mp] = pv
            else:
                acc_ref[sel, mp] = jnp.exp2(m_old - m_new) * acc_ref[sel, mp] + pv
            m_ref[sel, mp] = m_new

    def item(k):
        if isinstance(k, int) and k < NQ:
            return q_blocks[k], k, True
        idx = k - NQ
        sel = sum((idx >= st).astype(jnp.int32) for st in starts[1:])
        first = starts[NQ - 1]
        for j in range(NQ - 2, -1, -1):
            first = jnp.where(sel == j, starts[j], first)
        return idx - first, sel, False

    def run_items(base, count):
        for q in range(count):
            nb, ns, nd = item(base + q + 1)
            stage_a(nb, ns, q % 2, nd)
            cb, cs, cd = item(base + q)
            stage_b(cb, cs, (q + 1) % 2, cd)

    stage_a(q_blocks[0], 0, 0, True)
    nb, ns, nd = item(1)
    stage_a(nb, ns, 1, nd)
    stage_b(q_blocks[0], 0, 0, True)
    run_items(1, UNROLL)

    def trip(t, carry):
        run_items(1 + UNROLL * t, UNROLL)
        return carry

    lax.fori_loop(1, (n_items - 2) // UNROLL, trip, 0)
    b_last, s_last, _ = item(n_items - 1)
    stage_b(b_last, s_last, (n_items - 1) % 2, False)

    lam = (jnp.exp(jnp.sum(lq1_ref[...] * lk1_ref[...], axis=1, keepdims=True))
           - jnp.exp(jnp.sum(lq2_ref[...] * lk2_ref[...], axis=1, keepdims=True))
           + lam_init)
    for sel, o_ref in enumerate(o_refs):
        o = (acc_ref[sel, 0, :HEAD_W, :] / acc_ref[sel, 0, HEAD_W:HEAD_W + 1, :]
             - lam * (acc_ref[sel, 1, :HEAD_W, :] / acc_ref[sel, 1, HEAD_W:HEAD_W + 1, :]))
        ms = jnp.mean(o * o, axis=0, keepdims=True)
        o = o * lax.rsqrt(ms + NORM_EPS) * g_ref[...] * (1.0 - lam_init)
        o_ref[...] = o.astype(o_ref.dtype)


def _diff_attention(qk, v_t, lq1, lk1, lq2, lk2, g_col, layer, lam_init, blk):
    l = qk.shape[0]
    a = v_t.shape[0]
    n_heads = a // HEAD_W
    n_qblk = l // blk
    assert NQ % 2 == 0 and UNROLL % 2 == 0 and n_qblk % NQ == 0
    assert UNROLL >= NQ - 1
    assert (NQ + NQ // 2 * (n_qblk - 1) - 2) % UNROLL == 0
    per = n_qblk // NQ
    lam_spec = pl.BlockSpec((None, 1, HEAD_DIM), lambda h, i: (layer, 0, 0))
    out_sds = jax.ShapeDtypeStruct((a, l // NQ), BF16)

    def q_spec(sel):
        return pl.BlockSpec((blk, HEAD_W), lambda h, i: (_query_blocks(i, n_qblk)[sel], h))

    def o_spec(sel):
        return pl.BlockSpec((HEAD_W, blk),
                            lambda h, i: (h, _query_blocks(i, n_qblk)[sel] - sel * per))

    return pl.pallas_call(
        functools.partial(_attn_kernel, blk=blk, n_qblk=n_qblk, lam_init=lam_init),
        grid=(n_heads, per),
        in_specs=[lam_spec, lam_spec, lam_spec, lam_spec] + [q_spec(sel) for sel in range(NQ)] + [
            pl.BlockSpec((l, HEAD_W), lambda h, i: (0, n_heads + h)),
            pl.BlockSpec((HEAD_W, l), lambda h, i: (h, 0)),
            pl.BlockSpec((None, HEAD_W, 1), lambda h, i: (layer, h, 0))],
        out_specs=[o_spec(sel) for sel in range(NQ)],
        out_shape=[out_sds] * NQ,
        scratch_shapes=[pltpu.VMEM((NQ, 2, blk, HEAD_W), BF16),
                        pltpu.VMEM((2, 2, blk, blk), F32),
                        pltpu.VMEM((2, 2, 1, blk), F32),
                        pltpu.VMEM((NQ, 2, 1, blk), F32),
                        pltpu.VMEM((NQ, 2, HEAD_W + ONES_ROWS, blk), F32)],
        compiler_params=_cparams("parallel", "arbitrary"),
        name="diff_attn",
    )(lq1, lk1, lq2, lk2, *([qk] * (NQ + 1)), v_t, g_col)


def _complex_pow(ar, ai, e, n_bits):
    shape = jnp.broadcast_shapes(ar.shape, e.shape)
    pr = jnp.ones(shape, F32)
    pi = jnp.zeros(shape, F32)
    fr, fi = ar, ai
    for b in range(n_bits):
        bit = ((e >> b) & 1) == 1
        nr = pr * fr - pi * fi
        ni = pr * fi + pi * fr
        pr = jnp.where(bit, nr, pr)
        pi = jnp.where(bit, ni, pi)
        if b + 1 < n_bits:
            fr, fi = fr * fr - fi * fi, 2.0 * fr * fi
    return pr, pi


def _gelu_tanh(y):
    k0 = math.sqrt(2.0 / math.pi)
    return 0.5 * y * (1.0 + jnp.tanh(k0 * (y + 0.044715 * (y * y * y))))


def _spread_channels(x, t):
    n_ch = x.shape[1]
    expand = (lax.broadcasted_iota(jnp.int32, (n_ch, n_ch * t), 1) // t
              == lax.broadcasted_iota(jnp.int32, (n_ch, n_ch * t), 0)).astype(BF16)
    out = None
    rest = x
    for _ in range(3):
        piece = rest.astype(BF16)
        rest = rest - piece.astype(F32)
        part = jnp.dot(piece, expand, preferred_element_type=F32)
        out = part if out is None else out + part
    return out


def _dot_split(a, b):
    a_hi = a.astype(BF16)
    b_hi = b.astype(BF16)
    a_lo = (a - a_hi.astype(F32)).astype(BF16)
    b_lo = (b - b_hi.astype(F32)).astype(BF16)
    small = (jnp.dot(a_hi, b_lo, preferred_element_type=F32)
             + jnp.dot(a_lo, b_hi, preferred_element_type=F32))
    return jnp.dot(a_hi, b_hi, preferred_element_type=F32) + small


def _row_to_col(row):
    n = row.shape[1]
    eye = (lax.broadcasted_iota(jnp.int32, (n, n), 0)
           == lax.broadcasted_iota(jnp.int32, (n, n), 1))
    return jnp.sum(jnp.where(eye, jnp.broadcast_to(row, (n, n)), 0.0), axis=1, keepdims=True)


def _ssm_build(prm, m_ref, w_re_ref, w_im_ref, v_ref, at_ref, drep_ref):
    ldt_ref, lr_ref, li_ref, bt_re_ref, bt_im_ref, ct_re_ref, ct_im_ref, d_ref = prm
    t = CHUNK
    n_ch = SSM_GROUP_CH
    w = n_ch * t
    n_bits = t.bit_length() - 1
    dt = jnp.exp(ldt_ref[...])

    lr = lr_ref[...]
    li = li_ref[...]
    mag = jnp.exp(lr * dt)
    ar = mag * jnp.cos(li * dt)
    ai = mag * jnp.sin(li * dt)
    den = lr * lr + li * li
    nr = ar - 1.0
    fr = (nr * lr + ai * li) / den
    fi = (ai * lr - nr * li) / den
    bbt_r = fr * bt_re_ref[...] - fi * bt_im_ref[...]
    bbt_i = fr * bt_im_ref[...] + fi * bt_re_ref[...]

    rev = (t - 1) - lax.broadcasted_iota(jnp.int32, (t, 1), 0)
    pr, pi = _complex_pow(ar, ai, rev, n_bits)
    for c in range(n_ch):
        br = bbt_r[c:c + 1, :]
        bi = bbt_i[c:c + 1, :]
        w_re_ref[c * t:(c + 1) * t, :] = (pr * br - pi * bi).astype(BF16)
        w_im_ref[c * t:(c + 1) * t, :] = (pr * bi + pi * br).astype(BF16)
    at_r, at_i = _complex_pow(ar, ai, jnp.full((1, 1), t, jnp.int32), n_bits + 1)
    at_ref[0] = at_r
    at_ref[1] = at_i

    ar_c = _row_to_col(ar)
    ai_c = _row_to_col(ai)
    tau_tile = lax.broadcasted_iota(jnp.int32, (1, LANES), 1) % t
    qr, qi = _complex_pow(ar_c, ai_c, tau_tile, n_bits)
    qr = jnp.concatenate([qr] * (w // LANES), axis=1)
    qi = jnp.concatenate([qi] * (w // LANES), axis=1)
    c_re = _spread_channels(ct_re_ref[...], t)
    c_im = _spread_channels(ct_im_ref[...], t)
    drep_ref[...] = _spread_channels(jnp.broadcast_to(d_ref[...], (8, n_ch)), t)[0:1, :]
    ca_r = c_re * qr - c_im * qi
    ca_i = c_re * qi + c_im * qr
    z = _dot_split(jnp.concatenate([bbt_r, -bbt_i], axis=1),
                   jnp.concatenate([ca_r, ca_i], axis=0))
    v_ref[0] = (ca_r * ar_c - ca_i * ai_c).astype(BF16)
    v_ref[1] = (ca_r * ai_c + ca_i * ar_c).astype(BF16)

    tau = lax.broadcasted_iota(jnp.int32, (1, w), 1) % t
    s_row = lax.broadcasted_iota(jnp.int32, (t, 1), 0)
    causal = tau >= s_row
    for c in range(n_ch):
        strip = jnp.broadcast_to(z[c:c + 1, :], (t, w))
        shifted = pltpu.roll(strip, 0, 1, stride=1, stride_axis=0)
        m_ref[c * t:(c + 1) * t, :] = jnp.where(causal, shifted, 0.0).astype(BF16)


def _ssm_fold(u_ref, fold_ref, ub_ref, n_chunks):
    t = CHUNK
    n_ch = SSM_GROUP_CH
    pairs = n_chunks // 2
    half_lane = lax.broadcasted_iota(jnp.int32, (pairs, LANES), 1) < t
    for j in range(pairs):
        fold_ref[j * n_ch:(j + 1) * n_ch, :] = u_ref[:, j * LANES:(j + 1) * LANES].astype(F32)
    for a in range(n_ch // 2):
        lo = fold_ref[pl.ds(2 * a, pairs, stride=n_ch), :]
        hi = fold_ref[pl.ds(2 * a + 1, pairs, stride=n_ch), :]
        even = jnp.where(half_lane, lo, pltpu.roll(hi, t, 1))
        odd = jnp.where(half_lane, pltpu.roll(lo, t, 1), hi)
        ub_ref[0:pairs, a * LANES:(a + 1) * LANES] = even.astype(BF16)
        ub_ref[pairs:n_chunks, a * LANES:(a + 1) * LANES] = odd.astype(BF16)


def _ssm_unfold(y, fold_ref, y_ref, n_chunks):
    t = CHUNK
    n_ch = SSM_GROUP_CH
    pairs = n_chunks // 2
    half_lane = lax.broadcasted_iota(jnp.int32, (pairs, LANES), 1) < t
    for a in range(n_ch // 2):
        even = y[0:pairs, a * LANES:(a + 1) * LANES]
        odd = y[pairs:n_chunks, a * LANES:(a + 1) * LANES]
        fold_ref[pl.ds(2 * a, pairs, stride=n_ch), :] = jnp.where(
            half_lane, even, pltpu.roll(odd, t, 1))
        fold_ref[pl.ds(2 * a + 1, pairs, stride=n_ch), :] = jnp.where(
            half_lane, pltpu.roll(even, t, 1), odd)
    for j in range(pairs):
        y_ref[:, j * LANES:(j + 1) * LANES] = fold_ref[j * n_ch:(j + 1) * n_ch, :].astype(y_ref.dtype)


def _ssm_kernel(*refs, n_chunks):
    n_prm = 8
    u_ref = refs[0]
    prm_all = refs[1:1 + n_prm]
    y_ref = refs[1 + n_prm]
    (m_ref, w_re_ref, w_im_ref, v_ref, at_ref, drep_ref,
     s_re_ref, s_im_ref, h_re_ref, h_im_ref, fold_ref, ub_ref, y_scr) = refs[2 + n_prm:]
    n_ch = SSM_GROUP_CH
    pairs = n_chunks // 2
    k = pl.program_id(0)
    n_groups = 2 * pl.num_programs(0)

    def prm_of(group):
        return tuple(r.at[group] for r in prm_all)

    def build(prm, slot):
        _ssm_build(prm, m_ref.at[slot], w_re_ref.at[slot], w_im_ref.at[slot], v_ref.at[slot],
                   at_ref.at[slot], drep_ref.at[slot])

    def data(slot, build_next):
        rows = slice(slot * n_ch, (slot + 1) * n_ch)
        _ssm_fold(u_ref.at[rows, :], fold_ref, ub_ref, n_chunks)
        build_next()
        ub = ub_ref[...]
        y_scr[...] = jnp.dot(ub, m_ref[slot], preferred_element_type=F32)
        s_re_ref[...] = jnp.dot(ub, w_re_ref[slot], preferred_element_type=F32)
        s_im_ref[...] = jnp.dot(ub, w_im_ref[slot], preferred_element_type=F32)
        at_r = at_ref[slot, 0]
        at_i = at_ref[slot, 1]

        def step(j, carry):
            hr, hi = carry
            for r in (j, pairs + j):
                h_re_ref[pl.ds(r, 1), :] = hr
                h_im_ref[pl.ds(r, 1), :] = hi
                sr = s_re_ref[pl.ds(r, 1), :]
                si = s_im_ref[pl.ds(r, 1), :]
                hr, hi = at_r * hr - at_i * hi + sr, at_r * hi + at_i * hr + si
            return hr, hi

        zero = jnp.zeros((1, SSM_STATE), F32)
        lax.fori_loop(0, pairs, step, (zero, zero))

        y = y_scr[...] + jnp.dot(h_re_ref[...].astype(BF16), v_ref[slot, 0],
                                 preferred_element_type=F32)
        y = y - jnp.dot(h_im_ref[...].astype(BF16), v_ref[slot, 1], preferred_element_type=F32)
        y = _gelu_tanh(y + drep_ref[slot] * ub_ref[...].astype(F32))
        _ssm_unfold(y, fold_ref, y_ref.at[rows, :], n_chunks)

    @pl.when(k == 0)
    def _():
        build(prm_of(0), 0)

    data(0, lambda: build(prm_of(2 * k + 1), 1))
    data(1, lambda: build(prm_of(jnp.minimum(2 * k + 2, n_groups - 1)), 0))


def _ssm_core(u_t, params, layer):
    width, l = u_t.shape
    p = SSM_STATE
    c = SSM_GROUP_CH
    g = width // c
    assert 2 * CHUNK == LANES and l % (16 * LANES) == 0
    assert g % 2 == 0
    n_chunks = l // CHUNK
    w = c * CHUNK
    shapes = [(1, 1), (1, p), (1, p), (c, p), (c, p), (p, c), (p, c), (1, c)]

    pspecs = [pl.BlockSpec((None, g) + shp, lambda k: (layer, 0, 0, 0)) for shp in shapes]
    data = pl.BlockSpec((2 * c, l), lambda k: (k, 0))
    return pl.pallas_call(
        functools.partial(_ssm_kernel, n_chunks=n_chunks),
        grid=(g // 2,),
        in_specs=[data] + pspecs,
        out_specs=data,
        out_shape=jax.ShapeDtypeStruct((width, l), BF16),
        scratch_shapes=[pltpu.VMEM((2, w, w), BF16),
                        pltpu.VMEM((2, w, p), BF16), pltpu.VMEM((2, w, p), BF16),
                        pltpu.VMEM((2, 2, p, w), BF16),
                        pltpu.VMEM((2, 2, 1, p), F32),
                        pltpu.VMEM((2, 1, w), F32),
                        pltpu.VMEM((n_chunks, p), F32), pltpu.VMEM((n_chunks, p), F32),
                        pltpu.VMEM((n_chunks, p), F32), pltpu.VMEM((n_chunks, p), F32),
                        pltpu.VMEM((l // LANES * c, LANES), F32),
                        pltpu.VMEM((n_chunks, w), BF16),
                        pltpu.VMEM((n_chunks, w), F32)],
        compiler_params=_cparams("arbitrary"),
        name="ssm_core",
    )(u_t, *params)


def _layer_norm(y, g, b):
    mu = jnp.mean(y, axis=-1, keepdims=True)
    yc = y - mu
    var = jnp.mean(yc * yc, axis=-1, keepdims=True)
    return yc * lax.rsqrt(var + NORM_EPS) * g + b


def _glu_group_norm(yb, w, b_col, g_col):
    tn_dims = (((0,), (0,)), ((), ()))
    y = yb.astype(F32)
    z = lax.dot_general(w, yb, tn_dims, preferred_element_type=F32) + b_col
    o = y * (1.0 / (1.0 + jnp.exp(-z)))
    width, tl = o.shape
    o3 = o.reshape(width // SSM_GROUP_CH, SSM_GROUP_CH, tl)
    ms = jnp.mean(o3 * o3, axis=1, keepdims=True)
    o3 = o3 * lax.rsqrt(ms + NORM_EPS)
    return o3.reshape(width, tl) * g_col


def _out_proj_kernel(*refs):
    a_refs = refs[:NQ]
    (y_ref, glu_w_ref, glu_b_ref, sg_ref, wa_ref, ws_ref, x_ref, g_ref, b_ref,
     o_ref, ob_ref) = refs[NQ:]
    tn_dims = (((0,), (0,)), ((), ()))
    piece = pl.program_id(0) // (pl.num_programs(0) // NQ)
    attn = a_refs[NQ - 1][...]
    for c in range(NQ - 2, -1, -1):
        attn = jnp.where(piece == c, a_refs[c][...], attn)
    mix = lax.dot_general(attn, wa_ref[...], tn_dims, preferred_element_type=F32)
    ssm = _glu_group_norm(y_ref[...], glu_w_ref[...], glu_b_ref[...], sg_ref[...]).astype(BF16)
    mix = mix + lax.dot_general(ssm, ws_ref[...], tn_dims, preferred_element_type=F32)
    y = _layer_norm(DEEPNORM_ALPHA * x_ref[...] + mix, g_ref[...], b_ref[...])
    o_ref[...] = y
    ob_ref[...] = y.astype(BF16)


def _out_proj_ln(attn_pieces, y_t, glu_w_b, glu_b_col, ssm_g_col, w_out_b, x, g, b, layer, tm):
    l, d = x.shape
    a = attn_pieces[0].shape[0]
    s = y_t.shape[0]
    assert a == s
    per = l // tm // NQ

    def piece_spec(c):
        return pl.BlockSpec((a, tm), lambda i: (0, jnp.clip(i - c * per, 0, per - 1)))

    row = pl.BlockSpec((None, 1, d), lambda i: (layer, 0, 0))
    col = pl.BlockSpec((None, s, 1), lambda i: (layer, 0, 0))
    return pl.pallas_call(
        _out_proj_kernel,
        grid=(l // tm,),
        in_specs=[piece_spec(c) for c in range(NQ)] + [
                  pl.BlockSpec((s, tm), lambda i: (0, i)),
                  pl.BlockSpec((None, s, s), lambda i: (layer, 0, 0)), col, col,
                  pl.BlockSpec((None, a, d), lambda i: (layer, 0, 0)),
                  pl.BlockSpec((None, s, d), lambda i: (layer, 1, 0)),
                  pl.BlockSpec((tm, d), lambda i: (i, 0)), row, row],
        out_specs=[pl.BlockSpec((tm, d), lambda i: (i, 0)),
                   pl.BlockSpec((tm, d), lambda i: (i, 0))],
        out_shape=[jax.ShapeDtypeStruct((l, d), F32), jax.ShapeDtypeStruct((l, d), BF16)],
        compiler_params=_cparams("parallel"),
        name="out_proj_ln",
    )(*attn_pieces, y_t, glu_w_b, glu_b_col, ssm_g_col, w_out_b, w_out_b, x, g, b)


def _mlp_kernel(xb_ref, x_ref, wu_ref, wd_ref, g_ref, b_ref, o_ref, ob_ref, acc_ref, *, n_tiles):
    i = pl.program_id(0)
    j = pl.program_id(1)
    n_rows = pl.num_programs(0) - 1
    cur = i % 2
    rows = x_ref.shape[0] // n_tiles

    def norm_slice():
        r0 = pl.multiple_of(j * rows, rows)
        y = _layer_norm(DEEPNORM_ALPHA * x_ref[pl.ds(r0, rows), :]
                        + acc_ref[1 - cur, pl.ds(r0, rows), :], g_ref[...], b_ref[...])
        o_ref[pl.ds(r0, rows), :] = y
        ob_ref[pl.ds(r0, rows), :] = y.astype(BF16)

    @pl.when(jnp.logical_and(i == 0, j == 0))
    def _():
        acc_ref[...] = jnp.zeros(acc_ref.shape, F32)

    @pl.when(i < n_rows)
    def _():
        norm_slice()
        h = jnp.dot(xb_ref[...], wu_ref[...], preferred_element_type=F32)
        h = jnp.maximum(h, 0.0)
        h = (h * h).astype(BF16)
        part = jnp.dot(h, wd_ref[...], preferred_element_type=F32)
        acc_ref[cur] = jnp.where(j == 0, 0.0, acc_ref[cur]) + part

    @pl.when(i == n_rows)
    def _():
        norm_slice()


def _mlp_ln(xb, x, w_up_b, w_down_b, g, b, layer, tm, tf):
    l, d = x.shape
    f = w_up_b.shape[2]
    n_rows = l // tm
    n_tiles = f // tf
    assert tm % (16 * n_tiles) == 0
    row = pl.BlockSpec((None, 1, d), lambda i, j: (layer, 0, 0))

    def lagged(i, j):
        return (jnp.maximum(i - 1, 0), 0)

    return pl.pallas_call(
        functools.partial(_mlp_kernel, n_tiles=n_tiles),
        grid=(n_rows + 1, n_tiles),
        in_specs=[pl.BlockSpec((tm, d), lambda i, j: (jnp.minimum(i, n_rows - 1), 0)),
                  pl.BlockSpec((tm, d), lagged),
                  pl.BlockSpec((None, d, tf), lambda i, j: (layer, 0, j)),
                  pl.BlockSpec((None, tf, d), lambda i, j: (layer, j, 0)), row, row],
        out_specs=[pl.BlockSpec((tm, d), lagged), pl.BlockSpec((tm, d), lagged)],
        out_shape=[jax.ShapeDtypeStruct((l, d), F32), jax.ShapeDtypeStruct((l, d), BF16)],
        scratch_shapes=[pltpu.VMEM((2, tm, d), F32)],
        compiler_params=_cparams("arbitrary", "arbitrary"),
        name="mlp_ln",
    )(xb, x, w_up_b, w_down_b, g, b)


def _pick(n, pref):
    while n % pref:
        pref //= 2
    return pref


def kernel(x, w_in, lambda_q1, lambda_k1, lambda_q2, lambda_k2, attn_norm_g, ssm_lambda_re, ssm_lambda_im, ssm_log_dt, ssm_b_re, ssm_b_im, ssm_c_re, ssm_c_im, ssm_d, glu_w, glu_b, ssm_norm_g, w_out, ln1_g, ln1_b, w_up, w_down, ln2_g, ln2_b):
    bsz, seq, d = x.shape
    depth = w_in.shape[0]
    attn_w = attn_norm_g.shape[1]
    ssm_w = ssm_d.shape[1]
    n_groups = ssm_w // SSM_GROUP_CH
    assert bsz == 1 and seq % CHUNK == 0
    n_chunks = seq // CHUNK
    p = SSM_STATE
    c = SSM_GROUP_CH

    w_in_b = w_in.astype(BF16)
    w_vu_t = w_in_b[:, :, 2 * attn_w:].transpose(0, 2, 1)
    w_out_b = w_out.astype(BF16)
    w_up_b = w_up.astype(BF16)
    w_down_b = w_down.astype(BF16)
    glu_w_b = glu_w.astype(BF16)
    lq1 = lambda_q1.reshape(depth, 1, HEAD_DIM)
    lk1 = lambda_k1.reshape(depth, 1, HEAD_DIM)
    lq2 = lambda_q2.reshape(depth, 1, HEAD_DIM)
    lk2 = lambda_k2.reshape(depth, 1, HEAD_DIM)
    attn_g_col = attn_norm_g.reshape(depth, attn_w, 1)
    ldt = ssm_log_dt.reshape(depth, n_groups, 1, 1)
    lam_re = ssm_lambda_re.reshape(depth, n_groups, 1, p)
    lam_im = ssm_lambda_im.reshape(depth, n_groups, 1, p)
    bt_re = ssm_b_re.transpose(0, 1, 3, 2)
    bt_im = ssm_b_im.transpose(0, 1, 3, 2)
    ct_re = ssm_c_re.transpose(0, 1, 3, 2)
    ct_im = ssm_c_im.transpose(0, 1, 3, 2)
    d_skip = ssm_d.reshape(depth, n_groups, 1, c)
    glu_b_col = glu_b.reshape(depth, ssm_w, 1)
    ssm_g_col = ssm_norm_g.reshape(depth, ssm_w, 1)
    ln1g, ln1b = ln1_g.reshape(depth, 1, d), ln1_b.reshape(depth, 1, d)
    ln2g, ln2b = ln2_g.reshape(depth, 1, d), ln2_b.reshape(depth, 1, d)

    blk = _pick(seq, ATTN_BLOCK)
    xf = x.reshape(seq, d)
    xb = xf.astype(BF16)
    for l in range(depth):
        lam_init = 0.8 - 0.6 * math.exp(-0.3 * l)
        qk = _proj_qk(xb, w_in_b, l, 2 * attn_w, attn_w, _pick(seq, 1024), 2 * attn_w)
        v_t, u_t = _proj_vu(w_vu_t, xb, l, attn_w, _pick(seq, 1024))

        attn_pieces = _diff_attention(qk, v_t, lq1, lk1, lq2, lk2, attn_g_col, l, lam_init, blk)

        y_t = _ssm_core(u_t, (ldt, lam_re, lam_im, bt_re, bt_im, ct_re, ct_im, d_skip), l)
        xf, xb = _out_proj_ln(attn_pieces, y_t, glu_w_b, glu_b_col, ssm_g_col, w_out_b, xf,
                              ln1g, ln1b, l, _pick(seq // NQ, 512))
        xf, xb = _mlp_ln(xb, xf, w_up_b, w_down_b, ln2g, ln2b, l, _pick(seq, 512), 1024)
    return xf.reshape(bsz, seq, d)
```

```python
import functools
import math

import jax
import jax.numpy as jnp
from jax import lax
from jax.experimental import pallas as pl
from jax.experimental.pallas import tpu as pltpu

F32 = jnp.float32
BF16 = jnp.bfloat16

DEPTH = 4
HEAD_DIM = 64
HEAD_W = 2 * HEAD_DIM
CHUNK = 64
SSM_GROUP_CH = 16
SSM_STATE = 64
DEEPNORM_ALPHA = (2.0 * DEPTH) ** 0.25
NORM_EPS = 1e-5
MASK_VALUE = -1e30
QK_SCALE_LOG2E = HEAD_DIM ** -0.5 * math.log2(math.e)
ONES_ROWS = 16
UNROLL = 10
NQ = 8
ATTN_BLOCK = 512
LANES = 128
VMEM_LIMIT = 56 * 1024 * 1024


def _cparams(*sem):
    return pltpu.CompilerParams(dimension_semantics=sem, vmem_limit_bytes=VMEM_LIMIT)


def _mm_qk_kernel(a_ref, b_ref, o_ref, *, q_cols):
    acc = jnp.dot(a_ref[...], b_ref[...], preferred_element_type=F32)
    tn = o_ref.shape[1]
    col = pl.program_id(1) * tn + lax.broadcasted_iota(jnp.int32, (1, tn), 1)
    scale = jnp.where(col < q_cols, QK_SCALE_LOG2E, 1.0)
    o_ref[...] = (acc * scale).astype(o_ref.dtype)


def _proj_qk(xb, w_in_b, layer, n_cols, q_cols, tm, tn):
    m, k = xb.shape
    return pl.pallas_call(
        functools.partial(_mm_qk_kernel, q_cols=q_cols),
        grid=(m // tm, n_cols // tn),
        in_specs=[pl.BlockSpec((tm, k), lambda i, j: (i, 0)),
                  pl.BlockSpec((None, k, tn), lambda i, j: (layer, 0, j))],
        out_specs=pl.BlockSpec((tm, tn), lambda i, j: (i, j)),
        out_shape=jax.ShapeDtypeStruct((m, n_cols), BF16),
        compiler_params=_cparams("parallel", "arbitrary"),
        name="proj_qk",
    )(xb, w_in_b)


def _mm_vu_kernel(w_ref, x_ref, v_ref, u_ref):
    nt_dims = (((1,), (1,)), ((), ()))
    rows_v = v_ref.shape[0]
    x = x_ref[...]
    v_ref[...] = lax.dot_general(w_ref[:rows_v, :], x, nt_dims,
                                 preferred_element_type=F32).astype(v_ref.dtype)
    u_ref[...] = lax.dot_general(w_ref[rows_v:, :], x, nt_dims,
                                 preferred_element_type=F32).astype(u_ref.dtype)


def _proj_vu(w_vu_t, xb, layer, rows_v, tl):
    _, r, k = w_vu_t.shape
    l = xb.shape[0]
    rows_u = r - rows_v
    return pl.pallas_call(
        _mm_vu_kernel,
        grid=(l // tl,),
        in_specs=[pl.BlockSpec((None, r, k), lambda i: (layer, 0, 0)),
                  pl.BlockSpec((tl, k), lambda i: (i, 0))],
        out_specs=[pl.BlockSpec((rows_v, tl), lambda i: (0, i)),
                   pl.BlockSpec((rows_u, tl), lambda i: (0, i))],
        out_shape=[jax.ShapeDtypeStruct((rows_v, l), BF16),
                   jax.ShapeDtypeStruct((rows_u, l), BF16)],
        compiler_params=_cparams("parallel"),
        name="proj_vu",
    )(w_vu_t, xb)


def _query_blocks(i, n_qblk):
    per = n_qblk // NQ
    blocks = []
    for j in range(NQ // 2):
        blocks += [2 * j * per + i, (2 * j + 2) * per - 1 - i]
    return tuple(blocks)


def _attn_kernel(*refs, blk, n_qblk, lam_init, n_cast):
    lq1_ref, lk1_ref, lq2_ref, lk2_ref = refs[:4]
    q_refs = refs[4:4 + NQ]
    k_ref, vt_ref, g_ref = refs[4 + NQ:7 + NQ]
    n_in = 7 + NQ + n_cast
    cast_src = refs[7 + NQ:n_in]
    o_refs = refs[n_in:n_in + NQ]
    cast_dst = refs[n_in + NQ:n_in + NQ + n_cast]
    qz_ref, s_ref, mb_ref, m_ref, acc_ref = refs[n_in + NQ + n_cast:]
    for src, dst in zip(cast_src, cast_dst):
        dst[...] = src[...].astype(dst.dtype)
    i = pl.program_id(1)
    q_blocks = _query_blocks(i, n_qblk)
    starts = [0]
    for b in q_blocks[:-1]:
        starts.append(starts[-1] + b)
    n_items = NQ + NQ // 2 * (n_qblk - 1)
    nt_dims = (((1,), (1,)), ((), ()))

    lane = lax.broadcasted_iota(jnp.int32, (blk, HEAD_W), 1)
    for sel, q_ref in enumerate(q_refs):
        q = q_ref[...]
        zero = jnp.zeros_like(q)
        qz_ref[sel, 0] = jnp.where(lane < HEAD_DIM, q, zero)
        qz_ref[sel, 1] = jnp.where(lane >= HEAD_DIM, q, zero)
    ones = jnp.ones((ONES_ROWS, blk), BF16)

    def stage_a(block, sel, slot, diagonal):
        st = pl.multiple_of(block * blk, blk)
        kb = k_ref[pl.ds(st, blk), :]
        for mp in range(2):
            s = lax.dot_general(kb, qz_ref[sel, mp], nt_dims, preferred_element_type=F32)
            if diagonal:
                key_chunk = lax.broadcasted_iota(jnp.int32, (blk, blk), 0) // CHUNK
                qry_chunk = lax.broadcasted_iota(jnp.int32, (blk, blk), 1) // CHUNK
                s = jnp.where(key_chunk <= qry_chunk, s, MASK_VALUE)
            s_ref[slot, mp] = s
            mb_ref[slot, mp] = jnp.max(s, axis=0, keepdims=True)

    def stage_b(block, sel, slot, first):
        st = pl.multiple_of(block * blk, blk)
        v_ext = jnp.concatenate([vt_ref[:, pl.ds(st, blk)], ones], axis=0)
        for mp in range(2):
            if first:
                m_new = mb_ref[slot, mp]
            else:
                m_old = m_ref[sel, mp]
                m_new = jnp.maximum(m_old, mb_ref[slot, mp])
            p = jnp.exp2(s_ref[slot, mp] - m_new).astype(BF16)
            pv = jnp.dot(v_ext, p, preferred_element_type=F32)
            if first:
                acc_ref[sel, mp] = pv
            else:
                acc_ref[sel, mp] = jnp.exp2(m_old - m_new) * acc_ref[sel, mp] + pv
            m_ref[sel, mp] = m_new

    def item(k):
        if isinstance(k, int) and k < NQ:
            return q_blocks[k], k, True
        idx = k - NQ
        sel = sum((idx >= st).astype(jnp.int32) for st in starts[1:])
        first = starts[NQ - 1]
        for j in range(NQ - 2, -1, -1):
            first = jnp.where(sel == j, starts[j], first)
        return idx - first, sel, False

    def run_items(base, count):
        for q in range(count):
            nb, ns, nd = item(base + q + 1)
            stage_a(nb, ns, q % 2, nd)
            cb, cs, cd = item(base + q)
            stage_b(cb, cs, (q + 1) % 2, cd)

    stage_a(q_blocks[0], 0, 0, True)
    nb, ns, nd = item(1)
    stage_a(nb, ns, 1, nd)
    stage_b(q_blocks[0], 0, 0, True)
    run_items(1, UNROLL)

    def trip(t, carry):
        run_items(1 + UNROLL * t, UNROLL)
        return carry

    lax.fori_loop(1, (n_items - 2) // UNROLL, trip, 0)
    b_last, s_last, _ = item(n_items - 1)
    stage_b(b_last, s_last, (n_items - 1) % 2, False)

    lam = (jnp.exp(jnp.sum(lq1_ref[...] * lk1_ref[...], axis=1, keepdims=True))
           - jnp.exp(jnp.sum(lq2_ref[...] * lk2_ref[...], axis=1, keepdims=True))
           + lam_init)
    for sel, o_ref in enumerate(o_refs):
        o = (acc_ref[sel, 0, :HEAD_W, :] / acc_ref[sel, 0, HEAD_W:HEAD_W + 1, :]
             - lam * (acc_ref[sel, 1, :HEAD_W, :] / acc_ref[sel, 1, HEAD_W:HEAD_W + 1, :]))
        ms = jnp.mean(o * o, axis=0, keepdims=True)
        o = o * lax.rsqrt(ms + NORM_EPS) * g_ref[...] * (1.0 - lam_init)
        o_ref[...] = o.astype(o_ref.dtype)


def _diff_attention(qk, v_t, lq1, lk1, lq2, lk2, g_col, layer, lam_init, blk, cast_srcs):
    l = qk.shape[0]
    a = v_t.shape[0]
    n_heads = a // HEAD_W
    n_qblk = l // blk
    assert NQ % 2 == 0 and UNROLL % 2 == 0 and n_qblk % NQ == 0
    assert UNROLL >= NQ - 1
    assert (NQ + NQ // 2 * (n_qblk - 1) - 2) % UNROLL == 0
    per = n_qblk // NQ
    lam_spec = pl.BlockSpec((None, 1, HEAD_DIM), lambda h, i: (layer, 0, 0))
    out_sds = jax.ShapeDtypeStruct((a, l // NQ), BF16)

    def q_spec(sel):
        return pl.BlockSpec((blk, HEAD_W), lambda h, i: (_query_blocks(i, n_qblk)[sel], h))

    def o_spec(sel):
        return pl.BlockSpec((HEAD_W, blk),
                            lambda h, i: (h, _query_blocks(i, n_qblk)[sel] - sel * per))

    steps = n_heads * per
    cast_in, cast_out, cast_sds = [], [], []
    for w in cast_srcs:
        _, rows, cols = w.shape
        assert rows % (16 * steps) == 0
        slab = (None, rows // steps, cols)
        cast_in.append(pl.BlockSpec(slab, lambda h, i: (layer, h * per + i, 0)))
        cast_out.append(pl.BlockSpec(slab, lambda h, i: (0, h * per + i, 0)))
        cast_sds.append(jax.ShapeDtypeStruct((1, rows, cols), BF16))

    outs = pl.pallas_call(
        functools.partial(_attn_kernel, blk=blk, n_qblk=n_qblk, lam_init=lam_init,
                          n_cast=len(cast_srcs)),
        grid=(n_heads, per),
        in_specs=[lam_spec, lam_spec, lam_spec, lam_spec] + [q_spec(sel) for sel in range(NQ)] + [
            pl.BlockSpec((l, HEAD_W), lambda h, i: (0, n_heads + h)),
            pl.BlockSpec((HEAD_W, l), lambda h, i: (h, 0)),
            pl.BlockSpec((None, HEAD_W, 1), lambda h, i: (layer, h, 0))] + cast_in,
        out_specs=[o_spec(sel) for sel in range(NQ)] + cast_out,
        out_shape=[out_sds] * NQ + cast_sds,
        scratch_shapes=[pltpu.VMEM((NQ, 2, blk, HEAD_W), BF16),
                        pltpu.VMEM((2, 2, blk, blk), F32),
                        pltpu.VMEM((2, 2, 1, blk), F32),
                        pltpu.VMEM((NQ, 2, 1, blk), F32),
                        pltpu.VMEM((NQ, 2, HEAD_W + ONES_ROWS, blk), F32)],
        compiler_params=_cparams("parallel", "arbitrary"),
        name="diff_attn",
    )(lq1, lk1, lq2, lk2, *([qk] * (NQ + 1)), v_t, g_col, *cast_srcs)
    return outs[:NQ], outs[NQ:]


def _complex_pow(ar, ai, e, n_bits):
    shape = jnp.broadcast_shapes(ar.shape, e.shape)
    pr = jnp.ones(shape, F32)
    pi = jnp.zeros(shape, F32)
    fr, fi = ar, ai
    for b in range(n_bits):
        bit = ((e >> b) & 1) == 1
        nr = pr * fr - pi * fi
        ni = pr * fi + pi * fr
        pr = jnp.where(bit, nr, pr)
        pi = jnp.where(bit, ni, pi)
        if b + 1 < n_bits:
            fr, fi = fr * fr - fi * fi, 2.0 * fr * fi
    return pr, pi


def _gelu_tanh(y):
    k0 = math.sqrt(2.0 / math.pi)
    return 0.5 * y * (1.0 + jnp.tanh(k0 * (y + 0.044715 * (y * y * y))))


def _spread_channels(x, t):
    n_ch = x.shape[1]
    expand = (lax.broadcasted_iota(jnp.int32, (n_ch, n_ch * t), 1) // t
              == lax.broadcasted_iota(jnp.int32, (n_ch, n_ch * t), 0)).astype(BF16)
    out = None
    rest = x
    for _ in range(3):
        piece = rest.astype(BF16)
        rest = rest - piece.astype(F32)
        part = jnp.dot(piece, expand, preferred_element_type=F32)
        out = part if out is None else out + part
    return out


def _dot_split(a, b):
    a_hi = a.astype(BF16)
    b_hi = b.astype(BF16)
    a_lo = (a - a_hi.astype(F32)).astype(BF16)
    b_lo = (b - b_hi.astype(F32)).astype(BF16)
    small = (jnp.dot(a_hi, b_lo, preferred_element_type=F32)
             + jnp.dot(a_lo, b_hi, preferred_element_type=F32))
    return jnp.dot(a_hi, b_hi, preferred_element_type=F32) + small


def _row_to_col(row):
    n = row.shape[1]
    eye = (lax.broadcasted_iota(jnp.int32, (n, n), 0)
           == lax.broadcasted_iota(jnp.int32, (n, n), 1))
    return jnp.sum(jnp.where(eye, jnp.broadcast_to(row, (n, n)), 0.0), axis=1, keepdims=True)


def _ssm_build(prm, m_ref, w_re_ref, w_im_ref, v_ref, at_ref, drep_ref):
    ldt_ref, lr_ref, li_ref, bt_re_ref, bt_im_ref, ct_re_ref, ct_im_ref, d_ref = prm
    t = CHUNK
    n_ch = SSM_GROUP_CH
    w = n_ch * t
    n_bits = t.bit_length() - 1
    dt = jnp.exp(ldt_ref[...])

    lr = lr_ref[...]
    li = li_ref[...]
    mag = jnp.exp(lr * dt)
    ar = mag * jnp.cos(li * dt)
    ai = mag * jnp.sin(li * dt)
    den = lr * lr + li * li
    nr = ar - 1.0
    fr = (nr * lr + ai * li) / den
    fi = (ai * lr - nr * li) / den
    bbt_r = fr * bt_re_ref[...] - fi * bt_im_ref[...]
    bbt_i = fr * bt_im_ref[...] + fi * bt_re_ref[...]

    rev = (t - 1) - lax.broadcasted_iota(jnp.int32, (t, 1), 0)
    pr, pi = _complex_pow(ar, ai, rev, n_bits)
    for c in range(n_ch):
        br = bbt_r[c:c + 1, :]
        bi = bbt_i[c:c + 1, :]
        w_re_ref[c * t:(c + 1) * t, :] = (pr * br - pi * bi).astype(BF16)
        w_im_ref[c * t:(c + 1) * t, :] = (pr * bi + pi * br).astype(BF16)
    at_r, at_i = _complex_pow(ar, ai, jnp.full((1, 1), t, jnp.int32), n_bits + 1)
    at_ref[0] = at_r
    at_ref[1] = at_i

    ar_c = _row_to_col(ar)
    ai_c = _row_to_col(ai)
    tau_tile = lax.broadcasted_iota(jnp.int32, (1, LANES), 1) % t
    qr, qi = _complex_pow(ar_c, ai_c, tau_tile, n_bits)
    qr = jnp.concatenate([qr] * (w // LANES), axis=1)
    qi = jnp.concatenate([qi] * (w // LANES), axis=1)
    c_re = _spread_channels(ct_re_ref[...], t)
    c_im = _spread_channels(ct_im_ref[...], t)
    drep_ref[...] = _spread_channels(jnp.broadcast_to(d_ref[...], (8, n_ch)), t)[0:1, :]
    ca_r = c_re * qr - c_im * qi
    ca_i = c_re * qi + c_im * qr
    z = _dot_split(jnp.concatenate([bbt_r, -bbt_i], axis=1),
                   jnp.concatenate([ca_r, ca_i], axis=0))
    v_ref[0] = (ca_r * ar_c - ca_i * ai_c).astype(BF16)
    v_ref[1] = (ca_r * ai_c + ca_i * ar_c).astype(BF16)

    tau = lax.broadcasted_iota(jnp.int32, (1, w), 1) % t
    s_row = lax.broadcasted_iota(jnp.int32, (t, 1), 0)
    causal = tau >= s_row
    for c in range(n_ch):
        strip = jnp.broadcast_to(z[c:c + 1, :], (t, w))
        shifted = pltpu.roll(strip, 0, 1, stride=1, stride_axis=0)
        m_ref[c * t:(c + 1) * t, :] = jnp.where(causal, shifted, 0.0).astype(BF16)


def _ssm_fold(u_ref, fold_ref, ub_ref, n_chunks):
    t = CHUNK
    n_ch = SSM_GROUP_CH
    pairs = n_chunks // 2
    half_lane = lax.broadcasted_iota(jnp.int32, (pairs, LANES), 1) < t
    for j in range(pairs):
        fold_ref[j * n_ch:(j + 1) * n_ch, :] = u_ref[:, j * LANES:(j + 1) * LANES].astype(F32)
    for a in range(n_ch // 2):
        lo = fold_ref[pl.ds(2 * a, pairs, stride=n_ch), :]
        hi = fold_ref[pl.ds(2 * a + 1, pairs, stride=n_ch), :]
        even = jnp.where(half_lane, lo, pltpu.roll(hi, t, 1))
        odd = jnp.where(half_lane, pltpu.roll(lo, t, 1), hi)
        ub_ref[0:pairs, a * LANES:(a + 1) * LANES] = even.astype(BF16)
        ub_ref[pairs:n_chunks, a * LANES:(a + 1) * LANES] = odd.astype(BF16)


def _ssm_unfold(y, fold_ref, y_ref, n_chunks):
    t = CHUNK
    n_ch = SSM_GROUP_CH
    pairs = n_chunks // 2
    half_lane = lax.broadcasted_iota(jnp.int32, (pairs, LANES), 1) < t
    for a in range(n_ch // 2):
        even = y[0:pairs, a * LANES:(a + 1) * LANES]
        odd = y[pairs:n_chunks, a * LANES:(a + 1) * LANES]
        fold_ref[pl.ds(2 * a, pairs, stride=n_ch), :] = jnp.where(
            half_lane, even, pltpu.roll(odd, t, 1))
        fold_ref[pl.ds(2 * a + 1, pairs, stride=n_ch), :] = jnp.where(
            half_lane, pltpu.roll(even, t, 1), odd)
    for j in range(pairs):
        y_ref[:, j * LANES:(j + 1) * LANES] = fold_ref[j * n_ch:(j + 1) * n_ch, :].astype(y_ref.dtype)


def _ssm_kernel(*refs, n_chunks):
    n_prm = 8
    u_ref = refs[0]
    prm_all = refs[1:1 + n_prm]
    y_ref = refs[1 + n_prm]
    (m_ref, w_re_ref, w_im_ref, v_ref, at_ref, drep_ref,
     s_re_ref, s_im_ref, h_re_ref, h_im_ref, fold_ref, ub_ref, y_scr) = refs[2 + n_prm:]
    n_ch = SSM_GROUP_CH
    pairs = n_chunks // 2
    k = pl.program_id(0)
    n_groups = 2 * pl.num_programs(0)

    def prm_of(group):
        return tuple(r.at[group] for r in prm_all)

    def build(prm, slot):
        _ssm_build(prm, m_ref.at[slot], w_re_ref.at[slot], w_im_ref.at[slot], v_ref.at[slot],
                   at_ref.at[slot], drep_ref.at[slot])

    def data(slot, build_next):
        rows = slice(slot * n_ch, (slot + 1) * n_ch)
        _ssm_fold(u_ref.at[rows, :], fold_ref, ub_ref, n_chunks)
        build_next()
        ub = ub_ref[...]
        y_scr[...] = jnp.dot(ub, m_ref[slot], preferred_element_type=F32)
        s_re_ref[...] = jnp.dot(ub, w_re_ref[slot], preferred_element_type=F32)
        s_im_ref[...] = jnp.dot(ub, w_im_ref[slot], preferred_element_type=F32)
        at_r = at_ref[slot, 0]
        at_i = at_ref[slot, 1]

        def step(j, carry):
            hr, hi = carry
            for r in (j, pairs + j):
                h_re_ref[pl.ds(r, 1), :] = hr
                h_im_ref[pl.ds(r, 1), :] = hi
                sr = s_re_ref[pl.ds(r, 1), :]
                si = s_im_ref[pl.ds(r, 1), :]
                hr, hi = at_r * hr - at_i * hi + sr, at_r * hi + at_i * hr + si
            return hr, hi

        zero = jnp.zeros((1, SSM_STATE), F32)
        lax.fori_loop(0, pairs, step, (zero, zero))

        y = y_scr[...] + jnp.dot(h_re_ref[...].astype(BF16), v_ref[slot, 0],
                                 preferred_element_type=F32)
        y = y - jnp.dot(h_im_ref[...].astype(BF16), v_ref[slot, 1], preferred_element_type=F32)
        y = _gelu_tanh(y + drep_ref[slot] * ub_ref[...].astype(F32))
        _ssm_unfold(y, fold_ref, y_ref.at[rows, :], n_chunks)

    @pl.when(k == 0)
    def _():
        build(prm_of(0), 0)

    data(0, lambda: build(prm_of(2 * k + 1), 1))
    data(1, lambda: build(prm_of(jnp.minimum(2 * k + 2, n_groups - 1)), 0))


def _ssm_core(u_t, params, layer):
    width, l = u_t.shape
    p = SSM_STATE
    c = SSM_GROUP_CH
    g = width // c
    assert 2 * CHUNK == LANES and l % (16 * LANES) == 0
    assert g % 2 == 0
    n_chunks = l // CHUNK
    w = c * CHUNK
    shapes = [(1, 1), (1, p), (1, p), (c, p), (c, p), (p, c), (p, c), (1, c)]

    pspecs = [pl.BlockSpec((None, g) + shp, lambda k: (layer, 0, 0, 0)) for shp in shapes]
    data = pl.BlockSpec((2 * c, l), lambda k: (k, 0))
    return pl.pallas_call(
        functools.partial(_ssm_kernel, n_chunks=n_chunks),
        grid=(g // 2,),
        in_specs=[data] + pspecs,
        out_specs=data,
        out_shape=jax.ShapeDtypeStruct((width, l), BF16),
        scratch_shapes=[pltpu.VMEM((2, w, w), BF16),
                        pltpu.VMEM((2, w, p), BF16), pltpu.VMEM((2, w, p), BF16),
                        pltpu.VMEM((2, 2, p, w), BF16),
                        pltpu.VMEM((2, 2, 1, p), F32),
                        pltpu.VMEM((2, 1, w), F32),
                        pltpu.VMEM((n_chunks, p), F32), pltpu.VMEM((n_chunks, p), F32),
                        pltpu.VMEM((n_chunks, p), F32), pltpu.VMEM((n_chunks, p), F32),
                        pltpu.VMEM((l // LANES * c, LANES), F32),
                        pltpu.VMEM((n_chunks, w), BF16),
                        pltpu.VMEM((n_chunks, w), F32)],
        compiler_params=_cparams("arbitrary"),
        name="ssm_core",
    )(u_t, *params)


def _layer_norm(y, g, b):
    mu = jnp.mean(y, axis=-1, keepdims=True)
    yc = y - mu
    var = jnp.mean(yc * yc, axis=-1, keepdims=True)
    return yc * lax.rsqrt(var + NORM_EPS) * g + b


def _glu_group_norm(yb, w, b_col, g_col):
    tn_dims = (((0,), (0,)), ((), ()))
    y = yb.astype(F32)
    z = lax.dot_general(w, yb, tn_dims, preferred_element_type=F32) + b_col
    o = y * (1.0 / (1.0 + jnp.exp(-z)))
    width, tl = o.shape
    o3 = o.reshape(width // SSM_GROUP_CH, SSM_GROUP_CH, tl)
    ms = jnp.mean(o3 * o3, axis=1, keepdims=True)
    o3 = o3 * lax.rsqrt(ms + NORM_EPS)
    return o3.reshape(width, tl) * g_col


def _out_proj_kernel(*refs):
    a_refs = refs[:NQ]
    (y_ref, glu_w_ref, glu_b_ref, sg_ref, wa_ref, ws_ref, x_ref, g_ref, b_ref,
     o_ref, ob_ref) = refs[NQ:]
    tn_dims = (((0,), (0,)), ((), ()))
    piece = pl.program_id(0) // (pl.num_programs(0) // NQ)
    attn = a_refs[NQ - 1][...]
    for c in range(NQ - 2, -1, -1):
        attn = jnp.where(piece == c, a_refs[c][...], attn)
    mix = lax.dot_general(attn, wa_ref[...], tn_dims, preferred_element_type=F32)
    ssm = _glu_group_norm(y_ref[...], glu_w_ref[...], glu_b_ref[...], sg_ref[...]).astype(BF16)
    mix = mix + lax.dot_general(ssm, ws_ref[...], tn_dims, preferred_element_type=F32)
    y = _layer_norm(DEEPNORM_ALPHA * x_ref[...] + mix, g_ref[...], b_ref[...])
    o_ref[...] = y
    ob_ref[...] = y.astype(BF16)


def _out_proj_ln(attn_pieces, y_t, glu_w_b, glu_b_col, ssm_g_col, w_out_b, x, g, b, layer, tm):
    l, d = x.shape
    a = attn_pieces[0].shape[0]
    s = y_t.shape[0]
    assert a == s
    per = l // tm // NQ

    def piece_spec(c):
        return pl.BlockSpec((a, tm), lambda i: (0, jnp.clip(i - c * per, 0, per - 1)))

    row = pl.BlockSpec((None, 1, d), lambda i: (layer, 0, 0))
    col = pl.BlockSpec((None, s, 1), lambda i: (layer, 0, 0))
    return pl.pallas_call(
        _out_proj_kernel,
        grid=(l // tm,),
        in_specs=[piece_spec(c) for c in range(NQ)] + [
                  pl.BlockSpec((s, tm), lambda i: (0, i)),
                  pl.BlockSpec((None, s, s), lambda i: (layer, 0, 0)), col, col,
                  pl.BlockSpec((None, a, d), lambda i: (0, 0, 0)),
                  pl.BlockSpec((None, s, d), lambda i: (0, 1, 0)),
                  pl.BlockSpec((tm, d), lambda i: (i, 0)), row, row],
        out_specs=[pl.BlockSpec((tm, d), lambda i: (i, 0)),
                   pl.BlockSpec((tm, d), lambda i: (i, 0))],
        out_shape=[jax.ShapeDtypeStruct((l, d), F32), jax.ShapeDtypeStruct((l, d), BF16)],
        compiler_params=_cparams("parallel"),
        name="out_proj_ln",
    )(*attn_pieces, y_t, glu_w_b, glu_b_col, ssm_g_col, w_out_b, w_out_b, x, g, b)


def _mlp_kernel(xb_ref, x_ref, wu_ref, wd_ref, g_ref, b_ref, o_ref, ob_ref, acc_ref):
    j = pl.program_id(1)

    @pl.when(j == 0)
    def _():
        acc_ref[...] = jnp.zeros(acc_ref.shape, F32)

    h = jnp.dot(xb_ref[...], wu_ref[...], preferred_element_type=F32)
    h = jnp.maximum(h, 0.0)
    h = (h * h).astype(BF16)
    acc_ref[...] += jnp.dot(h, wd_ref[...], preferred_element_type=F32)

    @pl.when(j == pl.num_programs(1) - 1)
    def _():
        y = _layer_norm(DEEPNORM_ALPHA * x_ref[...] + acc_ref[...], g_ref[...], b_ref[...])
        o_ref[...] = y
        ob_ref[...] = y.astype(BF16)


def _mlp_ln(xb, x, w_up_b, w_down_b, g, b, layer, tm, tf):
    l, d = x.shape
    f = w_up_b.shape[2]
    row = pl.BlockSpec((None, 1, d), lambda i, j: (layer, 0, 0))
    return pl.pallas_call(
        _mlp_kernel,
        grid=(l // tm, f // tf),
        in_specs=[pl.BlockSpec((tm, d), lambda i, j: (i, 0)),
                  pl.BlockSpec((tm, d), lambda i, j: (i, 0)),
                  pl.BlockSpec((None, d, tf), lambda i, j: (0, 0, j)),
                  pl.BlockSpec((None, tf, d), lambda i, j: (0, j, 0)), row, row],
        out_specs=[pl.BlockSpec((tm, d), lambda i, j: (i, 0)),
                   pl.BlockSpec((tm, d), lambda i, j: (i, 0))],
        out_shape=[jax.ShapeDtypeStruct((l, d), F32), jax.ShapeDtypeStruct((l, d), BF16)],
        scratch_shapes=[pltpu.VMEM((tm, d), F32)],
        compiler_params=_cparams("parallel", "arbitrary"),
        name="mlp_ln",
    )(xb, x, w_up_b, w_down_b, g, b)


def _pick(n, pref):
    while n % pref:
        pref //= 2
    return pref


def kernel(x, w_in, lambda_q1, lambda_k1, lambda_q2, lambda_k2, attn_norm_g, ssm_lambda_re, ssm_lambda_im, ssm_log_dt, ssm_b_re, ssm_b_im, ssm_c_re, ssm_c_im, ssm_d, glu_w, glu_b, ssm_norm_g, w_out, ln1_g, ln1_b, w_up, w_down, ln2_g, ln2_b):
    bsz, seq, d = x.shape
    depth = w_in.shape[0]
    attn_w = attn_norm_g.shape[1]
    ssm_w = ssm_d.shape[1]
    n_groups = ssm_w // SSM_GROUP_CH
    assert bsz == 1 and seq % CHUNK == 0
    n_chunks = seq // CHUNK
    p = SSM_STATE
    c = SSM_GROUP_CH

    w_in_b = w_in.astype(BF16)
    w_vu_t = w_in_b[:, :, 2 * attn_w:].transpose(0, 2, 1)
    glu_w_b = glu_w.astype(BF16)
    lq1 = lambda_q1.reshape(depth, 1, HEAD_DIM)
    lk1 = lambda_k1.reshape(depth, 1, HEAD_DIM)
    lq2 = lambda_q2.reshape(depth, 1, HEAD_DIM)
    lk2 = lambda_k2.reshape(depth, 1, HEAD_DIM)
    attn_g_col = attn_norm_g.reshape(depth, attn_w, 1)
    ldt = ssm_log_dt.reshape(depth, n_groups, 1, 1)
    lam_re = ssm_lambda_re.reshape(depth, n_groups, 1, p)
    lam_im = ssm_lambda_im.reshape(depth, n_groups, 1, p)
    bt_re = ssm_b_re.transpose(0, 1, 3, 2)
    bt_im = ssm_b_im.transpose(0, 1, 3, 2)
    ct_re = ssm_c_re.transpose(0, 1, 3, 2)
    ct_im = ssm_c_im.transpose(0, 1, 3, 2)
    d_skip = ssm_d.reshape(depth, n_groups, 1, c)
    glu_b_col = glu_b.reshape(depth, ssm_w, 1)
    ssm_g_col = ssm_norm_g.reshape(depth, ssm_w, 1)
    ln1g, ln1b = ln1_g.reshape(depth, 1, d), ln1_b.reshape(depth, 1, d)
    ln2g, ln2b = ln2_g.reshape(depth, 1, d), ln2_b.reshape(depth, 1, d)

    blk = _pick(seq, ATTN_BLOCK)
    xf = x.reshape(seq, d)
    xb = xf.astype(BF16)
    for l in range(depth):
        lam_init = 0.8 - 0.6 * math.exp(-0.3 * l)
        qk = _proj_qk(xb, w_in_b, l, 2 * attn_w, attn_w, _pick(seq, 1024), 2 * attn_w)
        v_t, u_t = _proj_vu(w_vu_t, xb, l, attn_w, _pick(seq, 1024))

        attn_pieces, (w_out_b, w_up_b, w_down_b) = _diff_attention(
            qk, v_t, lq1, lk1, lq2, lk2, attn_g_col, l, lam_init, blk, (w_out, w_up, w_down))

        y_t = _ssm_core(u_t, (ldt, lam_re, lam_im, bt_re, bt_im, ct_re, ct_im, d_skip), l)
        xf, xb = _out_proj_ln(attn_pieces, y_t, glu_w_b, glu_b_col, ssm_g_col, w_out_b, xf,
                              ln1g, ln1b, l, _pick(seq // NQ, 512))
        xf, xb = _mlp_ln(xb, xf, w_up_b, w_down_b, ln2g, ln2b, l, _pick(seq, 512), 1024)
    return xf.reshape(bsz, seq, d)
```

```python
import functools
import math

import jax
import jax.numpy as jnp
from jax import lax
from jax.experimental import pallas as pl
from jax.experimental.pallas import tpu as pltpu

F32 = jnp.float32
BF16 = jnp.bfloat16

DEPTH = 4
HEAD_DIM = 64
HEAD_W = 2 * HEAD_DIM
CHUNK = 64
SSM_GROUP_CH = 16
SSM_STATE = 64
DEEPNORM_ALPHA = (2.0 * DEPTH) ** 0.25
NORM_EPS = 1e-5
MASK_VALUE = -1e30
QK_SCALE_LOG2E = HEAD_DIM ** -0.5 * math.log2(math.e)
ONES_ROWS = 16
UNROLL = 10
NQ = 8
ATTN_BLOCK = 512
LANES = 128
VMEM_LIMIT = 56 * 1024 * 1024


def _cparams(*sem):
    return pltpu.CompilerParams(dimension_semantics=sem, vmem_limit_bytes=VMEM_LIMIT)


def _w_in_kernel(qk_ref, vu_ref, oqk_ref, ovu_ref):
    oqk_ref[...] = qk_ref[...].astype(oqk_ref.dtype)
    ovu_ref[...] = vu_ref[...].T.astype(ovu_ref.dtype)


def _prep_w_in(w_in, n_qk):
    depth, d, cols = w_in.shape
    n_vu = cols - n_qk
    assert n_vu == n_qk
    tr = _pick(d, 256)
    return pl.pallas_call(
        _w_in_kernel,
        grid=(depth, d // tr),
        in_specs=[pl.BlockSpec((None, tr, n_qk), lambda l, r: (l, r, 0)),
                  pl.BlockSpec((None, tr, n_vu), lambda l, r: (l, r, 1))],
        out_specs=[pl.BlockSpec((None, tr, n_qk), lambda l, r: (l, r, 0)),
                   pl.BlockSpec((None, n_vu, tr), lambda l, r: (l, 0, r))],
        out_shape=[jax.ShapeDtypeStruct((depth, d, n_qk), BF16),
                   jax.ShapeDtypeStruct((depth, n_vu, d), BF16)],
        compiler_params=_cparams("parallel", "parallel"),
        name="prep_w_in",
    )(w_in, w_in)


def _mm_qk_kernel(a_ref, b_ref, o_ref, *, q_cols):
    acc = jnp.dot(a_ref[...], b_ref[...], preferred_element_type=F32)
    tn = o_ref.shape[1]
    col = pl.program_id(1) * tn + lax.broadcasted_iota(jnp.int32, (1, tn), 1)
    scale = jnp.where(col < q_cols, QK_SCALE_LOG2E, 1.0)
    o_ref[...] = (acc * scale).astype(o_ref.dtype)


def _proj_qk(xb, w_in_b, layer, n_cols, q_cols, tm, tn):
    m, k = xb.shape
    return pl.pallas_call(
        functools.partial(_mm_qk_kernel, q_cols=q_cols),
        grid=(m // tm, n_cols // tn),
        in_specs=[pl.BlockSpec((tm, k), lambda i, j: (i, 0)),
                  pl.BlockSpec((None, k, tn), lambda i, j: (layer, 0, j))],
        out_specs=pl.BlockSpec((tm, tn), lambda i, j: (i, j)),
        out_shape=jax.ShapeDtypeStruct((m, n_cols), BF16),
        compiler_params=_cparams("parallel", "arbitrary"),
        name="proj_qk",
    )(xb, w_in_b)


def _mm_vu_kernel(w_ref, x_ref, v_ref, u_ref):
    nt_dims = (((1,), (1,)), ((), ()))
    rows_v = v_ref.shape[0]
    x = x_ref[...]
    v_ref[...] = lax.dot_general(w_ref[:rows_v, :], x, nt_dims,
                                 preferred_element_type=F32).astype(v_ref.dtype)
    u_ref[...] = lax.dot_general(w_ref[rows_v:, :], x, nt_dims,
                                 preferred_element_type=F32).astype(u_ref.dtype)


def _proj_vu(w_vu_t, xb, layer, rows_v, tl):
    _, r, k = w_vu_t.shape
    l = xb.shape[0]
    rows_u = r - rows_v
    return pl.pallas_call(
        _mm_vu_kernel,
        grid=(l // tl,),
        in_specs=[pl.BlockSpec((None, r, k), lambda i: (layer, 0, 0)),
                  pl.BlockSpec((tl, k), lambda i: (i, 0))],
        out_specs=[pl.BlockSpec((rows_v, tl), lambda i: (0, i)),
                   pl.BlockSpec((rows_u, tl), lambda i: (0, i))],
        out_shape=[jax.ShapeDtypeStruct((rows_v, l), BF16),
                   jax.ShapeDtypeStruct((rows_u, l), BF16)],
        compiler_params=_cparams("parallel"),
        name="proj_vu",
    )(w_vu_t, xb)


def _query_blocks(i, n_qblk):
    per = n_qblk // NQ
    blocks = []
    for j in range(NQ // 2):
        blocks += [2 * j * per + i, (2 * j + 2) * per - 1 - i]
    return tuple(blocks)


def _attn_kernel(*refs, blk, n_qblk, lam_init, n_cast):
    lq1_ref, lk1_ref, lq2_ref, lk2_ref = refs[:4]
    q_refs = refs[4:4 + NQ]
    k_ref, vt_ref, g_ref = refs[4 + NQ:7 + NQ]
    n_in = 7 + NQ + n_cast
    cast_src = refs[7 + NQ:n_in]
    o_refs = refs[n_in:n_in + NQ]
    cast_dst = refs[n_in + NQ:n_in + NQ + n_cast]
    qz_ref, s_ref, mb_ref, m_ref, acc_ref = refs[n_in + NQ + n_cast:]
    for src, dst in zip(cast_src, cast_dst):
        dst[...] = src[...].astype(dst.dtype)
    i = pl.program_id(1)
    q_blocks = _query_blocks(i, n_qblk)
    starts = [0]
    for b in q_blocks[:-1]:
        starts.append(starts[-1] + b)
    n_items = NQ + NQ // 2 * (n_qblk - 1)
    nt_dims = (((1,), (1,)), ((), ()))

    lane = lax.broadcasted_iota(jnp.int32, (blk, HEAD_W), 1)
    for sel, q_ref in enumerate(q_refs):
        q = q_ref[...]
        zero = jnp.zeros_like(q)
        qz_ref[sel, 0] = jnp.where(lane < HEAD_DIM, q, zero)
        qz_ref[sel, 1] = jnp.where(lane >= HEAD_DIM, q, zero)
    ones = jnp.ones((ONES_ROWS, blk), BF16)

    def stage_a(block, sel, slot, diagonal):
        st = pl.multiple_of(block * blk, blk)
        kb = k_ref[pl.ds(st, blk), :]
        for mp in range(2):
            s = lax.dot_general(kb, qz_ref[sel, mp], nt_dims, preferred_element_type=F32)
            if diagonal:
                key_chunk = lax.broadcasted_iota(jnp.int32, (blk, blk), 0) // CHUNK
                qry_chunk = lax.broadcasted_iota(jnp.int32, (blk, blk), 1) // CHUNK
                s = jnp.where(key_chunk <= qry_chunk, s, MASK_VALUE)
            s_ref[slot, mp] = s
            mb_ref[slot, mp] = jnp.max(s, axis=0, keepdims=True)

    def stage_b(block, sel, slot, first):
        st = pl.multiple_of(block * blk, blk)
        v_ext = jnp.concatenate([vt_ref[:, pl.ds(st, blk)], ones], axis=0)
        for mp in range(2):
            if first:
                m_new = mb_ref[slot, mp]
            else:
                m_old = m_ref[sel, mp]
                m_new = jnp.maximum(m_old, mb_ref[slot, mp])
            p = jnp.exp2(s_ref[slot, mp] - m_new).astype(BF16)
            pv = jnp.dot(v_ext, p, preferred_element_type=F32)
            if first:
                acc_ref[sel, mp] = pv
            else:
                acc_ref[sel, mp] = jnp.exp2(m_old - m_new) * acc_ref[sel, mp] + pv
            m_ref[sel, mp] = m_new

    def item(k):
        if isinstance(k, int) and k < NQ:
            return q_blocks[k], k, True
        idx = k - NQ
        sel = sum((idx >= st).astype(jnp.int32) for st in starts[1:])
        first = starts[NQ - 1]
        for j in range(NQ - 2, -1, -1):
            first = jnp.where(sel == j, starts[j], first)
        return idx - first, sel, False

    def run_items(base, count):
        for q in range(count):
            nb, ns, nd = item(base + q + 1)
            stage_a(nb, ns, q % 2, nd)
            cb, cs, cd = item(base + q)
            stage_b(cb, cs, (q + 1) % 2, cd)

    stage_a(q_blocks[0], 0, 0, True)
    nb, ns, nd = item(1)
    stage_a(nb, ns, 1, nd)
    stage_b(q_blocks[0], 0, 0, True)
    run_items(1, UNROLL)

    def trip(t, carry):
        run_items(1 + UNROLL * t, UNROLL)
        return carry

    lax.fori_loop(1, (n_items - 2) // UNROLL, trip, 0)
    b_last, s_last, _ = item(n_items - 1)
    stage_b(b_last, s_last, (n_items - 1) % 2, False)

    lam = (jnp.exp(jnp.sum(lq1_ref[...] * lk1_ref[...], axis=1, keepdims=True))
           - jnp.exp(jnp.sum(lq2_ref[...] * lk2_ref[...], axis=1, keepdims=True))
           + lam_init)
    for sel, o_ref in enumerate(o_refs):
        o = (acc_ref[sel, 0, :HEAD_W, :] / acc_ref[sel, 0, HEAD_W:HEAD_W + 1, :]
             - lam * (acc_ref[sel, 1, :HEAD_W, :] / acc_ref[sel, 1, HEAD_W:HEAD_W + 1, :]))
        ms = jnp.mean(o * o, axis=0, keepdims=True)
        o = o * lax.rsqrt(ms + NORM_EPS) * g_ref[...] * (1.0 - lam_init)
        o_ref[...] = o.astype(o_ref.dtype)


def _diff_attention(qk, v_t, lq1, lk1, lq2, lk2, g_col, layer, lam_init, blk, cast_srcs):
    l = qk.shape[0]
    a = v_t.shape[0]
    n_heads = a // HEAD_W
    n_qblk = l // blk
    assert NQ % 2 == 0 and UNROLL % 2 == 0 and n_qblk % NQ == 0
    assert UNROLL >= NQ - 1
    assert (NQ + NQ // 2 * (n_qblk - 1) - 2) % UNROLL == 0
    per = n_qblk // NQ
    lam_spec = pl.BlockSpec((None, 1, HEAD_DIM), lambda h, i: (layer, 0, 0))
    out_sds = jax.ShapeDtypeStruct((a, l // NQ), BF16)

    def q_spec(sel):
        return pl.BlockSpec((blk, HEAD_W), lambda h, i: (_query_blocks(i, n_qblk)[sel], h))

    def o_spec(sel):
        return pl.BlockSpec((HEAD_W, blk),
                            lambda h, i: (h, _query_blocks(i, n_qblk)[sel] - sel * per))

    steps = n_heads * per
    cast_in, cast_out, cast_sds = [], [], []
    for w in cast_srcs:
        _, rows, cols = w.shape
        assert rows % (16 * steps) == 0
        slab = (None, rows // steps, cols)
        cast_in.append(pl.BlockSpec(slab, lambda h, i: (layer, h * per + i, 0)))
        cast_out.append(pl.BlockSpec(slab, lambda h, i: (0, h * per + i, 0)))
        cast_sds.append(jax.ShapeDtypeStruct((1, rows, cols), BF16))

    outs = pl.pallas_call(
        functools.partial(_attn_kernel, blk=blk, n_qblk=n_qblk, lam_init=lam_init,
                          n_cast=len(cast_srcs)),
        grid=(n_heads, per),
        in_specs=[lam_spec, lam_spec, lam_spec, lam_spec] + [q_spec(sel) for sel in range(NQ)] + [
            pl.BlockSpec((l, HEAD_W), lambda h, i: (0, n_heads + h)),
            pl.BlockSpec((HEAD_W, l), lambda h, i: (h, 0)),
            pl.BlockSpec((None, HEAD_W, 1), lambda h, i: (layer, h, 0))] + cast_in,
        out_specs=[o_spec(sel) for sel in range(NQ)] + cast_out,
        out_shape=[out_sds] * NQ + cast_sds,
        scratch_shapes=[pltpu.VMEM((NQ, 2, blk, HEAD_W), BF16),
                        pltpu.VMEM((2, 2, blk, blk), F32),
                        pltpu.VMEM((2, 2, 1, blk), F32),
                        pltpu.VMEM((NQ, 2, 1, blk), F32),
                        pltpu.VMEM((NQ, 2, HEAD_W + ONES_ROWS, blk), F32)],
        compiler_params=_cparams("parallel", "arbitrary"),
        name="diff_attn",
    )(lq1, lk1, lq2, lk2, *([qk] * (NQ + 1)), v_t, g_col, *cast_srcs)
    return outs[:NQ], outs[NQ:]


def _complex_pow(ar, ai, e, n_bits):
    shape = jnp.broadcast_shapes(ar.shape, e.shape)
    pr = jnp.ones(shape, F32)
    pi = jnp.zeros(shape, F32)
    fr, fi = ar, ai
    for b in range(n_bits):
        bit = ((e >> b) & 1) == 1
        nr = pr * fr - pi * fi
        ni = pr * fi + pi * fr
        pr = jnp.where(bit, nr, pr)
        pi = jnp.where(bit, ni, pi)
        if b + 1 < n_bits:
            fr, fi = fr * fr - fi * fi, 2.0 * fr * fi
    return pr, pi


def _gelu_tanh(y):
    k0 = math.sqrt(2.0 / math.pi)
    return 0.5 * y * (1.0 + jnp.tanh(k0 * (y + 0.044715 * (y * y * y))))


def _spread_channels(x, t):
    n_ch = x.shape[1]
    expand = (lax.broadcasted_iota(jnp.int32, (n_ch, n_ch * t), 1) // t
              == lax.broadcasted_iota(jnp.int32, (n_ch, n_ch * t), 0)).astype(BF16)
    out = None
    rest = x
    for _ in range(3):
        piece = rest.astype(BF16)
        rest = rest - piece.astype(F32)
        part = jnp.dot(piece, expand, preferred_element_type=F32)
        out = part if out is None else out + part
    return out


def _dot_split(a, b):
    a_hi = a.astype(BF16)
    b_hi = b.astype(BF16)
    a_lo = (a - a_hi.astype(F32)).astype(BF16)
    b_lo = (b - b_hi.astype(F32)).astype(BF16)
    small = (jnp.dot(a_hi, b_lo, preferred_element_type=F32)
             + jnp.dot(a_lo, b_hi, preferred_element_type=F32))
    return jnp.dot(a_hi, b_hi, preferred_element_type=F32) + small


def _row_to_col(row):
    n = row.shape[1]
    eye = (lax.broadcasted_iota(jnp.int32, (n, n), 0)
           == lax.broadcasted_iota(jnp.int32, (n, n), 1))
    return jnp.sum(jnp.where(eye, jnp.broadcast_to(row, (n, n)), 0.0), axis=1, keepdims=True)


def _ssm_build(prm, m_ref, w_re_ref, w_im_ref, v_ref, at_ref, drep_ref):
    ldt_ref, lr_ref, li_ref, bt_re_ref, bt_im_ref, ct_re_ref, ct_im_ref, d_ref = prm
    t = CHUNK
    n_ch = SSM_GROUP_CH
    w = n_ch * t
    n_bits = t.bit_length() - 1
    dt = jnp.exp(ldt_ref[...])

    lr = lr_ref[...]
    li = li_ref[...]
    mag = jnp.exp(lr * dt)
    ar = mag * jnp.cos(li * dt)
    ai = mag * jnp.sin(li * dt)
    den = lr * lr + li * li
    nr = ar - 1.0
    fr = (nr * lr + ai * li) / den
    fi = (ai * lr - nr * li) / den
    bbt_r = fr * bt_re_ref[...] - fi * bt_im_ref[...]
    bbt_i = fr * bt_im_ref[...] + fi * bt_re_ref[...]

    rev = (t - 1) - lax.broadcasted_iota(jnp.int32, (t, 1), 0)
    pr, pi = _complex_pow(ar, ai, rev, n_bits)
    for c in range(n_ch):
        br = bbt_r[c:c + 1, :]
        bi = bbt_i[c:c + 1, :]
        w_re_ref[c * t:(c + 1) * t, :] = (pr * br - pi * bi).astype(BF16)
        w_im_ref[c * t:(c + 1) * t, :] = (pr * bi + pi * br).astype(BF16)
    at_r, at_i = _complex_pow(ar, ai, jnp.full((1, 1), t, jnp.int32), n_bits + 1)
    at_ref[0] = at_r
    at_ref[1] = at_i

    ar_c = _row_to_col(ar)
    ai_c = _row_to_col(ai)
    tau_tile = lax.broadcasted_iota(jnp.int32, (1, LANES), 1) % t
    qr, qi = _complex_pow(ar_c, ai_c, tau_tile, n_bits)
    qr = jnp.concatenate([qr] * (w // LANES), axis=1)
    qi = jnp.concatenate([qi] * (w // LANES), axis=1)
    c_re = _spread_channels(ct_re_ref[...], t)
    c_im = _spread_channels(ct_im_ref[...], t)
    drep_ref[...] = _spread_channels(jnp.broadcast_to(d_ref[...], (8, n_ch)), t)[0:1, :]
    ca_r = c_re * qr - c_im * qi
    ca_i = c_re * qi + c_im * qr
    z = _dot_split(jnp.concatenate([bbt_r, -bbt_i], axis=1),
                   jnp.concatenate([ca_r, ca_i], axis=0))
    v_ref[0] = (ca_r * ar_c - ca_i * ai_c).astype(BF16)
    v_ref[1] = (ca_r * ai_c + ca_i * ar_c).astype(BF16)

    tau = lax.broadcasted_iota(jnp.int32, (1, w), 1) % t
    s_row = lax.broadcasted_iota(jnp.int32, (t, 1), 0)
    causal = tau >= s_row
    for c in range(n_ch):
        strip = jnp.broadcast_to(z[c:c + 1, :], (t, w))
        shifted = pltpu.roll(strip, 0, 1, stride=1, stride_axis=0)
        m_ref[c * t:(c + 1) * t, :] = jnp.where(causal, shifted, 0.0).astype(BF16)


def _ssm_fold(u_ref, fold_ref, ub_ref, n_chunks):
    t = CHUNK
    n_ch = SSM_GROUP_CH
    pairs = n_chunks // 2
    half_lane = lax.broadcasted_iota(jnp.int32, (pairs, LANES), 1) < t
    for j in range(pairs):
        fold_ref[j * n_ch:(j + 1) * n_ch, :] = u_ref[:, j * LANES:(j + 1) * LANES].astype(F32)
    for a in range(n_ch // 2):
        lo = fold_ref[pl.ds(2 * a, pairs, stride=n_ch), :]
        hi = fold_ref[pl.ds(2 * a + 1, pairs, stride=n_ch), :]
        even = jnp.where(half_lane, lo, pltpu.roll(hi, t, 1))
        odd = jnp.where(half_lane, pltpu.roll(lo, t, 1), hi)
        ub_ref[0:pairs, a * LANES:(a + 1) * LANES] = even.astype(BF16)
        ub_ref[pairs:n_chunks, a * LANES:(a + 1) * LANES] = odd.astype(BF16)


def _ssm_unfold(y, fold_ref, y_ref, n_chunks):
    t = CHUNK
    n_ch = SSM_GROUP_CH
    pairs = n_chunks // 2
    half_lane = lax.broadcasted_iota(jnp.int32, (pairs, LANES), 1) < t
    for a in range(n_ch // 2):
        even = y[0:pairs, a * LANES:(a + 1) * LANES]
        odd = y[pairs:n_chunks, a * LANES:(a + 1) * LANES]
        fold_ref[pl.ds(2 * a, pairs, stride=n_ch), :] = jnp.where(
            half_lane, even, pltpu.roll(odd, t, 1))
        fold_ref[pl.ds(2 * a + 1, pairs, stride=n_ch), :] = jnp.where(
            half_lane, pltpu.roll(even, t, 1), odd)
    for j in range(pairs):
        y_ref[:, j * LANES:(j + 1) * LANES] = fold_ref[j * n_ch:(j + 1) * n_ch, :].astype(y_ref.dtype)


def _ssm_kernel(*refs, n_chunks):
    n_prm = 8
    u_ref = refs[0]
    prm_all = refs[1:1 + n_prm]
    y_ref = refs[1 + n_prm]
    (m_ref, w_re_ref, w_im_ref, v_ref, at_ref, drep_ref,
     s_re_ref, s_im_ref, h_re_ref, h_im_ref, fold_ref, ub_ref, y_scr) = refs[2 + n_prm:]
    n_ch = SSM_GROUP_CH
    pairs = n_chunks // 2
    k = pl.program_id(0)
    n_groups = 2 * pl.num_programs(0)

    def prm_of(group):
        return tuple(r.at[group] for r in prm_all)

    def build(prm, slot):
        _ssm_build(prm, m_ref.at[slot], w_re_ref.at[slot], w_im_ref.at[slot], v_ref.at[slot],
                   at_ref.at[slot], drep_ref.at[slot])

    def data(slot, build_next):
        rows = slice(slot * n_ch, (slot + 1) * n_ch)
        _ssm_fold(u_ref.at[rows, :], fold_ref, ub_ref, n_chunks)
        build_next()
        ub = ub_ref[...]
        y_scr[...] = jnp.dot(ub, m_ref[slot], preferred_element_type=F32)
        s_re_ref[...] = jnp.dot(ub, w_re_ref[slot], preferred_element_type=F32)
        s_im_ref[...] = jnp.dot(ub, w_im_ref[slot], preferred_element_type=F32)
        at_r = at_ref[slot, 0]
        at_i = at_ref[slot, 1]

        def step(j, carry):
            hr, hi = carry
            for r in (j, pairs + j):
                h_re_ref[pl.ds(r, 1), :] = hr
                h_im_ref[pl.ds(r, 1), :] = hi
                sr = s_re_ref[pl.ds(r, 1), :]
                si = s_im_ref[pl.ds(r, 1), :]
                hr, hi = at_r * hr - at_i * hi + sr, at_r * hi + at_i * hr + si
            return hr, hi

        zero = jnp.zeros((1, SSM_STATE), F32)
        lax.fori_loop(0, pairs, step, (zero, zero))

        y = y_scr[...] + jnp.dot(h_re_ref[...].astype(BF16), v_ref[slot, 0],
                                 preferred_element_type=F32)
        y = y - jnp.dot(h_im_ref[...].astype(BF16), v_ref[slot, 1], preferred_element_type=F32)
        y = _gelu_tanh(y + drep_ref[slot] * ub_ref[...].astype(F32))
        _ssm_unfold(y, fold_ref, y_ref.at[rows, :], n_chunks)

    @pl.when(k == 0)
    def _():
        build(prm_of(0), 0)

    data(0, lambda: build(prm_of(2 * k + 1), 1))
    data(1, lambda: build(prm_of(jnp.minimum(2 * k + 2, n_groups - 1)), 0))


def _ssm_core(u_t, params, layer):
    width, l = u_t.shape
    p = SSM_STATE
    c = SSM_GROUP_CH
    g = width // c
    assert 2 * CHUNK == LANES and l % (16 * LANES) == 0
    assert g % 2 == 0
    n_chunks = l // CHUNK
    w = c * CHUNK
    shapes = [(1, 1), (1, p), (1, p), (c, p), (c, p), (p, c), (p, c), (1, c)]

    pspecs = [pl.BlockSpec((None, g) + shp, lambda k: (layer, 0, 0, 0)) for shp in shapes]
    data = pl.BlockSpec((2 * c, l), lambda k: (k, 0))
    return pl.pallas_call(
        functools.partial(_ssm_kernel, n_chunks=n_chunks),
        grid=(g // 2,),
        in_specs=[data] + pspecs,
        out_specs=data,
        out_shape=jax.ShapeDtypeStruct((width, l), BF16),
        scratch_shapes=[pltpu.VMEM((2, w, w), BF16),
                        pltpu.VMEM((2, w, p), BF16), pltpu.VMEM((2, w, p), BF16),
                        pltpu.VMEM((2, 2, p, w), BF16),
                        pltpu.VMEM((2, 2, 1, p), F32),
                        pltpu.VMEM((2, 1, w), F32),
                        pltpu.VMEM((n_chunks, p), F32), pltpu.VMEM((n_chunks, p), F32),
                        pltpu.VMEM((n_chunks, p), F32), pltpu.VMEM((n_chunks, p), F32),
                        pltpu.VMEM((l // LANES * c, LANES), F32),
                        pltpu.VMEM((n_chunks, w), BF16),
                        pltpu.VMEM((n_chunks, w), F32)],
        compiler_params=_cparams("arbitrary"),
        name="ssm_core",
    )(u_t, *params)


def _layer_norm(y, g, b):
    mu = jnp.mean(y, axis=-1, keepdims=True)
    yc = y - mu
    var = jnp.mean(yc * yc, axis=-1, keepdims=True)
    return yc * lax.rsqrt(var + NORM_EPS) * g + b


def _glu_group_norm(yb, w, b_col, g_col):
    tn_dims = (((0,), (0,)), ((), ()))
    y = yb.astype(F32)
    z = lax.dot_general(w, yb, tn_dims, preferred_element_type=F32) + b_col
    o = y * (1.0 / (1.0 + jnp.exp(-z)))
    width, tl = o.shape
    o3 = o.reshape(width // SSM_GROUP_CH, SSM_GROUP_CH, tl)
    ms = jnp.mean(o3 * o3, axis=1, keepdims=True)
    o3 = o3 * lax.rsqrt(ms + NORM_EPS)
    return o3.reshape(width, tl) * g_col


def _out_proj_kernel(*refs):
    a_refs = refs[:NQ]
    (y_ref, glu_w_ref, glu_b_ref, sg_ref, wa_ref, ws_ref, x_ref, g_ref, b_ref,
     o_ref, ob_ref) = refs[NQ:]
    tn_dims = (((0,), (0,)), ((), ()))
    piece = pl.program_id(0) // (pl.num_programs(0) // NQ)
    attn = a_refs[NQ - 1][...]
    for c in range(NQ - 2, -1, -1):
        attn = jnp.where(piece == c, a_refs[c][...], attn)
    mix = lax.dot_general(attn, wa_ref[...], tn_dims, preferred_element_type=F32)
    ssm = _glu_group_norm(y_ref[...], glu_w_ref[...], glu_b_ref[...], sg_ref[...]).astype(BF16)
    mix = mix + lax.dot_general(ssm, ws_ref[...], tn_dims, preferred_element_type=F32)
    y = _layer_norm(DEEPNORM_ALPHA * x_ref[...] + mix, g_ref[...], b_ref[...])
    o_ref[...] = y
    ob_ref[...] = y.astype(BF16)


def _out_proj_ln(attn_pieces, y_t, glu_w_b, glu_b_col, ssm_g_col, w_out_b, x, g, b, layer, tm):
    l, d = x.shape
    a = attn_pieces[0].shape[0]
    s = y_t.shape[0]
    assert a == s
    per = l // tm // NQ

    def piece_spec(c):
        return pl.BlockSpec((a, tm), lambda i: (0, jnp.clip(i - c * per, 0, per - 1)))

    row = pl.BlockSpec((None, 1, d), lambda i: (layer, 0, 0))
    col = pl.BlockSpec((None, s, 1), lambda i: (layer, 0, 0))
    return pl.pallas_call(
        _out_proj_kernel,
        grid=(l // tm,),
        in_specs=[piece_spec(c) for c in range(NQ)] + [
                  pl.BlockSpec((s, tm), lambda i: (0, i)),
                  pl.BlockSpec((None, s, s), lambda i: (layer, 0, 0)), col, col,
                  pl.BlockSpec((None, a, d), lambda i: (0, 0, 0)),
                  pl.BlockSpec((None, s, d), lambda i: (0, 1, 0)),
                  pl.BlockSpec((tm, d), lambda i: (i, 0)), row, row],
        out_specs=[pl.BlockSpec((tm, d), lambda i: (i, 0)),
                   pl.BlockSpec((tm, d), lambda i: (i, 0))],
        out_shape=[jax.ShapeDtypeStruct((l, d), F32), jax.ShapeDtypeStruct((l, d), BF16)],
        compiler_params=_cparams("parallel"),
        name="out_proj_ln",
    )(*attn_pieces, y_t, glu_w_b, glu_b_col, ssm_g_col, w_out_b, w_out_b, x, g, b)


def _mlp_kernel(xb_ref, x_ref, wu_ref, wd_ref, g_ref, b_ref, o_ref, ob_ref, acc_ref):
    j = pl.program_id(1)

    @pl.when(j == 0)
    def _():
        acc_ref[...] = jnp.zeros(acc_ref.shape, F32)

    h = jnp.dot(xb_ref[...], wu_ref[...], preferred_element_type=F32)
    h = jnp.maximum(h, 0.0)
    h = (h * h).astype(BF16)
    acc_ref[...] += jnp.dot(h, wd_ref[...], preferred_element_type=F32)

    @pl.when(j == pl.num_programs(1) - 1)
    def _():
        y = _layer_norm(DEEPNORM_ALPHA * x_ref[...] + acc_ref[...], g_ref[...], b_ref[...])
        o_ref[...] = y
        ob_ref[...] = y.astype(BF16)


def _mlp_ln(xb, x, w_up_b, w_down_b, g, b, layer, tm, tf):
    l, d = x.shape
    f = w_up_b.shape[2]
    row = pl.BlockSpec((None, 1, d), lambda i, j: (layer, 0, 0))
    return pl.pallas_call(
        _mlp_kernel,
        grid=(l // tm, f // tf),
        in_specs=[pl.BlockSpec((tm, d), lambda i, j: (i, 0)),
                  pl.BlockSpec((tm, d), lambda i, j: (i, 0)),
                  pl.BlockSpec((None, d, tf), lambda i, j: (0, 0, j)),
                  pl.BlockSpec((None, tf, d), lambda i, j: (0, j, 0)), row, row],
        out_specs=[pl.BlockSpec((tm, d), lambda i, j: (i, 0)),
                   pl.BlockSpec((tm, d), lambda i, j: (i, 0))],
        out_shape=[jax.ShapeDtypeStruct((l, d), F32), jax.ShapeDtypeStruct((l, d), BF16)],
        scratch_shapes=[pltpu.VMEM((tm, d), F32)],
        compiler_params=_cparams("parallel", "arbitrary"),
        name="mlp_ln",
    )(xb, x, w_up_b, w_down_b, g, b)


def _pick(n, pref):
    while n % pref:
        pref //= 2
    return pref


def kernel(x, w_in, lambda_q1, lambda_k1, lambda_q2, lambda_k2, attn_norm_g, ssm_lambda_re, ssm_lambda_im, ssm_log_dt, ssm_b_re, ssm_b_im, ssm_c_re, ssm_c_im, ssm_d, glu_w, glu_b, ssm_norm_g, w_out, ln1_g, ln1_b, w_up, w_down, ln2_g, ln2_b):
    bsz, seq, d = x.shape
    depth = w_in.shape[0]
    attn_w = attn_norm_g.shape[1]
    ssm_w = ssm_d.shape[1]
    n_groups = ssm_w // SSM_GROUP_CH
    assert bsz == 1 and seq % CHUNK == 0
    n_chunks = seq // CHUNK
    p = SSM_STATE
    c = SSM_GROUP_CH

    w_in_b, w_vu_t = _prep_w_in(w_in, 2 * attn_w)
    glu_w_b = glu_w.astype(BF16)
    lq1 = lambda_q1.reshape(depth, 1, HEAD_DIM)
    lk1 = lambda_k1.reshape(depth, 1, HEAD_DIM)
    lq2 = lambda_q2.reshape(depth, 1, HEAD_DIM)
    lk2 = lambda_k2.reshape(depth, 1, HEAD_DIM)
    attn_g_col = attn_norm_g.reshape(depth, attn_w, 1)
    ldt = ssm_log_dt.reshape(depth, n_groups, 1, 1)
    lam_re = ssm_lambda_re.reshape(depth, n_groups, 1, p)
    lam_im = ssm_lambda_im.reshape(depth, n_groups, 1, p)
    bt_re = ssm_b_re.transpose(0, 1, 3, 2)
    bt_im = ssm_b_im.transpose(0, 1, 3, 2)
    ct_re = ssm_c_re.transpose(0, 1, 3, 2)
    ct_im = ssm_c_im.transpose(0, 1, 3, 2)
    d_skip = ssm_d.reshape(depth, n_groups, 1, c)
    glu_b_col = glu_b.reshape(depth, ssm_w, 1)
    ssm_g_col = ssm_norm_g.reshape(depth, ssm_w, 1)
    ln1g, ln1b = ln1_g.reshape(depth, 1, d), ln1_b.reshape(depth, 1, d)
    ln2g, ln2b = ln2_g.reshape(depth, 1, d), ln2_b.reshape(depth, 1, d)

    blk = _pick(seq, ATTN_BLOCK)
    xf = x.reshape(seq, d)
    xb = xf.astype(BF16)
    for l in range(depth):
        lam_init = 0.8 - 0.6 * math.exp(-0.3 * l)
        qk = _proj_qk(xb, w_in_b, l, 2 * attn_w, attn_w, _pick(seq, 1024), 2 * attn_w)
        v_t, u_t = _proj_vu(w_vu_t, xb, l, attn_w, _pick(seq, 1024))

        attn_pieces, (w_out_b, w_up_b, w_down_b) = _diff_attention(
            qk, v_t, lq1, lk1, lq2, lk2, attn_g_col, l, lam_init, blk, (w_out, w_up, w_down))

        y_t = _ssm_core(u_t, (ldt, lam_re, lam_im, bt_re, bt_im, ct_re, ct_im, d_skip), l)
        xf, xb = _out_proj_ln(attn_pieces, y_t, glu_w_b, glu_b_col, ssm_g_col, w_out_b, xf,
                              ln1g, ln1b, l, _pick(seq // NQ, 512))
        xf, xb = _mlp_ln(xb, xf, w_up_b, w_down_b, ln2g, ln2b, l, _pick(seq, 512), 1024)
    return xf.reshape(bsz, seq, d)
```

```python
import functools
import math

import jax
import jax.numpy as jnp
from jax import lax
from jax.experimental import pallas as pl
from jax.experimental.pallas import tpu as pltpu

F32 = jnp.float32
BF16 = jnp.bfloat16

DEPTH = 4
HEAD_DIM = 64
HEAD_W = 2 * HEAD_DIM
CHUNK = 64
SSM_GROUP_CH = 16
SSM_STATE = 64
DEEPNORM_ALPHA = (2.0 * DEPTH) ** 0.25
NORM_EPS = 1e-5
MASK_VALUE = -1e30
QK_SCALE_LOG2E = HEAD_DIM ** -0.5 * math.log2(math.e)
ONES_ROWS = 16
UNROLL = 10
NQ = 8
ATTN_BLOCK = 512
PROJ_VU_TILE = 1024
LANES = 128
VMEM_LIMIT = 56 * 1024 * 1024


def _cparams(*sem):
    return pltpu.CompilerParams(dimension_semantics=sem, vmem_limit_bytes=VMEM_LIMIT)


def _w_in_kernel(qk_ref, vu_ref, oqk_ref, ovu_ref):
    oqk_ref[...] = qk_ref[...].astype(oqk_ref.dtype)
    ovu_ref[...] = vu_ref[...].T.astype(ovu_ref.dtype)


def _prep_w_in(w_in, n_qk):
    depth, d, cols = w_in.shape
    n_vu = cols - n_qk
    assert n_vu == n_qk
    tr = _pick(d, 256)
    return pl.pallas_call(
        _w_in_kernel,
        grid=(depth, d // tr),
        in_specs=[pl.BlockSpec((None, tr, n_qk), lambda l, r: (l, r, 0)),
                  pl.BlockSpec((None, tr, n_vu), lambda l, r: (l, r, 1))],
        out_specs=[pl.BlockSpec((None, tr, n_qk), lambda l, r: (l, r, 0)),
                   pl.BlockSpec((None, n_vu, tr), lambda l, r: (l, 0, r))],
        out_shape=[jax.ShapeDtypeStruct((depth, d, n_qk), BF16),
                   jax.ShapeDtypeStruct((depth, n_vu, d), BF16)],
        compiler_params=_cparams("parallel", "parallel"),
        name="prep_w_in",
    )(w_in, w_in)


def _mm_qk_kernel(a_ref, b_ref, o_ref, *, q_cols):
    acc = jnp.dot(a_ref[...], b_ref[...], preferred_element_type=F32)
    tn = o_ref.shape[1]
    col = pl.program_id(1) * tn + lax.broadcasted_iota(jnp.int32, (1, tn), 1)
    scale = jnp.where(col < q_cols, QK_SCALE_LOG2E, 1.0)
    o_ref[...] = (acc * scale).astype(o_ref.dtype)


def _proj_qk(xb, w_in_b, layer, n_cols, q_cols, tm, tn):
    m, k = xb.shape
    return pl.pallas_call(
        functools.partial(_mm_qk_kernel, q_cols=q_cols),
        grid=(m // tm, n_cols // tn),
        in_specs=[pl.BlockSpec((tm, k), lambda i, j: (i, 0)),
                  pl.BlockSpec((None, k, tn), lambda i, j: (layer, 0, j))],
        out_specs=pl.BlockSpec((tm, tn), lambda i, j: (i, j)),
        out_shape=jax.ShapeDtypeStruct((m, n_cols), BF16),
        compiler_params=_cparams("parallel", "arbitrary"),
        name="proj_qk",
    )(xb, w_in_b)


def _mm_vu_kernel(*refs, gps):
    w_ref, x_ref = refs[:2]
    prm = refs[2:10]
    v_ref, u_ref = refs[10:12]
    ops = refs[12:]
    for g in range(gps):
        _ssm_build(tuple(r.at[g] for r in prm), *(r.at[g] for r in ops))
    nt_dims = (((1,), (1,)), ((), ()))
    rows_v = v_ref.shape[0]
    x = x_ref[...]
    v_ref[...] = lax.dot_general(w_ref[:rows_v, :], x, nt_dims,
                                 preferred_element_type=F32).astype(v_ref.dtype)
    u_ref[...] = lax.dot_general(w_ref[rows_v:, :], x, nt_dims,
                                 preferred_element_type=F32).astype(u_ref.dtype)


def _proj_vu(w_vu_t, xb, layer, rows_v, tl, ssm_params):
    _, r, k = w_vu_t.shape
    l = xb.shape[0]
    rows_u = r - rows_v
    p = SSM_STATE
    c = SSM_GROUP_CH
    w = c * CHUNK
    n_steps = l // tl
    g = ssm_params[0].shape[1]
    assert g % n_steps == 0
    gps = g // n_steps
    prm_shapes = [(1, 1), (1, p), (1, p), (c, p), (c, p), (p, c), (p, c), (1, c)]
    prm_specs = [pl.BlockSpec((None, gps) + shp, lambda i: (layer, i, 0, 0)) for shp in prm_shapes]
    op_shapes = [((w, w), BF16), ((w, p), BF16), ((w, p), BF16), ((2, p, w), BF16),
                 ((2, 1, p), F32), ((1, w), F32)]
    op_specs = [pl.BlockSpec((gps,) + shp, lambda i, n=len(shp): (i,) + (0,) * n)
                for shp, _ in op_shapes]
    op_sds = [jax.ShapeDtypeStruct((g,) + shp, dt) for shp, dt in op_shapes]
    outs = pl.pallas_call(
        functools.partial(_mm_vu_kernel, gps=gps),
        grid=(n_steps,),
        in_specs=[pl.BlockSpec((None, r, k), lambda i: (layer, 0, 0)),
                  pl.BlockSpec((tl, k), lambda i: (i, 0))] + prm_specs,
        out_specs=[pl.BlockSpec((rows_v, tl), lambda i: (0, i)),
                   pl.BlockSpec((rows_u, tl), lambda i: (0, i))] + op_specs,
        out_shape=[jax.ShapeDtypeStruct((rows_v, l), BF16),
                   jax.ShapeDtypeStruct((rows_u, l), BF16)] + op_sds,
        compiler_params=_cparams("parallel"),
        name="proj_vu",
    )(w_vu_t, xb, *ssm_params)
    return outs[0], outs[1], tuple(outs[2:])


def _query_blocks(i, n_qblk):
    per = n_qblk // NQ
    blocks = []
    for j in range(NQ // 2):
        blocks += [2 * j * per + i, (2 * j + 2) * per - 1 - i]
    return tuple(blocks)


def _attn_kernel(*refs, blk, n_qblk, lam_init, n_cast):
    lq1_ref, lk1_ref, lq2_ref, lk2_ref = refs[:4]
    q_refs = refs[4:4 + NQ]
    k_ref, vt_ref, g_ref = refs[4 + NQ:7 + NQ]
    n_in = 7 + NQ + n_cast
    cast_src = refs[7 + NQ:n_in]
    o_refs = refs[n_in:n_in + NQ]
    cast_dst = refs[n_in + NQ:n_in + NQ + n_cast]
    qz_ref, s_ref, mb_ref, m_ref, acc_ref = refs[n_in + NQ + n_cast:]
    for src, dst in zip(cast_src, cast_dst):
        dst[...] = src[...].astype(dst.dtype)
    i = pl.program_id(1)
    q_blocks = _query_blocks(i, n_qblk)
    starts = [0]
    for b in q_blocks[:-1]:
        starts.append(starts[-1] + b)
    n_items = NQ + NQ // 2 * (n_qblk - 1)
    nt_dims = (((1,), (1,)), ((), ()))

    lane = lax.broadcasted_iota(jnp.int32, (blk, HEAD_W), 1)
    for sel, q_ref in enumerate(q_refs):
        q = q_ref[...]
        zero = jnp.zeros_like(q)
        qz_ref[sel, 0] = jnp.where(lane < HEAD_DIM, q, zero)
        qz_ref[sel, 1] = jnp.where(lane >= HEAD_DIM, q, zero)
    ones = jnp.ones((ONES_ROWS, blk), BF16)

    def stage_a(block, sel, slot, diagonal):
        st = pl.multiple_of(block * blk, blk)
        kb = k_ref[pl.ds(st, blk), :]
        for mp in range(2):
            s = lax.dot_general(kb, qz_ref[sel, mp], nt_dims, preferred_element_type=F32)
            if diagonal:
                key_chunk = lax.broadcasted_iota(jnp.int32, (blk, blk), 0) // CHUNK
                qry_chunk = lax.broadcasted_iota(jnp.int32, (blk, blk), 1) // CHUNK
                s = jnp.where(key_chunk <= qry_chunk, s, MASK_VALUE)
            s_ref[slot, mp] = s
            mb_ref[slot, mp] = jnp.max(s, axis=0, keepdims=True)

    def stage_b(block, sel, slot, first):
        st = pl.multiple_of(block * blk, blk)
        v_ext = jnp.concatenate([vt_ref[:, pl.ds(st, blk)], ones], axis=0)
        for mp in range(2):
            if first:
                m_new = mb_ref[slot, mp]
            else:
                m_old = m_ref[sel, mp]
                m_new = jnp.maximum(m_old, mb_ref[slot, mp])
            p = jnp.exp2(s_ref[slot, mp] - m_new).astype(BF16)
            pv = jnp.dot(v_ext, p, preferred_element_type=F32)
            if first:
                acc_ref[sel, mp] = pv
            else:
                acc_ref[sel, mp] = jnp.exp2(m_old - m_new) * acc_ref[sel, mp] + pv
            m_ref[sel, mp] = m_new

    def item(k):
        if isinstance(k, int) and k < NQ:
            return q_blocks[k], k, True
        idx = k - NQ
        sel = sum((idx >= st).astype(jnp.int32) for st in starts[1:])
        first = starts[NQ - 1]
        for j in range(NQ - 2, -1, -1):
            first = jnp.where(sel == j, starts[j], first)
        return idx - first, sel, False

    def run_items(base, count):
        for q in range(count):
            nb, ns, nd = item(base + q + 1)
            stage_a(nb, ns, q % 2, nd)
            cb, cs, cd = item(base + q)
            stage_b(cb, cs, (q + 1) % 2, cd)

    stage_a(q_blocks[0], 0, 0, True)
    nb, ns, nd = item(1)
    stage_a(nb, ns, 1, nd)
    stage_b(q_blocks[0], 0, 0, True)
    run_items(1, UNROLL)

    def trip(t, carry):
        run_items(1 + UNROLL * t, UNROLL)
        return carry

    lax.fori_loop(1, (n_items - 2) // UNROLL, trip, 0)
    b_last, s_last, _ = item(n_items - 1)
    stage_b(b_last, s_last, (n_items - 1) % 2, False)

    lam = (jnp.exp(jnp.sum(lq1_ref[...] * lk1_ref[...], axis=1, keepdims=True))
           - jnp.exp(jnp.sum(lq2_ref[...] * lk2_ref[...], axis=1, keepdims=True))
           + lam_init)
    for sel, o_ref in enumerate(o_refs):
        o = (acc_ref[sel, 0, :HEAD_W, :] / acc_ref[sel, 0, HEAD_W:HEAD_W + 1, :]
             - lam * (acc_ref[sel, 1, :HEAD_W, :] / acc_ref[sel, 1, HEAD_W:HEAD_W + 1, :]))
        ms = jnp.mean(o * o, axis=0, keepdims=True)
        o = o * lax.rsqrt(ms + NORM_EPS) * g_ref[...] * (1.0 - lam_init)
        o_ref[...] = o.astype(o_ref.dtype)


def _diff_attention(qk, v_t, lq1, lk1, lq2, lk2, g_col, layer, lam_init, blk, cast_srcs):
    l = qk.shape[0]
    a = v_t.shape[0]
    n_heads = a // HEAD_W
    n_qblk = l // blk
    assert NQ % 2 == 0 and UNROLL % 2 == 0 and n_qblk % NQ == 0
    assert UNROLL >= NQ - 1
    assert (NQ + NQ // 2 * (n_qblk - 1) - 2) % UNROLL == 0
    per = n_qblk // NQ
    lam_spec = pl.BlockSpec((None, 1, HEAD_DIM), lambda h, i: (layer, 0, 0))
    out_sds = jax.ShapeDtypeStruct((a, l // NQ), BF16)

    def q_spec(sel):
        return pl.BlockSpec((blk, HEAD_W), lambda h, i: (_query_blocks(i, n_qblk)[sel], h))

    def o_spec(sel):
        return pl.BlockSpec((HEAD_W, blk),
                            lambda h, i: (h, _query_blocks(i, n_qblk)[sel] - sel * per))

    steps = n_heads * per
    cast_in, cast_out, cast_sds = [], [], []
    for w in cast_srcs:
        _, rows, cols = w.shape
        assert rows % (16 * steps) == 0
        slab = (None, rows // steps, cols)
        cast_in.append(pl.BlockSpec(slab, lambda h, i: (layer, h * per + i, 0)))
        cast_out.append(pl.BlockSpec(slab, lambda h, i: (0, h * per + i, 0)))
        cast_sds.append(jax.ShapeDtypeStruct((1, rows, cols), BF16))

    outs = pl.pallas_call(
        functools.partial(_attn_kernel, blk=blk, n_qblk=n_qblk, lam_init=lam_init,
                          n_cast=len(cast_srcs)),
        grid=(n_heads, per),
        in_specs=[lam_spec, lam_spec, lam_spec, lam_spec] + [q_spec(sel) for sel in range(NQ)] + [
            pl.BlockSpec((l, HEAD_W), lambda h, i: (0, n_heads + h)),
            pl.BlockSpec((HEAD_W, l), lambda h, i: (h, 0)),
            pl.BlockSpec((None, HEAD_W, 1), lambda h, i: (layer, h, 0))] + cast_in,
        out_specs=[o_spec(sel) for sel in range(NQ)] + cast_out,
        out_shape=[out_sds] * NQ + cast_sds,
        scratch_shapes=[pltpu.VMEM((NQ, 2, blk, HEAD_W), BF16),
                        pltpu.VMEM((2, 2, blk, blk), F32),
                        pltpu.VMEM((2, 2, 1, blk), F32),
                        pltpu.VMEM((NQ, 2, 1, blk), F32),
                        pltpu.VMEM((NQ, 2, HEAD_W + ONES_ROWS, blk), F32)],
        compiler_params=_cparams("parallel", "arbitrary"),
        name="diff_attn",
    )(lq1, lk1, lq2, lk2, *([qk] * (NQ + 1)), v_t, g_col, *cast_srcs)
    return outs[:NQ], outs[NQ:]


def _complex_pow(ar, ai, e, n_bits):
    shape = jnp.broadcast_shapes(ar.shape, e.shape)
    pr = jnp.ones(shape, F32)
    pi = jnp.zeros(shape, F32)
    fr, fi = ar, ai
    for b in range(n_bits):
        bit = ((e >> b) & 1) == 1
        nr = pr * fr - pi * fi
        ni = pr * fi + pi * fr
        pr = jnp.where(bit, nr, pr)
        pi = jnp.where(bit, ni, pi)
        if b + 1 < n_bits:
            fr, fi = fr * fr - fi * fi, 2.0 * fr * fi
    return pr, pi


def _gelu_tanh(y):
    k0 = math.sqrt(2.0 / math.pi)
    return 0.5 * y * (1.0 + jnp.tanh(k0 * (y + 0.044715 * (y * y * y))))


def _spread_channels(x, t):
    n_ch = x.shape[1]
    expand = (lax.broadcasted_iota(jnp.int32, (n_ch, n_ch * t), 1) // t
              == lax.broadcasted_iota(jnp.int32, (n_ch, n_ch * t), 0)).astype(BF16)
    out = None
    rest = x
    for _ in range(3):
        piece = rest.astype(BF16)
        rest = rest - piece.astype(F32)
        part = jnp.dot(piece, expand, preferred_element_type=F32)
        out = part if out is None else out + part
    return out


def _dot_split(a, b):
    a_hi = a.astype(BF16)
    b_hi = b.astype(BF16)
    a_lo = (a - a_hi.astype(F32)).astype(BF16)
    b_lo = (b - b_hi.astype(F32)).astype(BF16)
    small = (jnp.dot(a_hi, b_lo, preferred_element_type=F32)
             + jnp.dot(a_lo, b_hi, preferred_element_type=F32))
    return jnp.dot(a_hi, b_hi, preferred_element_type=F32) + small


def _row_to_col(row):
    n = row.shape[1]
    eye = (lax.broadcasted_iota(jnp.int32, (n, n), 0)
           == lax.broadcasted_iota(jnp.int32, (n, n), 1))
    return jnp.sum(jnp.where(eye, jnp.broadcast_to(row, (n, n)), 0.0), axis=1, keepdims=True)


def _ssm_build(prm, m_ref, w_re_ref, w_im_ref, v_ref, at_ref, drep_ref):
    ldt_ref, lr_ref, li_ref, bt_re_ref, bt_im_ref, ct_re_ref, ct_im_ref, d_ref = prm
    t = CHUNK
    n_ch = SSM_GROUP_CH
    w = n_ch * t
    n_bits = t.bit_length() - 1
    dt = jnp.exp(ldt_ref[...])

    lr = lr_ref[...]
    li = li_ref[...]
    mag = jnp.exp(lr * dt)
    ar = mag * jnp.cos(li * dt)
    ai = mag * jnp.sin(li * dt)
    den = lr * lr + li * li
    nr = ar - 1.0
    fr = (nr * lr + ai * li) / den
    fi = (ai * lr - nr * li) / den
    bbt_r = fr * bt_re_ref[...] - fi * bt_im_ref[...]
    bbt_i = fr * bt_im_ref[...] + fi * bt_re_ref[...]

    rev = (t - 1) - lax.broadcasted_iota(jnp.int32, (t, 1), 0)
    pr, pi = _complex_pow(ar, ai, rev, n_bits)
    for c in range(n_ch):
        br = bbt_r[c:c + 1, :]
        bi = bbt_i[c:c + 1, :]
        w_re_ref[c * t:(c + 1) * t, :] = (pr * br - pi * bi).astype(BF16)
        w_im_ref[c * t:(c + 1) * t, :] = (pr * bi + pi * br).astype(BF16)
    at_r, at_i = _complex_pow(ar, ai, jnp.full((1, 1), t, jnp.int32), n_bits + 1)
    at_ref[0] = at_r
    at_ref[1] = at_i

    ar_c = _row_to_col(ar)
    ai_c = _row_to_col(ai)
    tau_tile = lax.broadcasted_iota(jnp.int32, (1, LANES), 1) % t
    qr, qi = _complex_pow(ar_c, ai_c, tau_tile, n_bits)
    qr = jnp.concatenate([qr] * (w // LANES), axis=1)
    qi = jnp.concatenate([qi] * (w // LANES), axis=1)
    c_re = _spread_channels(ct_re_ref[...], t)
    c_im = _spread_channels(ct_im_ref[...], t)
    drep_ref[...] = _spread_channels(jnp.broadcast_to(d_ref[...], (8, n_ch)), t)[0:1, :]
    ca_r = c_re * qr - c_im * qi
    ca_i = c_re * qi + c_im * qr
    z = _dot_split(jnp.concatenate([bbt_r, -bbt_i], axis=1),
                   jnp.concatenate([ca_r, ca_i], axis=0))
    v_ref[0] = (ca_r * ar_c - ca_i * ai_c).astype(BF16)
    v_ref[1] = (ca_r * ai_c + ca_i * ar_c).astype(BF16)

    tau = lax.broadcasted_iota(jnp.int32, (1, w), 1) % t
    s_row = lax.broadcasted_iota(jnp.int32, (t, 1), 0)
    causal = tau >= s_row
    for c in range(n_ch):
        strip = jnp.broadcast_to(z[c:c + 1, :], (t, w))
        shifted = pltpu.roll(strip, 0, 1, stride=1, stride_axis=0)
        m_ref[c * t:(c + 1) * t, :] = jnp.where(causal, shifted, 0.0).astype(BF16)


def _ssm_fold(u_ref, fold_ref, ub_ref, n_chunks):
    t = CHUNK
    n_ch = SSM_GROUP_CH
    pairs = n_chunks // 2
    half_lane = lax.broadcasted_iota(jnp.int32, (pairs, LANES), 1) < t
    for j in range(pairs):
        fold_ref[j * n_ch:(j + 1) * n_ch, :] = u_ref[:, j * LANES:(j + 1) * LANES].astype(F32)
    for a in range(n_ch // 2):
        lo = fold_ref[pl.ds(2 * a, pairs, stride=n_ch), :]
        hi = fold_ref[pl.ds(2 * a + 1, pairs, stride=n_ch), :]
        even = jnp.where(half_lane, lo, pltpu.roll(hi, t, 1))
        odd = jnp.where(half_lane, pltpu.roll(lo, t, 1), hi)
        ub_ref[0:pairs, a * LANES:(a + 1) * LANES] = even.astype(BF16)
        ub_ref[pairs:n_chunks, a * LANES:(a + 1) * LANES] = odd.astype(BF16)


def _ssm_unfold(y, fold_ref, y_ref, n_chunks):
    t = CHUNK
    n_ch = SSM_GROUP_CH
    pairs = n_chunks // 2
    half_lane = lax.broadcasted_iota(jnp.int32, (pairs, LANES), 1) < t
    for a in range(n_ch // 2):
        even = y[0:pairs, a * LANES:(a + 1) * LANES]
        odd = y[pairs:n_chunks, a * LANES:(a + 1) * LANES]
        fold_ref[pl.ds(2 * a, pairs, stride=n_ch), :] = jnp.where(
            half_lane, even, pltpu.roll(odd, t, 1))
        fold_ref[pl.ds(2 * a + 1, pairs, stride=n_ch), :] = jnp.where(
            half_lane, pltpu.roll(even, t, 1), odd)
    for j in range(pairs):
        y_ref[:, j * LANES:(j + 1) * LANES] = fold_ref[j * n_ch:(j + 1) * n_ch, :].astype(y_ref.dtype)


def _ssm_kernel(u_ref, m_ref, w_re_ref, w_im_ref, v_ref, at_ref, drep_ref, y_ref,
                s_re_ref, s_im_ref, h_re_ref, h_im_ref, fold_ref, ub_ref, y_scr, *, n_chunks):
    n_ch = SSM_GROUP_CH
    pairs = n_chunks // 2

    for slot in range(2):
        rows = slice(slot * n_ch, (slot + 1) * n_ch)
        _ssm_fold(u_ref.at[rows, :], fold_ref, ub_ref, n_chunks)
        ub = ub_ref[...]
        y_scr[...] = jnp.dot(ub, m_ref[slot], preferred_element_type=F32)
        s_re_ref[...] = jnp.dot(ub, w_re_ref[slot], preferred_element_type=F32)
        s_im_ref[...] = jnp.dot(ub, w_im_ref[slot], preferred_element_type=F32)
        at_r = at_ref[slot, 0]
        at_i = at_ref[slot, 1]

        def step(j, carry):
            hr, hi = carry
            for r in (j, pairs + j):
                h_re_ref[pl.ds(r, 1), :] = hr
                h_im_ref[pl.ds(r, 1), :] = hi
                sr = s_re_ref[pl.ds(r, 1), :]
                si = s_im_ref[pl.ds(r, 1), :]
                hr, hi = at_r * hr - at_i * hi + sr, at_r * hi + at_i * hr + si
            return hr, hi

        zero = jnp.zeros((1, SSM_STATE), F32)
        lax.fori_loop(0, pairs, step, (zero, zero))

        y = y_scr[...] + jnp.dot(h_re_ref[...].astype(BF16), v_ref[slot, 0],
                                 preferred_element_type=F32)
        y = y - jnp.dot(h_im_ref[...].astype(BF16), v_ref[slot, 1], preferred_element_type=F32)
        y = _gelu_tanh(y + drep_ref[slot] * ub_ref[...].astype(F32))
        _ssm_unfold(y, fold_ref, y_ref.at[rows, :], n_chunks)


def _ssm_core(u_t, ops):
    width, l = u_t.shape
    p = SSM_STATE
    c = SSM_GROUP_CH
    g = width // c
    assert 2 * CHUNK == LANES and l % (16 * LANES) == 0
    assert g % 2 == 0
    n_chunks = l // CHUNK
    w = c * CHUNK
    op_specs = [pl.BlockSpec((2,) + o.shape[1:], lambda k, n=o.ndim - 1: (k,) + (0,) * n)
                for o in ops]
    data = pl.BlockSpec((2 * c, l), lambda k: (k, 0))
    return pl.pallas_call(
        functools.partial(_ssm_kernel, n_chunks=n_chunks),
        grid=(g // 2,),
        in_specs=[data] + op_specs,
        out_specs=data,
        out_shape=jax.ShapeDtypeStruct((width, l), BF16),
        scratch_shapes=[pltpu.VMEM((n_chunks, p), F32), pltpu.VMEM((n_chunks, p), F32),
                        pltpu.VMEM((n_chunks, p), F32), pltpu.VMEM((n_chunks, p), F32),
                        pltpu.VMEM((l // LANES * c, LANES), F32),
                        pltpu.VMEM((n_chunks, w), BF16),
                        pltpu.VMEM((n_chunks, w), F32)],
        compiler_params=_cparams("parallel"),
        name="ssm_core",
    )(u_t, *ops)


def _layer_norm(y, g, b):
    mu = jnp.mean(y, axis=-1, keepdims=True)
    yc = y - mu
    var = jnp.mean(yc * yc, axis=-1, keepdims=True)
    return yc * lax.rsqrt(var + NORM_EPS) * g + b


def _glu_group_norm(yb, w, b_col, g_col):
    tn_dims = (((0,), (0,)), ((), ()))
    y = yb.astype(F32)
    z = lax.dot_general(w, yb, tn_dims, preferred_element_type=F32) + b_col
    o = y * (1.0 / (1.0 + jnp.exp(-z)))
    width, tl = o.shape
    o3 = o.reshape(width // SSM_GROUP_CH, SSM_GROUP_CH, tl)
    ms = jnp.mean(o3 * o3, axis=1, keepdims=True)
    o3 = o3 * lax.rsqrt(ms + NORM_EPS)
    return o3.reshape(width, tl) * g_col


def _out_proj_kernel(*refs):
    a_refs = refs[:NQ]
    (y_ref, glu_w_ref, glu_b_ref, sg_ref, wa_ref, ws_ref, x_ref, g_ref, b_ref,
     o_ref, ob_ref) = refs[NQ:]
    tn_dims = (((0,), (0,)), ((), ()))
    piece = pl.program_id(0) // (pl.num_programs(0) // NQ)
    attn = a_refs[NQ - 1][...]
    for c in range(NQ - 2, -1, -1):
        attn = jnp.where(piece == c, a_refs[c][...], attn)
    mix = lax.dot_general(attn, wa_ref[...], tn_dims, preferred_element_type=F32)
    ssm = _glu_group_norm(y_ref[...], glu_w_ref[...], glu_b_ref[...], sg_ref[...]).astype(BF16)
    mix = mix + lax.dot_general(ssm, ws_ref[...], tn_dims, preferred_element_type=F32)
    y = _layer_norm(DEEPNORM_ALPHA * x_ref[...] + mix, g_ref[...], b_ref[...])
    o_ref[...] = y
    ob_ref[...] = y.astype(BF16)


def _out_proj_ln(attn_pieces, y_t, glu_w_b, glu_b_col, ssm_g_col, w_out_b, x, g, b, layer, tm):
    l, d = x.shape
    a = attn_pieces[0].shape[0]
    s = y_t.shape[0]
    assert a == s
    per = l // tm // NQ

    def piece_spec(c):
        return pl.BlockSpec((a, tm), lambda i: (0, jnp.clip(i - c * per, 0, per - 1)))

    row = pl.BlockSpec((None, 1, d), lambda i: (layer, 0, 0))
    col = pl.BlockSpec((None, s, 1), lambda i: (layer, 0, 0))
    return pl.pallas_call(
        _out_proj_kernel,
        grid=(l // tm,),
        in_specs=[piece_spec(c) for c in range(NQ)] + [
                  pl.BlockSpec((s, tm), lambda i: (0, i)),
                  pl.BlockSpec((None, s, s), lambda i: (layer, 0, 0)), col, col,
                  pl.BlockSpec((None, a, d), lambda i: (0, 0, 0)),
                  pl.BlockSpec((None, s, d), lambda i: (0, 1, 0)),
                  pl.BlockSpec((tm, d), lambda i: (i, 0)), row, row],
        out_specs=[pl.BlockSpec((tm, d), lambda i: (i, 0)),
                   pl.BlockSpec((tm, d), lambda i: (i, 0))],
        out_shape=[jax.ShapeDtypeStruct((l, d), F32), jax.ShapeDtypeStruct((l, d), BF16)],
        compiler_params=_cparams("parallel"),
        name="out_proj_ln",
    )(*attn_pieces, y_t, glu_w_b, glu_b_col, ssm_g_col, w_out_b, w_out_b, x, g, b)


def _mlp_kernel(xb_ref, x_ref, wu_ref, wd_ref, g_ref, b_ref, o_ref, ob_ref, acc_ref):
    j = pl.program_id(1)

    @pl.when(j == 0)
    def _():
        acc_ref[...] = jnp.zeros(acc_ref.shape, F32)

    h = jnp.dot(xb_ref[...], wu_ref[...], preferred_element_type=F32)
    h = jnp.maximum(h, 0.0)
    h = (h * h).astype(BF16)
    acc_ref[...] += jnp.dot(h, wd_ref[...], preferred_element_type=F32)

    @pl.when(j == pl.num_programs(1) - 1)
    def _():
        y = _layer_norm(DEEPNORM_ALPHA * x_ref[...] + acc_ref[...], g_ref[...], b_ref[...])
        o_ref[...] = y
        ob_ref[...] = y.astype(BF16)


def _mlp_ln(xb, x, w_up_b, w_down_b, g, b, layer, tm, tf):
    l, d = x.shape
    f = w_up_b.shape[2]
    row = pl.BlockSpec((None, 1, d), lambda i, j: (layer, 0, 0))
    return pl.pallas_call(
        _mlp_kernel,
        grid=(l // tm, f // tf),
        in_specs=[pl.BlockSpec((tm, d), lambda i, j: (i, 0)),
                  pl.BlockSpec((tm, d), lambda i, j: (i, 0)),
                  pl.BlockSpec((None, d, tf), lambda i, j: (0, 0, j)),
                  pl.BlockSpec((None, tf, d), lambda i, j: (0, j, 0)), row, row],
        out_specs=[pl.BlockSpec((tm, d), lambda i, j: (i, 0)),
                   pl.BlockSpec((tm, d), lambda i, j: (i, 0))],
        out_shape=[jax.ShapeDtypeStruct((l, d), F32), jax.ShapeDtypeStruct((l, d), BF16)],
        scratch_shapes=[pltpu.VMEM((tm, d), F32)],
        compiler_params=_cparams("parallel", "arbitrary"),
        name="mlp_ln",
    )(xb, x, w_up_b, w_down_b, g, b)


def _pick(n, pref):
    while n % pref:
        pref //= 2
    return pref


def kernel(x, w_in, lambda_q1, lambda_k1, lambda_q2, lambda_k2, attn_norm_g, ssm_lambda_re, ssm_lambda_im, ssm_log_dt, ssm_b_re, ssm_b_im, ssm_c_re, ssm_c_im, ssm_d, glu_w, glu_b, ssm_norm_g, w_out, ln1_g, ln1_b, w_up, w_down, ln2_g, ln2_b):
    bsz, seq, d = x.shape
    depth = w_in.shape[0]
    attn_w = attn_norm_g.shape[1]
    ssm_w = ssm_d.shape[1]
    n_groups = ssm_w // SSM_GROUP_CH
    assert bsz == 1 and seq % CHUNK == 0
    n_chunks = seq // CHUNK
    p = SSM_STATE
    c = SSM_GROUP_CH

    w_in_b, w_vu_t = _prep_w_in(w_in, 2 * attn_w)
    glu_w_b = glu_w.astype(BF16)
    lq1 = lambda_q1.reshape(depth, 1, HEAD_DIM)
    lk1 = lambda_k1.reshape(depth, 1, HEAD_DIM)
    lq2 = lambda_q2.reshape(depth, 1, HEAD_DIM)
    lk2 = lambda_k2.reshape(depth, 1, HEAD_DIM)
    attn_g_col = attn_norm_g.reshape(depth, attn_w, 1)
    ldt = ssm_log_dt.reshape(depth, n_groups, 1, 1)
    lam_re = ssm_lambda_re.reshape(depth, n_groups, 1, p)
    lam_im = ssm_lambda_im.reshape(depth, n_groups, 1, p)
    bt_re = ssm_b_re.transpose(0, 1, 3, 2)
    bt_im = ssm_b_im.transpose(0, 1, 3, 2)
    ct_re = ssm_c_re.transpose(0, 1, 3, 2)
    ct_im = ssm_c_im.transpose(0, 1, 3, 2)
    d_skip = ssm_d.reshape(depth, n_groups, 1, c)
    ssm_params = (ldt, lam_re, lam_im, bt_re, bt_im, ct_re, ct_im, d_skip)
    glu_b_col = glu_b.reshape(depth, ssm_w, 1)
    ssm_g_col = ssm_norm_g.reshape(depth, ssm_w, 1)
    ln1g, ln1b = ln1_g.reshape(depth, 1, d), ln1_b.reshape(depth, 1, d)
    ln2g, ln2b = ln2_g.reshape(depth, 1, d), ln2_b.reshape(depth, 1, d)

    blk = _pick(seq, ATTN_BLOCK)
    xf = x.reshape(seq, d)
    xb = xf.astype(BF16)
    for l in range(depth):
        lam_init = 0.8 - 0.6 * math.exp(-0.3 * l)
        qk = _proj_qk(xb, w_in_b, l, 2 * attn_w, attn_w, _pick(seq, 1024), 2 * attn_w)
        v_t, u_t, ssm_ops = _proj_vu(w_vu_t, xb, l, attn_w, _pick(seq, PROJ_VU_TILE), ssm_params)

        attn_pieces, (w_out_b, w_up_b, w_down_b) = _diff_attention(
            qk, v_t, lq1, lk1, lq2, lk2, attn_g_col, l, lam_init, blk, (w_out, w_up, w_down))

        y_t = _ssm_core(u_t, ssm_ops)
        xf, xb = _out_proj_ln(attn_pieces, y_t, glu_w_b, glu_b_col, ssm_g_col, w_out_b, xf,
                              ln1g, ln1b, l, _pick(seq // NQ, 512))
        xf, xb = _mlp_ln(xb, xf, w_up_b, w_down_b, ln2g, ln2b, l, _pick(seq, 512), 1024)
    return xf.reshape(bsz, seq, d)
```

```python
import functools
import math

import jax
import jax.numpy as jnp
from jax import lax
from jax.experimental import pallas as pl
from jax.experimental.pallas import tpu as pltpu

F32 = jnp.float32
BF16 = jnp.bfloat16

DEPTH = 4
HEAD_DIM = 64
HEAD_W = 2 * HEAD_DIM
CHUNK = 64
SSM_GROUP_CH = 16
SSM_STATE = 64
DEEPNORM_ALPHA = (2.0 * DEPTH) ** 0.25
NORM_EPS = 1e-5
MASK_VALUE = -1e30
QK_SCALE_LOG2E = HEAD_DIM ** -0.5 * math.log2(math.e)
ONES_ROWS = 16
UNROLL = 10
NQ = 8
ATTN_BLOCK = 512
PROJ_VU_TILE = 1024
LANES = 128
VMEM_LIMIT = 56 * 1024 * 1024


def _cparams(*sem):
    return pltpu.CompilerParams(dimension_semantics=sem, vmem_limit_bytes=VMEM_LIMIT)


def _w_in_kernel(qk_ref, vu_ref, oqk_ref, ovu_ref):
    oqk_ref[...] = qk_ref[...].astype(oqk_ref.dtype)
    ovu_ref[...] = vu_ref[...].T.astype(ovu_ref.dtype)


def _prep_w_in(w_in, n_qk):
    depth, d, cols = w_in.shape
    n_vu = cols - n_qk
    assert n_vu == n_qk
    tr = _pick(d, 256)
    return pl.pallas_call(
        _w_in_kernel,
        grid=(depth, d // tr),
        in_specs=[pl.BlockSpec((None, tr, n_qk), lambda l, r: (l, r, 0)),
                  pl.BlockSpec((None, tr, n_vu), lambda l, r: (l, r, 1))],
        out_specs=[pl.BlockSpec((None, tr, n_qk), lambda l, r: (l, r, 0)),
                   pl.BlockSpec((None, n_vu, tr), lambda l, r: (l, 0, r))],
        out_shape=[jax.ShapeDtypeStruct((depth, d, n_qk), BF16),
                   jax.ShapeDtypeStruct((depth, n_vu, d), BF16)],
        compiler_params=_cparams("parallel", "parallel"),
        name="prep_w_in",
    )(w_in, w_in)


def _mm_qk_kernel(a_ref, b_ref, o_ref, *, q_cols):
    acc = jnp.dot(a_ref[...], b_ref[...], preferred_element_type=F32)
    tn = o_ref.shape[1]
    col = pl.program_id(1) * tn + lax.broadcasted_iota(jnp.int32, (1, tn), 1)
    scale = jnp.where(col < q_cols, QK_SCALE_LOG2E, 1.0)
    o_ref[...] = (acc * scale).astype(o_ref.dtype)


def _proj_qk(xb, w_in_b, layer, n_cols, q_cols, tm, tn):
    m, k = xb.shape
    return pl.pallas_call(
        functools.partial(_mm_qk_kernel, q_cols=q_cols),
        grid=(m // tm, n_cols // tn),
        in_specs=[pl.BlockSpec((tm, k), lambda i, j: (i, 0)),
                  pl.BlockSpec((None, k, tn), lambda i, j: (layer, 0, j))],
        out_specs=pl.BlockSpec((tm, tn), lambda i, j: (i, j)),
        out_shape=jax.ShapeDtypeStruct((m, n_cols), BF16),
        compiler_params=_cparams("parallel", "arbitrary"),
        name="proj_qk",
    )(xb, w_in_b)


def _mm_vu_kernel(*refs, gps):
    w_ref, x_ref = refs[:2]
    prm = refs[2:10]
    v_ref, u_ref = refs[10:12]
    ops = refs[12:]
    nt_dims = (((1,), (1,)), ((), ()))
    rows_v = v_ref.shape[0]
    x = x_ref[...]
    v_ref[...] = lax.dot_general(w_ref[:rows_v, :], x, nt_dims,
                                 preferred_element_type=F32).astype(v_ref.dtype)
    u_ref[...] = lax.dot_general(w_ref[rows_v:, :], x, nt_dims,
                                 preferred_element_type=F32).astype(u_ref.dtype)
    for g in range(gps):
        _ssm_build(tuple(r.at[g] for r in prm), *(r.at[g] for r in ops))


def _proj_vu(w_vu_t, xb, layer, rows_v, tl, ssm_params):
    _, r, k = w_vu_t.shape
    l = xb.shape[0]
    rows_u = r - rows_v
    p = SSM_STATE
    c = SSM_GROUP_CH
    w = c * CHUNK
    n_steps = l // tl
    g = ssm_params[0].shape[1]
    assert g % n_steps == 0
    gps = g // n_steps
    prm_shapes = [(1, 1), (1, p), (1, p), (c, p), (c, p), (p, c), (p, c), (1, c)]
    prm_specs = [pl.BlockSpec((None, gps) + shp, lambda i: (layer, i, 0, 0)) for shp in prm_shapes]
    op_shapes = [((w, w), BF16), ((w, p), BF16), ((w, p), BF16), ((2, p, w), BF16),
                 ((2, 1, p), F32), ((1, w), F32)]
    op_specs = [pl.BlockSpec((gps,) + shp, lambda i, n=len(shp): (i,) + (0,) * n)
                for shp, _ in op_shapes]
    op_sds = [jax.ShapeDtypeStruct((g,) + shp, dt) for shp, dt in op_shapes]
    outs = pl.pallas_call(
        functools.partial(_mm_vu_kernel, gps=gps),
        grid=(n_steps,),
        in_specs=[pl.BlockSpec((None, r, k), lambda i: (layer, 0, 0)),
                  pl.BlockSpec((tl, k), lambda i: (i, 0))] + prm_specs,
        out_specs=[pl.BlockSpec((rows_v, tl), lambda i: (0, i)),
                   pl.BlockSpec((rows_u, tl), lambda i: (0, i))] + op_specs,
        out_shape=[jax.ShapeDtypeStruct((rows_v, l), BF16),
                   jax.ShapeDtypeStruct((rows_u, l), BF16)] + op_sds,
        compiler_params=_cparams("parallel"),
        name="proj_vu",
    )(w_vu_t, xb, *ssm_params)
    return outs[0], outs[1], tuple(outs[2:])


def _query_blocks(i, n_qblk):
    per = n_qblk // NQ
    blocks = []
    for j in range(NQ // 2):
        blocks += [2 * j * per + i, (2 * j + 2) * per - 1 - i]
    return tuple(blocks)


def _attn_kernel(*refs, blk, n_qblk, lam_init, n_cast):
    lq1_ref, lk1_ref, lq2_ref, lk2_ref = refs[:4]
    q_refs = refs[4:4 + NQ]
    k_ref, vt_ref, g_ref = refs[4 + NQ:7 + NQ]
    n_in = 7 + NQ + n_cast
    cast_src = refs[7 + NQ:n_in]
    o_refs = refs[n_in:n_in + NQ]
    cast_dst = refs[n_in + NQ:n_in + NQ + n_cast]
    qz_ref, s_ref, mb_ref, m_ref, acc_ref = refs[n_in + NQ + n_cast:]
    for src, dst in zip(cast_src, cast_dst):
        dst[...] = src[...].astype(dst.dtype)
    i = pl.program_id(1)
    q_blocks = _query_blocks(i, n_qblk)
    starts = [0]
    for b in q_blocks[:-1]:
        starts.append(starts[-1] + b)
    n_items = NQ + NQ // 2 * (n_qblk - 1)
    nt_dims = (((1,), (1,)), ((), ()))

    lane = lax.broadcasted_iota(jnp.int32, (blk, HEAD_W), 1)
    for sel, q_ref in enumerate(q_refs):
        q = q_ref[...]
        zero = jnp.zeros_like(q)
        qz_ref[sel, 0] = jnp.where(lane < HEAD_DIM, q, zero)
        qz_ref[sel, 1] = jnp.where(lane >= HEAD_DIM, q, zero)
    ones = jnp.ones((ONES_ROWS, blk), BF16)

    def stage_a(block, sel, slot, diagonal):
        st = pl.multiple_of(block * blk, blk)
        kb = k_ref[pl.ds(st, blk), :]
        for mp in range(2):
            s = lax.dot_general(kb, qz_ref[sel, mp], nt_dims, preferred_element_type=F32)
            if diagonal:
                key_chunk = lax.broadcasted_iota(jnp.int32, (blk, blk), 0) // CHUNK
                qry_chunk = lax.broadcasted_iota(jnp.int32, (blk, blk), 1) // CHUNK
                s = jnp.where(key_chunk <= qry_chunk, s, MASK_VALUE)
            s_ref[slot, mp] = s
            mb_ref[slot, mp] = jnp.max(s, axis=0, keepdims=True)

    def stage_b(block, sel, slot, first):
        st = pl.multiple_of(block * blk, blk)
        v_ext = jnp.concatenate([vt_ref[:, pl.ds(st, blk)], ones], axis=0)
        for mp in range(2):
            if first:
                m_new = mb_ref[slot, mp]
            else:
                m_old = m_ref[sel, mp]
                m_new = jnp.maximum(m_old, mb_ref[slot, mp])
            p = jnp.exp2(s_ref[slot, mp] - m_new).astype(BF16)
            pv = jnp.dot(v_ext, p, preferred_element_type=F32)
            if first:
                acc_ref[sel, mp] = pv
            else:
                acc_ref[sel, mp] = jnp.exp2(m_old - m_new) * acc_ref[sel, mp] + pv
            m_ref[sel, mp] = m_new

    def item(k):
        if isinstance(k, int) and k < NQ:
            return q_blocks[k], k, True
        idx = k - NQ
        sel = sum((idx >= st).astype(jnp.int32) for st in starts[1:])
        first = starts[NQ - 1]
        for j in range(NQ - 2, -1, -1):
            first = jnp.where(sel == j, starts[j], first)
        return idx - first, sel, False

    def run_items(base, count):
        for q in range(count):
            nb, ns, nd = item(base + q + 1)
            stage_a(nb, ns, q % 2, nd)
            cb, cs, cd = item(base + q)
            stage_b(cb, cs, (q + 1) % 2, cd)

    stage_a(q_blocks[0], 0, 0, True)
    nb, ns, nd = item(1)
    stage_a(nb, ns, 1, nd)
    stage_b(q_blocks[0], 0, 0, True)
    run_items(1, UNROLL)

    def trip(t, carry):
        run_items(1 + UNROLL * t, UNROLL)
        return carry

    lax.fori_loop(1, (n_items - 2) // UNROLL, trip, 0)
    b_last, s_last, _ = item(n_items - 1)
    stage_b(b_last, s_last, (n_items - 1) % 2, False)

    lam = (jnp.exp(jnp.sum(lq1_ref[...] * lk1_ref[...], axis=1, keepdims=True))
           - jnp.exp(jnp.sum(lq2_ref[...] * lk2_ref[...], axis=1, keepdims=True))
           + lam_init)
    for sel, o_ref in enumerate(o_refs):
        o = (acc_ref[sel, 0, :HEAD_W, :] / acc_ref[sel, 0, HEAD_W:HEAD_W + 1, :]
             - lam * (acc_ref[sel, 1, :HEAD_W, :] / acc_ref[sel, 1, HEAD_W:HEAD_W + 1, :]))
        ms = jnp.mean(o * o, axis=0, keepdims=True)
        o = o * lax.rsqrt(ms + NORM_EPS) * g_ref[...] * (1.0 - lam_init)
        o_ref[...] = o.astype(o_ref.dtype)


def _diff_attention(qk, v_t, lq1, lk1, lq2, lk2, g_col, layer, lam_init, blk, cast_srcs):
    l = qk.shape[0]
    a = v_t.shape[0]
    n_heads = a // HEAD_W
    n_qblk = l // blk
    assert NQ % 2 == 0 and UNROLL % 2 == 0 and n_qblk % NQ == 0
    assert UNROLL >= NQ - 1
    assert (NQ + NQ // 2 * (n_qblk - 1) - 2) % UNROLL == 0
    per = n_qblk // NQ
    lam_spec = pl.BlockSpec((None, 1, HEAD_DIM), lambda h, i: (layer, 0, 0))
    out_sds = jax.ShapeDtypeStruct((a, l // NQ), BF16)

    def q_spec(sel):
        return pl.BlockSpec((blk, HEAD_W), lambda h, i: (_query_blocks(i, n_qblk)[sel], h))

    def o_spec(sel):
        return pl.BlockSpec((HEAD_W, blk),
                            lambda h, i: (h, _query_blocks(i, n_qblk)[sel] - sel * per))

    steps = n_heads * per
    cast_in, cast_out, cast_sds = [], [], []
    for w in cast_srcs:
        _, rows, cols = w.shape
        assert rows % (16 * steps) == 0
        slab = (None, rows // steps, cols)
        cast_in.append(pl.BlockSpec(slab, lambda h, i: (layer, h * per + i, 0)))
        cast_out.append(pl.BlockSpec(slab, lambda h, i: (0, h * per + i, 0)))
        cast_sds.append(jax.ShapeDtypeStruct((1, rows, cols), BF16))

    outs = pl.pallas_call(
        functools.partial(_attn_kernel, blk=blk, n_qblk=n_qblk, lam_init=lam_init,
                          n_cast=len(cast_srcs)),
        grid=(n_heads, per),
        in_specs=[lam_spec, lam_spec, lam_spec, lam_spec] + [q_spec(sel) for sel in range(NQ)] + [
            pl.BlockSpec((l, HEAD_W), lambda h, i: (0, n_heads + h)),
            pl.BlockSpec((HEAD_W, l), lambda h, i: (h, 0)),
            pl.BlockSpec((None, HEAD_W, 1), lambda h, i: (layer, h, 0))] + cast_in,
        out_specs=[o_spec(sel) for sel in range(NQ)] + cast_out,
        out_shape=[out_sds] * NQ + cast_sds,
        scratch_shapes=[pltpu.VMEM((NQ, 2, blk, HEAD_W), BF16),
                        pltpu.VMEM((2, 2, blk, blk), F32),
                        pltpu.VMEM((2, 2, 1, blk), F32),
                        pltpu.VMEM((NQ, 2, 1, blk), F32),
                        pltpu.VMEM((NQ, 2, HEAD_W + ONES_ROWS, blk), F32)],
        compiler_params=_cparams("parallel", "arbitrary"),
        name="diff_attn",
    )(lq1, lk1, lq2, lk2, *([qk] * (NQ + 1)), v_t, g_col, *cast_srcs)
    return outs[:NQ], outs[NQ:]


def _complex_pow(ar, ai, e, n_bits):
    shape = jnp.broadcast_shapes(ar.shape, e.shape)
    pr = jnp.ones(shape, F32)
    pi = jnp.zeros(shape, F32)
    fr, fi = ar, ai
    for b in range(n_bits):
        bit = ((e >> b) & 1) == 1
        nr = pr * fr - pi * fi
        ni = pr * fi + pi * fr
        pr = jnp.where(bit, nr, pr)
        pi = jnp.where(bit, ni, pi)
        if b + 1 < n_bits:
            fr, fi = fr * fr - fi * fi, 2.0 * fr * fi
    return pr, pi


def _gelu_tanh(y):
    k0 = math.sqrt(2.0 / math.pi)
    return 0.5 * y * (1.0 + jnp.tanh(k0 * (y + 0.044715 * (y * y * y))))


def _spread_channels(x, t):
    n_ch = x.shape[1]
    expand = (lax.broadcasted_iota(jnp.int32, (n_ch, n_ch * t), 1) // t
              == lax.broadcasted_iota(jnp.int32, (n_ch, n_ch * t), 0)).astype(BF16)
    out = None
    rest = x
    for _ in range(3):
        piece = rest.astype(BF16)
        rest = rest - piece.astype(F32)
        part = jnp.dot(piece, expand, preferred_element_type=F32)
        out = part if out is None else out + part
    return out


def _dot_split(a, b):
    a_hi = a.astype(BF16)
    b_hi = b.astype(BF16)
    a_lo = (a - a_hi.astype(F32)).astype(BF16)
    b_lo = (b - b_hi.astype(F32)).astype(BF16)
    small = (jnp.dot(a_hi, b_lo, preferred_element_type=F32)
             + jnp.dot(a_lo, b_hi, preferred_element_type=F32))
    return jnp.dot(a_hi, b_hi, preferred_element_type=F32) + small


def _row_to_col(row):
    n = row.shape[1]
    eye = (lax.broadcasted_iota(jnp.int32, (n, n), 0)
           == lax.broadcasted_iota(jnp.int32, (n, n), 1))
    return jnp.sum(jnp.where(eye, jnp.broadcast_to(row, (n, n)), 0.0), axis=1, keepdims=True)


def _ssm_build(prm, m_ref, w_re_ref, w_im_ref, v_ref, at_ref, drep_ref):
    ldt_ref, lr_ref, li_ref, bt_re_ref, bt_im_ref, ct_re_ref, ct_im_ref, d_ref = prm
    t = CHUNK
    n_ch = SSM_GROUP_CH
    w = n_ch * t
    n_bits = t.bit_length() - 1
    dt = jnp.exp(ldt_ref[...])

    lr = lr_ref[...]
    li = li_ref[...]
    mag = jnp.exp(lr * dt)
    ar = mag * jnp.cos(li * dt)
    ai = mag * jnp.sin(li * dt)
    den = lr * lr + li * li
    nr = ar - 1.0
    fr = (nr * lr + ai * li) / den
    fi = (ai * lr - nr * li) / den
    bbt_r = fr * bt_re_ref[...] - fi * bt_im_ref[...]
    bbt_i = fr * bt_im_ref[...] + fi * bt_re_ref[...]

    rev = (t - 1) - lax.broadcasted_iota(jnp.int32, (t, 1), 0)
    pr, pi = _complex_pow(ar, ai, rev, n_bits)
    for c in range(n_ch):
        br = bbt_r[c:c + 1, :]
        bi = bbt_i[c:c + 1, :]
        w_re_ref[c * t:(c + 1) * t, :] = (pr * br - pi * bi).astype(BF16)
        w_im_ref[c * t:(c + 1) * t, :] = (pr * bi + pi * br).astype(BF16)
    at_r, at_i = _complex_pow(ar, ai, jnp.full((1, 1), t, jnp.int32), n_bits + 1)
    at_ref[0] = at_r
    at_ref[1] = at_i

    ar_c = _row_to_col(ar)
    ai_c = _row_to_col(ai)
    tau_tile = lax.broadcasted_iota(jnp.int32, (1, LANES), 1) % t
    qr, qi = _complex_pow(ar_c, ai_c, tau_tile, n_bits)
    qr = jnp.concatenate([qr] * (w // LANES), axis=1)
    qi = jnp.concatenate([qi] * (w // LANES), axis=1)
    c_re = _spread_channels(ct_re_ref[...], t)
    c_im = _spread_channels(ct_im_ref[...], t)
    drep_ref[...] = _spread_channels(jnp.broadcast_to(d_ref[...], (8, n_ch)), t)[0:1, :]
    ca_r = c_re * qr - c_im * qi
    ca_i = c_re * qi + c_im * qr
    z = _dot_split(jnp.concatenate([bbt_r, -bbt_i], axis=1),
                   jnp.concatenate([ca_r, ca_i], axis=0))
    v_ref[0] = (ca_r * ar_c - ca_i * ai_c).astype(BF16)
    v_ref[1] = (ca_r * ai_c + ca_i * ar_c).astype(BF16)

    tau = lax.broadcasted_iota(jnp.int32, (1, w), 1) % t
    s_row = lax.broadcasted_iota(jnp.int32, (t, 1), 0)
    causal = tau >= s_row
    for c in range(n_ch):
        strip = jnp.broadcast_to(z[c:c + 1, :], (t, w))
        shifted = pltpu.roll(strip, 0, 1, stride=1, stride_axis=0)
        m_ref[c * t:(c + 1) * t, :] = jnp.where(causal, shifted, 0.0).astype(BF16)


def _ssm_fold(u_ref, fold_ref, ub_ref, n_chunks):
    t = CHUNK
    n_ch = SSM_GROUP_CH
    pairs = n_chunks // 2
    half_lane = lax.broadcasted_iota(jnp.int32, (pairs, LANES), 1) < t
    for j in range(pairs):
        fold_ref[j * n_ch:(j + 1) * n_ch, :] = u_ref[:, j * LANES:(j + 1) * LANES].astype(F32)
    for a in range(n_ch // 2):
        lo = fold_ref[pl.ds(2 * a, pairs, stride=n_ch), :]
        hi = fold_ref[pl.ds(2 * a + 1, pairs, stride=n_ch), :]
        even = jnp.where(half_lane, lo, pltpu.roll(hi, t, 1))
        odd = jnp.where(half_lane, pltpu.roll(lo, t, 1), hi)
        ub_ref[0:pairs, a * LANES:(a + 1) * LANES] = even.astype(BF16)
        ub_ref[pairs:n_chunks, a * LANES:(a + 1) * LANES] = odd.astype(BF16)


def _ssm_unfold(y, fold_ref, y_ref, n_chunks):
    t = CHUNK
    n_ch = SSM_GROUP_CH
    pairs = n_chunks // 2
    half_lane = lax.broadcasted_iota(jnp.int32, (pairs, LANES), 1) < t
    for a in range(n_ch // 2):
        even = y[0:pairs, a * LANES:(a + 1) * LANES]
        odd = y[pairs:n_chunks, a * LANES:(a + 1) * LANES]
        fold_ref[pl.ds(2 * a, pairs, stride=n_ch), :] = jnp.where(
            half_lane, even, pltpu.roll(odd, t, 1))
        fold_ref[pl.ds(2 * a + 1, pairs, stride=n_ch), :] = jnp.where(
            half_lane, pltpu.roll(even, t, 1), odd)
    for j in range(pairs):
        y_ref[:, j * LANES:(j + 1) * LANES] = fold_ref[j * n_ch:(j + 1) * n_ch, :].astype(y_ref.dtype)


def _ssm_kernel(u_ref, m_ref, w_re_ref, w_im_ref, v_ref, at_ref, drep_ref, y_ref,
                s_re_ref, s_im_ref, h_re_ref, h_im_ref, fold_ref, ub_ref, y_scr, *, n_chunks):
    n_ch = SSM_GROUP_CH
    pairs = n_chunks // 2

    for slot in range(2):
        rows = slice(slot * n_ch, (slot + 1) * n_ch)
        _ssm_fold(u_ref.at[rows, :], fold_ref, ub_ref, n_chunks)
        ub = ub_ref[...]
        y_scr[...] = jnp.dot(ub, m_ref[slot], preferred_element_type=F32)
        s_re_ref[...] = jnp.dot(ub, w_re_ref[slot], preferred_element_type=F32)
        s_im_ref[...] = jnp.dot(ub, w_im_ref[slot], preferred_element_type=F32)
        at_r = at_ref[slot, 0]
        at_i = at_ref[slot, 1]

        def step(j, carry):
            hr, hi = carry
            for r in (j, pairs + j):
                h_re_ref[pl.ds(r, 1), :] = hr
                h_im_ref[pl.ds(r, 1), :] = hi
                sr = s_re_ref[pl.ds(r, 1), :]
                si = s_im_ref[pl.ds(r, 1), :]
                hr, hi = at_r * hr - at_i * hi + sr, at_r * hi + at_i * hr + si
            return hr, hi

        zero = jnp.zeros((1, SSM_STATE), F32)
        lax.fori_loop(0, pairs, step, (zero, zero))

        y = y_scr[...] + jnp.dot(h_re_ref[...].astype(BF16), v_ref[slot, 0],
                                 preferred_element_type=F32)
        y = y - jnp.dot(h_im_ref[...].astype(BF16), v_ref[slot, 1], preferred_element_type=F32)
        y = _gelu_tanh(y + drep_ref[slot] * ub_ref[...].astype(F32))
        _ssm_unfold(y, fold_ref, y_ref.at[rows, :], n_chunks)


def _ssm_core(u_t, ops):
    width, l = u_t.shape
    p = SSM_STATE
    c = SSM_GROUP_CH
    g = width // c
    assert 2 * CHUNK == LANES and l % (16 * LANES) == 0
    assert g % 2 == 0
    n_chunks = l // CHUNK
    w = c * CHUNK
    op_specs = [pl.BlockSpec((2,) + o.shape[1:], lambda k, n=o.ndim - 1: (k,) + (0,) * n)
                for o in ops]
    data = pl.BlockSpec((2 * c, l), lambda k: (k, 0))
    return pl.pallas_call(
        functools.partial(_ssm_kernel, n_chunks=n_chunks),
        grid=(g // 2,),
        in_specs=[data] + op_specs,
        out_specs=data,
        out_shape=jax.ShapeDtypeStruct((width, l), BF16),
        scratch_shapes=[pltpu.VMEM((n_chunks, p), F32), pltpu.VMEM((n_chunks, p), F32),
                        pltpu.VMEM((n_chunks, p), F32), pltpu.VMEM((n_chunks, p), F32),
                        pltpu.VMEM((l // LANES * c, LANES), F32),
                        pltpu.VMEM((n_chunks, w), BF16),
                        pltpu.VMEM((n_chunks, w), F32)],
        compiler_params=_cparams("parallel"),
        name="ssm_core",
    )(u_t, *ops)


def _layer_norm(y, g, b):
    mu = jnp.mean(y, axis=-1, keepdims=True)
    yc = y - mu
    var = jnp.mean(yc * yc, axis=-1, keepdims=True)
    return yc * lax.rsqrt(var + NORM_EPS) * g + b


def _glu_group_norm(yb, w, b_col, g_col):
    tn_dims = (((0,), (0,)), ((), ()))
    y = yb.astype(F32)
    z = lax.dot_general(w, yb, tn_dims, preferred_element_type=F32) + b_col
    o = y * (1.0 / (1.0 + jnp.exp(-z)))
    width, tl = o.shape
    o3 = o.reshape(width // SSM_GROUP_CH, SSM_GROUP_CH, tl)
    ms = jnp.mean(o3 * o3, axis=1, keepdims=True)
    o3 = o3 * lax.rsqrt(ms + NORM_EPS)
    return o3.reshape(width, tl) * g_col


def _out_proj_kernel(*refs):
    a_refs = refs[:NQ]
    (y_ref, glu_w_ref, glu_b_ref, sg_ref, wa_ref, ws_ref, x_ref, g_ref, b_ref,
     o_ref, ob_ref) = refs[NQ:]
    tn_dims = (((0,), (0,)), ((), ()))
    piece = pl.program_id(0) // (pl.num_programs(0) // NQ)
    attn = a_refs[NQ - 1][...]
    for c in range(NQ - 2, -1, -1):
        attn = jnp.where(piece == c, a_refs[c][...], attn)
    mix = lax.dot_general(attn, wa_ref[...], tn_dims, preferred_element_type=F32)
    ssm = _glu_group_norm(y_ref[...], glu_w_ref[...], glu_b_ref[...], sg_ref[...]).astype(BF16)
    mix = mix + lax.dot_general(ssm, ws_ref[...], tn_dims, preferred_element_type=F32)
    y = _layer_norm(DEEPNORM_ALPHA * x_ref[...] + mix, g_ref[...], b_ref[...])
    o_ref[...] = y
    ob_ref[...] = y.astype(BF16)


def _out_proj_ln(attn_pieces, y_t, glu_w_b, glu_b_col, ssm_g_col, w_out_b, x, g, b, layer, tm):
    l, d = x.shape
    a = attn_pieces[0].shape[0]
    s = y_t.shape[0]
    assert a == s
    per = l // tm // NQ

    def piece_spec(c):
        return pl.BlockSpec((a, tm), lambda i: (0, jnp.clip(i - c * per, 0, per - 1)))

    row = pl.BlockSpec((None, 1, d), lambda i: (layer, 0, 0))
    col = pl.BlockSpec((None, s, 1), lambda i: (layer, 0, 0))
    return pl.pallas_call(
        _out_proj_kernel,
        grid=(l // tm,),
        in_specs=[piece_spec(c) for c in range(NQ)] + [
                  pl.BlockSpec((s, tm), lambda i: (0, i)),
                  pl.BlockSpec((None, s, s), lambda i: (layer, 0, 0)), col, col,
                  pl.BlockSpec((None, a, d), lambda i: (0, 0, 0)),
                  pl.BlockSpec((None, s, d), lambda i: (0, 1, 0)),
                  pl.BlockSpec((tm, d), lambda i: (i, 0)), row, row],
        out_specs=[pl.BlockSpec((tm, d), lambda i: (i, 0)),
                   pl.BlockSpec((tm, d), lambda i: (i, 0))],
        out_shape=[jax.ShapeDtypeStruct((l, d), F32), jax.ShapeDtypeStruct((l, d), BF16)],
        compiler_params=_cparams("parallel"),
        name="out_proj_ln",
    )(*attn_pieces, y_t, glu_w_b, glu_b_col, ssm_g_col, w_out_b, w_out_b, x, g, b)


def _mlp_kernel(xb_ref, x_ref, wu_ref, wd_ref, g_ref, b_ref, o_ref, ob_ref, acc_ref):
    j = pl.program_id(1)

    @pl.when(j == 0)
    def _():
        acc_ref[...] = jnp.zeros(acc_ref.shape, F32)

    h = jnp.dot(xb_ref[...], wu_ref[...], preferred_element_type=F32)
    h = jnp.maximum(h, 0.0)
    h = (h * h).astype(BF16)
    acc_ref[...] += jnp.dot(h, wd_ref[...], preferred_element_type=F32)

    @pl.when(j == pl.num_programs(1) - 1)
    def _():
        y = _layer_norm(DEEPNORM_ALPHA * x_ref[...] + acc_ref[...], g_ref[...], b_ref[...])
        o_ref[...] = y
        ob_ref[...] = y.astype(BF16)


def _mlp_ln(xb, x, w_up_b, w_down_b, g, b, layer, tm, tf):
    l, d = x.shape
    f = w_up_b.shape[2]
    row = pl.BlockSpec((None, 1, d), lambda i, j: (layer, 0, 0))
    return pl.pallas_call(
        _mlp_kernel,
        grid=(l // tm, f // tf),
        in_specs=[pl.BlockSpec((tm, d), lambda i, j: (i, 0)),
                  pl.BlockSpec((tm, d), lambda i, j: (i, 0)),
                  pl.BlockSpec((None, d, tf), lambda i, j: (0, 0, j)),
                  pl.BlockSpec((None, tf, d), lambda i, j: (0, j, 0)), row, row],
        out_specs=[pl.BlockSpec((tm, d), lambda i, j: (i, 0)),
                   pl.BlockSpec((tm, d), lambda i, j: (i, 0))],
        out_shape=[jax.ShapeDtypeStruct((l, d), F32), jax.ShapeDtypeStruct((l, d), BF16)],
        scratch_shapes=[pltpu.VMEM((tm, d), F32)],
        compiler_params=_cparams("parallel", "arbitrary"),
        name="mlp_ln",
    )(xb, x, w_up_b, w_down_b, g, b)


def _pick(n, pref):
    while n % pref:
        pref //= 2
    return pref


def kernel(x, w_in, lambda_q1, lambda_k1, lambda_q2, lambda_k2, attn_norm_g, ssm_lambda_re, ssm_lambda_im, ssm_log_dt, ssm_b_re, ssm_b_im, ssm_c_re, ssm_c_im, ssm_d, glu_w, glu_b, ssm_norm_g, w_out, ln1_g, ln1_b, w_up, w_down, ln2_g, ln2_b):
    bsz, seq, d = x.shape
    depth = w_in.shape[0]
    attn_w = attn_norm_g.shape[1]
    ssm_w = ssm_d.shape[1]
    n_groups = ssm_w // SSM_GROUP_CH
    assert bsz == 1 and seq % CHUNK == 0
    n_chunks = seq // CHUNK
    p = SSM_STATE
    c = SSM_GROUP_CH

    w_in_b, w_vu_t = _prep_w_in(w_in, 2 * attn_w)
    glu_w_b = glu_w.astype(BF16)
    lq1 = lambda_q1.reshape(depth, 1, HEAD_DIM)
    lk1 = lambda_k1.reshape(depth, 1, HEAD_DIM)
    lq2 = lambda_q2.reshape(depth, 1, HEAD_DIM)
    lk2 = lambda_k2.reshape(depth, 1, HEAD_DIM)
    attn_g_col = attn_norm_g.reshape(depth, attn_w, 1)
    ldt = ssm_log_dt.reshape(depth, n_groups, 1, 1)
    lam_re = ssm_lambda_re.reshape(depth, n_groups, 1, p)
    lam_im = ssm_lambda_im.reshape(depth, n_groups, 1, p)
    bt_re = ssm_b_re.transpose(0, 1, 3, 2)
    bt_im = ssm_b_im.transpose(0, 1, 3, 2)
    ct_re = ssm_c_re.transpose(0, 1, 3, 2)
    ct_im = ssm_c_im.transpose(0, 1, 3, 2)
    d_skip = ssm_d.reshape(depth, n_groups, 1, c)
    ssm_params = (ldt, lam_re, lam_im, bt_re, bt_im, ct_re, ct_im, d_skip)
    glu_b_col = glu_b.reshape(depth, ssm_w, 1)
    ssm_g_col = ssm_norm_g.reshape(depth, ssm_w, 1)
    ln1g, ln1b = ln1_g.reshape(depth, 1, d), ln1_b.reshape(depth, 1, d)
    ln2g, ln2b = ln2_g.reshape(depth, 1, d), ln2_b.reshape(depth, 1, d)

    blk = _pick(seq, ATTN_BLOCK)
    xf = x.reshape(seq, d)
    xb = xf.astype(BF16)
    for l in range(depth):
        lam_init = 0.8 - 0.6 * math.exp(-0.3 * l)
        qk = _proj_qk(xb, w_in_b, l, 2 * attn_w, attn_w, _pick(seq, 1024), 2 * attn_w)
        v_t, u_t, ssm_ops = _proj_vu(w_vu_t, xb, l, attn_w, _pick(seq, PROJ_VU_TILE), ssm_params)

        attn_pieces, (w_out_b, w_up_b, w_down_b) = _diff_attention(
            qk, v_t, lq1, lk1, lq2, lk2, attn_g_col, l, lam_init, blk, (w_out, w_up, w_down))

        y_t = _ssm_core(u_t, ssm_ops)
        xf, xb = _out_proj_ln(attn_pieces, y_t, glu_w_b, glu_b_col, ssm_g_col, w_out_b, xf,
                              ln1g, ln1b, l, _pick(seq // NQ, 512))
        xf, xb = _mlp_ln(xb, xf, w_up_b, w_down_b, ln2g, ln2b, l, _pick(seq, 512), 1024)
    return xf.reshape(bsz, seq, d)
```
